```python
import math
import jax, jax.numpy as jnp
from jax import lax
import numpy as np

D_MODEL = 1024
BATCH = 16
SEQ = 4096
DEPTH = 1
DEC_BATCH = 32
DEC_SEQ = 2048
PAST_LEN = 128

POOL_GROUPS = 4
POOL_WIDTH = D_MODEL // 2
POOL_GROUP_DIM = POOL_WIDTH // POOL_GROUPS
POOL_WINDOWS = (2, 4, 8, 16)
HEAD_DIM = 64
DIL_CONFIGS = ((128, 1), (512, 4), (2048, 16))
HEADS_PER_DIL = 4
N_DIL_HEADS = HEADS_PER_DIL * len(DIL_CONFIGS)
ATTN_WIDTH = N_DIL_HEADS * HEAD_DIM
ATTN_OUT_WIDTH = HEADS_PER_DIL * HEAD_DIM
BAND_BLOCK = 64
ROPE_DIM = HEAD_DIM // 4
ROPE_THETA = 500000.0
CROSS_HEADS = 4
CROSS_HEAD_DIM = 128
CROSS_WIDTH = CROSS_HEADS * CROSS_HEAD_DIM
N_MEM = 256
N_BRANCHES = 3
IN_WIDTH = POOL_WIDTH + 3 * ATTN_WIDTH + CROSS_WIDTH + N_BRANCHES * D_MODEL
N_EXPERTS = 32
TOP_K = 4
D_FF = D_MODEL
SWIGLU_LIMIT = 7.0
SWIGLU_ALPHA = 1.702
MOE_BLOCK = 128
EPS = 1e-5
NEG_INF = -1e30

kernel_name = "hybrid_gated_pool_dilattn_xattn_moe_encoder"


def _rmsnorm(x, g):
    xf = x.astype(jnp.float32)
    r = lax.rsqrt(jnp.mean(xf * xf, axis=-1, keepdims=True) + EPS)
    return (xf * r * g.astype(jnp.float32)).astype(x.dtype)


def _rope_tables(S):
    pos = jnp.arange(S, dtype=jnp.float32)
    inv = jnp.power(jnp.float32(ROPE_THETA), -jnp.arange(0, ROPE_DIM, 2, dtype=jnp.float32) / ROPE_DIM)
    ang = pos[:, None] * inv[None, :]
    return jnp.cos(ang), jnp.sin(ang)


def _partial_rope(t, cos, sin):
    half = ROPE_DIM // 2
    tr = t[..., :ROPE_DIM].astype(jnp.float32)
    x1, x2 = tr[..., :half], tr[..., half:]
    c, s = cos[None, :, None, :], sin[None, :, None, :]
    rot = jnp.concatenate([x1 * c - x2 * s, x2 * c + x1 * s], axis=-1).astype(t.dtype)
    return jnp.concatenate([rot, t[..., ROPE_DIM:]], axis=-1)


def _pool_mixer(u, w_pool, pool_scale):
    B, S, _ = u.shape
    uf = u.astype(jnp.float32).reshape(B, S, POOL_GROUPS, POOL_GROUP_DIM)
    cs = jnp.concatenate([jnp.zeros((B, 1, POOL_GROUPS, POOL_GROUP_DIM), jnp.float32),
                          jnp.cumsum(uf, axis=1)], axis=1)
    idx = jnp.arange(S)
    outs = []
    for g, w in enumerate(POOL_WINDOWS):
        lo = jnp.clip(idx - w // 2, 0, S)
        hi = jnp.clip(idx + w // 2, 0, S)
        cnt = (hi - lo).astype(jnp.float32)[None, :, None]
        mean = (cs[:, hi, g] - cs[:, lo, g]) / cnt
        outs.append(mean - uf[:, :, g])
    z = jnp.stack(outs, axis=2).astype(u.dtype)
    z = jnp.einsum('bsgc,gcd->bsgd', z, w_pool).reshape(B, S, POOL_WIDTH)
    return z * pool_scale


def _dilated_band_attention(q, k, v, dil, half):
    B, S, H, C = q.shape
    L = S // dil
    nb = -(-L // BAND_BLOCK)
    Lp = nb * BAND_BLOCK

    def residues(t):
        return t.reshape(B, L, dil, H, C).transpose(0, 2, 1, 3, 4)

    qr = jnp.pad(residues(q), ((0, 0), (0, 0), (0, Lp - L), (0, 0), (0, 0)))
    qr = qr.reshape(B, dil, nb, BAND_BLOCK, H, C)

    def key_blocks(t):
        tr = jnp.pad(residues(t), ((0, 0), (0, 0), (BAND_BLOCK, Lp - L + BAND_BLOCK), (0, 0), (0, 0)))
        tr = tr.reshape(B, dil, nb + 2, BAND_BLOCK, H, C)
        return jnp.concatenate([tr[:, :, :-2], tr[:, :, 1:-1], tr[:, :, 2:]], axis=3)

    kc = key_blocks(k)
    vc = key_blocks(v)
    scores = jnp.einsum('brnqhc,brnkhc->brnhqk', qr, kc).astype(jnp.float32) * (1.0 / math.sqrt(C))
    qpos = jnp.arange(nb)[:, None] * BAND_BLOCK + jnp.arange(BAND_BLOCK)[None, :]
    kpos = jnp.arange(nb)[:, None] * BAND_BLOCK - BAND_BLOCK + jnp.arange(3 * BAND_BLOCK)[None, :]
    diff = kpos[:, None, :] - qpos[:, :, None]
    valid = (jnp.abs(diff) <= half) & (kpos[:, None, :] >= 0) & (kpos[:, None, :] < L)
    scores = jnp.where(valid[None, None, :, None], scores, NEG_INF)
    lse = jax.nn.logsumexp(scores, axis=-1)
    p = jnp.exp(scores - lse[..., None])
    o = jnp.einsum('brnhqk,brnkhc->brnqhc', p.astype(v.dtype), vc)
    o = o.reshape(B, dil, Lp, H, C)[:, :, :L].transpose(0, 2, 1, 3, 4).reshape(B, S, H, C)
    lse = lse.transpose(0, 1, 2, 4, 3).reshape(B, dil, Lp, H)[:, :, :L].transpose(0, 2, 1, 3).reshape(B, S, H)
    return o, lse


def _dilated_mixer(q, k, v):
    B, S, _, _ = q.shape
    outs, lses = [], []
    for g, (window, dil) in enumerate(DIL_CONFIGS):
        hs = slice(g * HEADS_PER_DIL, (g + 1) * HEADS_PER_DIL)
        half = window // (2 * dil)
        o, lse = _dilated_band_attention(q[:, :, hs], k[:, :, hs], v[:, :, hs], dil, half)
        outs.append(o)
        lses.append(lse)
    alpha = jax.nn.softmax(jnp.stack(lses, axis=0), axis=0)
    o = jnp.sum(alpha[..., None].astype(q.dtype) * jnp.stack(outs, axis=0), axis=0)
    return o.reshape(B, S, ATTN_OUT_WIDTH)


def _cross_attention(qc, mem_n, w_mem_kv):
    B, S, _ = qc.shape
    M = mem_n.shape[1]
    q = qc.reshape(B, S, CROSS_HEADS, CROSS_HEAD_DIM)
    kv = jnp.einsum('bmd,de->bme', mem_n, w_mem_kv)
    k = kv[..., :CROSS_WIDTH].reshape(B, M, CROSS_HEADS, CROSS_HEAD_DIM)
    v = kv[..., CROSS_WIDTH:].reshape(B, M, CROSS_HEADS, CROSS_HEAD_DIM)
    s = jnp.einsum('bshc,bmhc->bhsm', q, k).astype(jnp.float32) * (1.0 / math.sqrt(CROSS_HEAD_DIM))
    p = jax.nn.softmax(s, axis=-1)
    o = jnp.einsum('bhsm,bmhc->bshc', p.astype(v.dtype), v)
    return o.reshape(B, S, CROSS_WIDTH)


def _moe(h, w_router, b_router, w_up, b_up, w_down, b_down):
    B, S, D = h.shape
    T = B * S
    xt = h.reshape(T, D)
    logits = (xt @ w_router + b_router).astype(jnp.float32)
    top_vals, top_idx = lax.top_k(logits, TOP_K)
    gates = jax.nn.softmax(top_vals, axis=-1)
    A = T * TOP_K
    e_flat = top_idx.reshape(A)
    tok_flat = jnp.repeat(jnp.arange(T, dtype=jnp.int32), TOP_K)
    w_flat = gates.reshape(A)
    order = jnp.argsort(e_flat)
    e_s, tok_s, w_s = e_flat[order], tok_flat[order], w_flat[order]
    counts = jnp.zeros((N_EXPERTS,), jnp.int32).at[e_flat].add(1)
    starts = jnp.cumsum(counts) - counts
    pcounts = (counts + MOE_BLOCK - 1) // MOE_BLOCK * MOE_BLOCK
    pends = jnp.cumsum(pcounts)
    pstarts = pends - pcounts
    dest = pstarts[e_s] + (jnp.arange(A, dtype=jnp.int32) - starts[e_s])
    P = -(-A // MOE_BLOCK) * MOE_BLOCK + N_EXPERTS * MOE_BLOCK
    nblk = P // MOE_BLOCK
    buf_tok = jnp.full((P,), T, jnp.int32).at[dest].set(tok_s)
    buf_w = jnp.zeros((P,), jnp.float32).at[dest].set(w_s)
    blk_start = jnp.arange(nblk, dtype=jnp.int32) * MOE_BLOCK
    blk_e = jnp.minimum(jnp.sum(blk_start[:, None] >= pends[None, :], axis=1), N_EXPERTS - 1)
    x_pad = jnp.concatenate([xt, jnp.zeros((1, D), xt.dtype)], axis=0)
    xb = x_pad[buf_tok].reshape(nblk, MOE_BLOCK, D)

    def expert_block(args):
        xblk, e = args
        hh = xblk @ w_up[e] + b_up[e]
        gate = jnp.minimum(hh[:, :D_FF], SWIGLU_LIMIT)
        up = jnp.clip(hh[:, D_FF:], -SWIGLU_LIMIT, SWIGLU_LIMIT)
        act = (up + 1.0) * (gate * jax.nn.sigmoid(SWIGLU_ALPHA * gate))
        return act @ w_down[e] + b_down[e]

    yb = lax.map(expert_block, (xb, blk_e)).reshape(P, D)
    y = jax.ops.segment_sum(yb * buf_w[:, None].astype(yb.dtype), buf_tok, num_segments=T + 1)[:T]
    return y.reshape(B, S, D)


def _layer(x, mem, norm_mix_g, norm_mem_g, w_in, w_pool, pool_scale, w_mem_kv,
           w_br_pool, w_br_attn, w_br_cross, w_out, norm_ffn_g,
           w_router, b_router, w_up, b_up, w_down, b_down):
    B, S, D = x.shape
    xn = _rmsnorm(x, norm_mix_g)
    proj = xn @ w_in
    o1 = POOL_WIDTH
    o2 = o1 + 3 * ATTN_WIDTH
    o3 = o2 + CROSS_WIDTH
    u_pool = proj[..., :o1]
    qkv = proj[..., o1:o2].reshape(B, S, 3, N_DIL_HEADS, HEAD_DIM)
    qc = proj[..., o2:o3]
    g = proj[..., o3:].reshape(B, S, N_BRANCHES, D)
    cos, sin = _rope_tables(S)
    q = _partial_rope(qkv[:, :, 0], cos, sin)
    k = _partial_rope(qkv[:, :, 1], cos, sin)
    v = qkv[:, :, 2]
    pool_out = _pool_mixer(u_pool, w_pool, pool_scale)
    attn_out = _dilated_mixer(q, k, v)
    cross_out = _cross_attention(qc, _rmsnorm(mem, norm_mem_g), w_mem_kv)
    gs = jax.nn.sigmoid(g)
    merged = (gs[:, :, 0] * (pool_out @ w_br_pool)
              + gs[:, :, 1] * (attn_out @ w_br_attn)
              + gs[:, :, 2] * (cross_out @ w_br_cross))
    x = x + merged @ w_out
    x = x + _moe(_rmsnorm(x, norm_ffn_g), w_router, b_router, w_up, b_up, w_down, b_down)
    return x


def _trunk(x, mem, layer_params, norm_final_g):
    for l in range(DEPTH):
        x = _layer(x, mem, *[p[l] for p in layer_params])
    return _rmsnorm(x, norm_final_g)


def setup_inputs(seed: int = 0) -> dict:
    key = jax.random.key(seed)
    ks = jax.random.split(key, 24)
    f32 = jnp.float32
    nrm = lambda k, shape, scale: jax.random.normal(k, shape, f32) * scale
    return {
        "x_prompt": nrm(ks[0], (BATCH, SEQ, D_MODEL), 1.0),
        "x_sample": nrm(ks[1], (DEC_BATCH, DEC_SEQ, D_MODEL), 1.0),
        "mem_prompt": nrm(ks[2], (BATCH, N_MEM, D_MODEL), 1.0),
        "mem_sample": nrm(ks[3], (DEC_BATCH, N_MEM, D_MODEL), 1.0),
        "norm_mix_g": 1.0 + nrm(ks[4], (DEPTH, D_MODEL), 0.02),
        "norm_mem_g": 1.0 + nrm(ks[5], (DEPTH, D_MODEL), 0.02),
        "w_in": nrm(ks[6], (DEPTH, D_MODEL, IN_WIDTH), D_MODEL ** -0.5),
        "w_pool": nrm(ks[7], (DEPTH, POOL_GROUPS, POOL_GROUP_DIM, POOL_GROUP_DIM), POOL_GROUP_DIM ** -0.5),
        "pool_scale": 1.0 + nrm(ks[8], (DEPTH, POOL_WIDTH), 0.02),
        "w_mem_kv": nrm(ks[9], (DEPTH, D_MODEL, 2 * CROSS_WIDTH), D_MODEL ** -0.5),
        "w_br_pool": nrm(ks[10], (DEPTH, POOL_WIDTH, D_MODEL), POOL_WIDTH ** -0.5),
        "w_br_attn": nrm(ks[11], (DEPTH, ATTN_OUT_WIDTH, D_MODEL), ATTN_OUT_WIDTH ** -0.5),
        "w_br_cross": nrm(ks[12], (DEPTH, CROSS_WIDTH, D_MODEL), CROSS_WIDTH ** -0.5),
        "w_out": nrm(ks[13], (DEPTH, D_MODEL, D_MODEL), D_MODEL ** -0.5),
        "norm_ffn_g": 1.0 + nrm(ks[14], (DEPTH, D_MODEL), 0.02),
        "w_router": nrm(ks[15], (DEPTH, D_MODEL, N_EXPERTS), D_MODEL ** -0.5),
        "b_router": nrm(ks[16], (DEPTH, N_EXPERTS), 0.01),
        "w_up": nrm(ks[17], (DEPTH, N_EXPERTS, D_MODEL, 2 * D_FF), D_MODEL ** -0.5),
        "b_up": nrm(ks[18], (DEPTH, N_EXPERTS, 2 * D_FF), 0.01),
        "w_down": nrm(ks[19], (DEPTH, N_EXPERTS, D_FF, D_MODEL), D_FF ** -0.5),
        "b_down": nrm(ks[20], (DEPTH, N_EXPERTS, D_MODEL), 0.01),
        "norm_final_g": 1.0 + nrm(ks[21], (D_MODEL,), 0.02),
    }


def reference(x_prompt, x_sample, mem_prompt, mem_sample, norm_mix_g, norm_mem_g, w_in, w_pool,
              pool_scale, w_mem_kv, w_br_pool, w_br_attn, w_br_cross, w_out, norm_ffn_g,
              w_router, b_router, w_up, b_up, w_down, b_down, norm_final_g):
    layer_params = (norm_mix_g, norm_mem_g, w_in, w_pool, pool_scale, w_mem_kv,
                    w_br_pool, w_br_attn, w_br_cross, w_out, norm_ffn_g,
                    w_router, b_router, w_up, b_up, w_down, b_down)
    y_prompt = _trunk(x_prompt, mem_prompt, layer_params, norm_final_g)
    y_sample = _trunk(x_sample, mem_sample, layer_params, norm_final_g)
    return (y_prompt, y_sample)
```

```python
import functools
import math

import jax
import jax.numpy as jnp
from jax import lax
from jax.experimental import pallas as pl
from jax.experimental.pallas import tpu as pltpu
from jax.experimental.pallas import tpu_sc as plsc

F32 = jnp.float32
BF16 = jnp.bfloat16
I32 = jnp.int32
U32 = jnp.uint32

D_MODEL = 1024
POOL_GROUPS = 4
POOL_WIDTH = 512
POOL_GROUP_DIM = 128
POOL_WINDOWS = (2, 4, 8, 16)
POOL_HALO = 16
HEAD_DIM = 64
DIL_CONFIGS = ((128, 1), (512, 4), (2048, 16))
HEADS_PER_DIL = 4
ATTN_WIDTH = 768
ATTN_OUT_WIDTH = 256
BAND_BLOCK = 64
ROPE_DIM = 16
ROPE_THETA = 500000.0
CROSS_HEADS = 4
CROSS_HEAD_DIM = 128
CROSS_WIDTH = 512
N_BRANCHES = 3
QKV_WIDTH = 3 * ATTN_WIDTH
IN_WIDTH = POOL_WIDTH + QKV_WIDTH + CROSS_WIDTH + N_BRANCHES * D_MODEL
N_EXPERTS = 32
TOP_K = 4
D_FF = 1024
SWIGLU_LIMIT = 7.0
SWIGLU_ALPHA = 1.702
EPS = 1e-5
NEG_INF = -1e30

LANES = 128
PACKED = D_MODEL // 2
TOKEN_TILE = 512
PROJ_CHUNK = 256
EXPERT_BLOCK = 512
SC_WINDOW = 128
VMEM_LIMIT = 52 * 1024 * 1024


def _rms(x, g):
    r = lax.rsqrt(jnp.mean(x * x, axis=-1, keepdims=True) + EPS)
    return x * r * g


def _pack_bf16_pair(x):
    bits = lax.bitcast_convert_type(x.astype(BF16).astype(F32), U32)
    packed = (bits[:, :PACKED] >> 16) | bits[:, PACKED:]
    return lax.bitcast_convert_type(packed, I32)


def _unpack_bf16_pair(w):
    u = lax.bitcast_convert_type(w, U32)
    lo = lax.bitcast_convert_type(u << 16, F32)
    hi = lax.bitcast_convert_type(u & jnp.uint32(0xFFFF0000), F32)
    return lo, hi


def _memkv_kernel(mem_ref, g_ref, w_ref, o_ref):
    xn = _rms(mem_ref[...], g_ref[...]).astype(BF16)
    o_ref[...] = jnp.dot(xn, w_ref[...], preferred_element_type=F32).astype(BF16)


def _memkv(mem2d, g, w_bf):
    rows = mem2d.shape[0]
    tm = 256
    return pl.pallas_call(
        _memkv_kernel,
        grid=(rows // tm,),
        in_specs=[
            pl.BlockSpec((tm, D_MODEL), lambda i: (i, 0)),
            pl.BlockSpec((1, D_MODEL), lambda i: (0, 0)),
            pl.BlockSpec((D_MODEL, 2 * CROSS_WIDTH), lambda i: (0, 0)),
        ],
        out_specs=pl.BlockSpec((tm, 2 * CROSS_WIDTH), lambda i: (i, 0)),
        out_shape=jax.ShapeDtypeStruct((rows, 2 * CROSS_WIDTH), BF16),
        compiler_params=pltpu.CompilerParams(dimension_semantics=("parallel",)),
        name="memkv",
    )(mem2d, g, w_bf)


_O_QKV = POOL_WIDTH
_O_QC = _O_QKV + QKV_WIDTH
_O_GATE = _O_QC + CROSS_WIDTH


def _proj_kernel(x_ref, g_ref, w_ref, cos_ref, s1_ref, s2_ref, u_ref, qkv_ref, qc_ref, gate_ref):
    xn = _rms(x_ref[...], g_ref[...]).astype(BF16)

    def mm(c0):
        return jnp.dot(xn, w_ref[:, c0:c0 + PROJ_CHUNK], preferred_element_type=F32)

    for j in range(POOL_WIDTH // PROJ_CHUNK):
        u_ref[:, j * PROJ_CHUNK:(j + 1) * PROJ_CHUNK] = mm(j * PROJ_CHUNK)

    cos = cos_ref[...]
    s1 = s1_ref[...]
    s2 = s2_ref[...]
    n_rope = 2 * ATTN_WIDTH // PROJ_CHUNK
    n_q = ATTN_WIDTH // PROJ_CHUNK
    for j in range(QKV_WIDTH // PROJ_CHUNK):
        t = mm(_O_QKV + j * PROJ_CHUNK)
        if j < n_rope:
            halves = []
            for hh in range(PROJ_CHUNK // LANES):
                th = t[:, hh * LANES:(hh + 1) * LANES]
                th = th * cos + pltpu.roll(th, LANES - ROPE_DIM // 2, axis=1) * s1 + pltpu.roll(th, ROPE_DIM // 2, axis=1) * s2
                halves.append(th)
            t = jnp.concatenate(halves, axis=1)
            if j < n_q:
                t = t * (1.0 / math.sqrt(HEAD_DIM))
        qkv_ref[:, j * PROJ_CHUNK:(j + 1) * PROJ_CHUNK] = t.astype(BF16)

    for j in range(CROSS_WIDTH // PROJ_CHUNK):
        qc_ref[:, j * PROJ_CHUNK:(j + 1) * PROJ_CHUNK] = mm(_O_QC + j * PROJ_CHUNK).astype(BF16)

    for j in range(N_BRANCHES * D_MODEL // PROJ_CHUNK):
        gate_ref[:, j * PROJ_CHUNK:(j + 1) * PROJ_CHUNK] = jax.nn.sigmoid(mm(_O_GATE + j * PROJ_CHUNK)).astype(BF16)


def _proj(x2d, g, w_in_bf, cos_t, s1_t, s2_t, seq):
    rows = x2d.shape[0]
    tm = TOKEN_TILE
    tiles_per_seq = seq // tm
    tab = pl.BlockSpec((tm, LANES), lambda i: (i % tiles_per_seq, 0))
    return pl.pallas_call(
        _proj_kernel,
        grid=(rows // tm,),
        in_specs=[
            pl.BlockSpec((tm, D_MODEL), lambda i: (i, 0)),
            pl.BlockSpec((1, D_MODEL), lambda i: (0, 0)),
            pl.BlockSpec((D_MODEL, IN_WIDTH), lambda i: (0, 0), pipeline_mode=pl.Buffered(1)),
            tab, tab, tab,
        ],
        out_specs=[
            pl.BlockSpec((tm, POOL_WIDTH), lambda i: (i, 0)),
            pl.BlockSpec((tm, QKV_WIDTH), lambda i: (i, 0)),
            pl.BlockSpec((tm, CROSS_WIDTH), lambda i: (i, 0)),
            pl.BlockSpec((tm, N_BRANCHES * D_MODEL), lambda i: (i, 0)),
        ],
        out_shape=[
            jax.ShapeDtypeStruct((rows, POOL_WIDTH), F32),
            jax.ShapeDtypeStruct((rows, QKV_WIDTH), BF16),
            jax.ShapeDtypeStruct((rows, CROSS_WIDTH), BF16),
            jax.ShapeDtypeStruct((rows, N_BRANCHES * D_MODEL), BF16),
        ],
        compiler_params=pltpu.CompilerParams(dimension_semantics=("parallel",), vmem_limit_bytes=VMEM_LIMIT),
        name="proj",
    )(x2d, g, w_in_bf, cos_t, s1_t, s2_t)


def _rope_tables(seq):
    pos = jnp.arange(seq, dtype=F32)
    inv = jnp.power(jnp.float32(ROPE_THETA), -jnp.arange(0, ROPE_DIM, 2, dtype=F32) / ROPE_DIM)
    ang = pos[:, None] * inv[None, :]
    half = ROPE_DIM // 2
    j = jnp.arange(LANES) % HEAD_DIM
    cos_l = jnp.cos(ang)[:, j % half]
    sin_l = jnp.sin(ang)[:, j % half]
    cos_t = jnp.where(j[None, :] < ROPE_DIM, cos_l, 1.0)
    s1_t = jnp.where(j[None, :] < half, -sin_l, 0.0)
    s2_t = jnp.where((j[None, :] >= half) & (j[None, :] < ROPE_DIM), sin_l, 0.0)
    return cos_t.astype(F32), s1_t.astype(F32), s2_t.astype(F32)


def _dilattn_kernel(q_ref, k_ref, v_ref, o_ref, lse_ref, *, res_len, tq, win, half):
    t = pl.program_id(2)
    jmi = lax.broadcasted_iota(I32, (BAND_BLOCK, win), 1) - lax.broadcasted_iota(I32, (BAND_BLOCK, win), 0)
    for n in range(tq // BAND_BLOCK):
        qpos0 = t * tq + n * BAND_BLOCK
        start = jnp.clip(qpos0 - BAND_BLOCK, 0, res_len - win)
        start = pl.multiple_of(start, BAND_BLOCK)
        kw = k_ref[0, pl.ds(start, win), :]
        vw = v_ref[0, pl.ds(start, win), :]
        qb = q_ref[0, n * BAND_BLOCK:(n + 1) * BAND_BLOCK, :]
        diff = jmi + (start - qpos0)
        valid = (diff >= -half) & (diff <= half)
        outs, lses = [], []
        for h in range(HEADS_PER_DIL):
            cs = slice(h * HEAD_DIM, (h + 1) * HEAD_DIM)
            s = lax.dot_general(qb[:, cs], kw[:, cs], (((1,), (1,)), ((), ())), preferred_element_type=F32)
            s = jnp.where(valid, s, NEG_INF)
            m = jnp.max(s, axis=-1, keepdims=True)
            p = jnp.exp(s - m)
            l = jnp.sum(p, axis=-1, keepdims=True)
            o = jnp.dot(p.astype(BF16), vw[:, cs], preferred_element_type=F32) / l
            outs.append(o)
            lses.append(jnp.broadcast_to(m + jnp.log(l), (BAND_BLOCK, HEAD_DIM)))
        rows = slice(n * BAND_BLOCK, (n + 1) * BAND_BLOCK)
        o_ref[0, rows, :] = jnp.concatenate(outs, axis=1).astype(BF16)
        lse_ref[0, rows, :] = jnp.concatenate(lses, axis=1)


def _dilattn(qkv2d, batch, seq, group):
    window, dil = DIL_CONFIGS[group]
    half = window // (2 * dil)
    res_len = seq // dil
    tq = min(256, res_len)
    win = min(3 * BAND_BLOCK, res_len)
    cols = dil * QKV_WIDTH // ATTN_OUT_WIDTH
    per_res = QKV_WIDTH // ATTN_OUT_WIDTH
    qkv3 = qkv2d.reshape(batch, res_len, dil * QKV_WIDTH)
    del cols
    kern = functools.partial(_dilattn_kernel, res_len=res_len, tq=tq, win=win, half=half)
    o, lse = pl.pallas_call(
        kern,
        grid=(batch, dil, res_len // tq),
        in_specs=[
            pl.BlockSpec((1, tq, ATTN_OUT_WIDTH), lambda b, r, t: (b, t, r * per_res + group)),
            pl.BlockSpec((1, res_len, ATTN_OUT_WIDTH), lambda b, r, t: (b, 0, r * per_res + 3 + group)),
            pl.BlockSpec((1, res_len, ATTN_OUT_WIDTH), lambda b, r, t: (b, 0, r * per_res + 6 + group)),
        ],
        out_specs=[
            pl.BlockSpec((1, tq, ATTN_OUT_WIDTH), lambda b, r, t: (b, t, r)),
            pl.BlockSpec((1, tq, ATTN_OUT_WIDTH), lambda b, r, t: (b, t, r)),
        ],
        out_shape=[
            jax.ShapeDtypeStruct((batch, res_len, dil * ATTN_OUT_WIDTH), BF16),
            jax.ShapeDtypeStruct((batch, res_len, dil * ATTN_OUT_WIDTH), F32),
        ],
        compiler_params=pltpu.CompilerParams(dimension_semantics=("parallel", "parallel", "parallel"),
                                             vmem_limit_bytes=VMEM_LIMIT),
        name=f"dilattn{group}",
    )(qkv3, qkv3, qkv3)
    return o.reshape(batch * seq, ATTN_OUT_WIDTH), lse.reshape(batch * seq, ATTN_OUT_WIDTH)


def _mix_kernel(x_ref, u_ref, up_ref, un_ref, qc_ref, gate_ref, o0_ref, o1_ref, o2_ref, l0_ref, l1_ref, l2_ref,
                kv_ref, wpool_ref, pscale_ref, wbp_ref, wba_ref, wbc_ref, wout_ref, h_ref, ubuf, merged, *, seq):
    tm = TOKEN_TILE
    i = pl.program_id(1)
    nt = pl.num_programs(1)

    ubuf[0:POOL_HALO, :] = jnp.where(i > 0, up_ref[...], 0.0)
    ubuf[POOL_HALO:POOL_HALO + tm, :] = u_ref[...]
    ubuf[POOL_HALO + tm:, :] = jnp.where(i < nt - 1, un_ref[...], 0.0)
    pos = i * tm + lax.broadcasted_iota(I32, (tm, POOL_GROUP_DIM), 0)
    pool_parts = []
    for g, w in enumerate(POOL_WINDOWS):
        cs = slice(g * POOL_GROUP_DIM, (g + 1) * POOL_GROUP_DIM)
        acc = ubuf[POOL_HALO - w // 2:POOL_HALO - w // 2 + tm, cs]
        for j in range(1, w):
            off = POOL_HALO - w // 2 + j
            acc = acc + ubuf[off:off + tm, cs]
        cnt = (jnp.minimum(pos + w // 2, seq) - jnp.maximum(pos - w // 2, 0)).astype(F32)
        z = acc / cnt - ubuf[POOL_HALO:POOL_HALO + tm, cs]
        zp = jnp.dot(z.astype(BF16), wpool_ref[g], preferred_element_type=F32)
        pool_parts.append(zp * pscale_ref[:, cs])
    pool_bf = jnp.concatenate(pool_parts, axis=1).astype(BF16)

    cross_parts = []
    for h in range(CROSS_HEADS):
        cs = slice(h * CROSS_HEAD_DIM, (h + 1) * CROSS_HEAD_DIM)
        kh = kv_ref[0, :, cs]
        vh = kv_ref[0, :, CROSS_WIDTH + h * CROSS_HEAD_DIM:CROSS_WIDTH + (h + 1) * CROSS_HEAD_DIM]
        s = lax.dot_general(qc_ref[:, cs], kh, (((1,), (1,)), ((), ())), preferred_element_type=F32)
        s = s * (1.0 / math.sqrt(CROSS_HEAD_DIM))
        m = jnp.max(s, axis=-1, keepdims=True)
        p = jnp.exp(s - m)
        l = jnp.sum(p, axis=-1, keepdims=True)
        cross_parts.append(jnp.dot(p.astype(BF16), vh, preferred_element_type=F32) / l)
    cross_bf = jnp.concatenate(cross_parts, axis=1).astype(BF16)

    l0, l1, l2 = l0_ref[...], l1_ref[...], l2_ref[...]
    mx = jnp.maximum(jnp.maximum(l0, l1), l2)
    e0, e1, e2 = jnp.exp(l0 - mx), jnp.exp(l1 - mx), jnp.exp(l2 - mx)
    attn = (e0 * o0_ref[...].astype(F32) + e1 * o1_ref[...].astype(F32) + e2 * o2_ref[...].astype(F32)) / (e0 + e1 + e2)
    attn_bf = attn.astype(BF16)

    for c in range(D_MODEL // PROJ_CHUNK):
        cs = slice(c * PROJ_CHUNK, (c + 1) * PROJ_CHUNK)
        mrg = gate_ref[:, cs].astype(F32) * jnp.dot(pool_bf, wbp_ref[:, cs], preferred_element_type=F32)
        mrg = mrg + gate_ref[:, D_MODEL + c * PROJ_CHUNK:D_MODEL + (c + 1) * PROJ_CHUNK].astype(F32) * jnp.dot(
            attn_bf, wba_ref[:, cs], preferred_element_type=F32)
        mrg = mrg + gate_ref[:, 2 * D_MODEL + c * PROJ_CHUNK:2 * D_MODEL + (c + 1) * PROJ_CHUNK].astype(F32) * jnp.dot(
            cross_bf, wbc_ref[:, cs], preferred_element_type=F32)
        merged[:, cs] = mrg.astype(BF16)
    h_ref[...] = x_ref[...] + jnp.dot(merged[...], wout_ref[...], preferred_element_type=F32)


def _mix(x2d, u, qc, gate, outs, lses, kv3, wpool_bf, pscale, wbp, wba, wbc, wout, batch, seq):
    tm = TOKEN_TILE
    ts = seq // tm
    rows = batch * seq
    hb = tm // POOL_HALO
    n_halo = rows // POOL_HALO

    def row(b, i):
        return (b * ts + i, 0)

    def const(b, i):
        return (0, 0)

    tok = lambda w: pl.BlockSpec((tm, w), row)
    in_specs = [
        tok(D_MODEL),
        tok(POOL_WIDTH),
        pl.BlockSpec((POOL_HALO, POOL_WIDTH), lambda b, i: (jnp.maximum((b * ts + i) * hb - 1, 0), 0)),
        pl.BlockSpec((POOL_HALO, POOL_WIDTH), lambda b, i: (jnp.minimum((b * ts + i + 1) * hb, n_halo - 1), 0)),
        tok(CROSS_WIDTH),
        tok(N_BRANCHES * D_MODEL),
        tok(ATTN_OUT_WIDTH), tok(ATTN_OUT_WIDTH), tok(ATTN_OUT_WIDTH),
        tok(ATTN_OUT_WIDTH), tok(ATTN_OUT_WIDTH), tok(ATTN_OUT_WIDTH),
        pl.BlockSpec((1, kv3.shape[1], 2 * CROSS_WIDTH), lambda b, i: (b, 0, 0)),
        pl.BlockSpec((POOL_GROUPS, POOL_GROUP_DIM, POOL_GROUP_DIM), lambda b, i: (0, 0, 0)),
        pl.BlockSpec((1, POOL_WIDTH), const),
        pl.BlockSpec((POOL_WIDTH, D_MODEL), const),
        pl.BlockSpec((ATTN_OUT_WIDTH, D_MODEL), const),
        pl.BlockSpec((CROSS_WIDTH, D_MODEL), const),
        pl.BlockSpec((D_MODEL, D_MODEL), const),
    ]
    return pl.pallas_call(
        functools.partial(_mix_kernel, seq=seq),
        grid=(batch, ts),
        in_specs=in_specs,
        out_specs=pl.BlockSpec((tm, D_MODEL), row),
        out_shape=jax.ShapeDtypeStruct((rows, D_MODEL), F32),
        scratch_shapes=[pltpu.VMEM((tm + 2 * POOL_HALO, POOL_WIDTH), F32), pltpu.VMEM((tm, D_MODEL), BF16)],
        compiler_params=pltpu.CompilerParams(dimension_semantics=("parallel", "parallel"), vmem_limit_bytes=VMEM_LIMIT),
        name="mix",
    )(x2d, u, u, u, qc, gate, *outs, *lses, kv3, wpool_bf, pscale, wbp, wba, wbc, wout)


def _route_kernel(h_ref, g_ref, whi_ref, wlo_ref, b_ref, hn_ref, ri_ref, rg_ref, cnt_ref, carry, tri):
    tm = TOKEN_TILE
    step = pl.program_id(0)

    @pl.when(step == 0)
    def _():
        carry[...] = jnp.zeros_like(carry)
        r = lax.broadcasted_iota(I32, (tm, tm), 0)
        c = lax.broadcasted_iota(I32, (tm, tm), 1)
        tri[...] = jnp.where(c < r, 1.0, 0.0).astype(BF16)

    hn = _rms(h_ref[...], g_ref[...])
    hn_ref[...] = _pack_bf16_pair(hn)
    hi = hn.astype(BF16)
    lo = (hn - hi.astype(F32)).astype(BF16)
    logits = (jnp.dot(hi, whi_ref[...], preferred_element_type=F32)
              + jnp.dot(lo, whi_ref[...], preferred_element_type=F32)
              + jnp.dot(hi, wlo_ref[...], preferred_element_type=F32)) + b_ref[...]

    lane = lax.broadcasted_iota(I32, (tm, N_EXPERTS), 1)
    work = logits
    idxs, vals = [], []
    onehot = jnp.zeros((tm, N_EXPERTS), F32)
    for _ in range(TOP_K):
        m = jnp.max(work, axis=-1, keepdims=True)
        idx = jnp.min(jnp.where(work == m, lane, N_EXPERTS), axis=-1, keepdims=True)
        sel = lane == idx
        onehot = jnp.where(sel, 1.0, onehot)
        work = jnp.where(sel, -jnp.inf, work)
        idxs.append(idx)
        vals.append(m)
    exps = [jnp.exp(v - vals[0]) for v in vals]
    den = exps[0] + exps[1] + exps[2] + exps[3]
    gates = [e / den for e in exps]

    prefix = jnp.dot(tri[...], onehot.astype(BF16), preferred_element_type=F32) + carry[...]
    ranks = [jnp.sum(jnp.where(lane == idx, prefix, 0.0), axis=-1, keepdims=True).astype(I32) for idx in idxs]
    carry[...] = carry[...] + jnp.sum(onehot, axis=0, keepdims=True)

    lane128 = lax.broadcasted_iota(I32, (tm, LANES), 1)
    ri = jnp.zeros((tm, LANES), I32)
    rg = jnp.zeros((tm, LANES), F32)
    for k in range(TOP_K):
        ri = jnp.where(lane128 == k, idxs[k], ri)
        ri = jnp.where(lane128 == TOP_K + k, ranks[k], ri)
        rg = jnp.where(lane128 == k, gates[k], rg)
    ri_ref[...] = ri
    rg_ref[...] = rg
    cnt_ref[...] = carry[...].astype(I32)


def _route(h2d, g, whi, wlo, b):
    rows = h2d.shape[0]
    tm = TOKEN_TILE
    const = lambda i: (0, 0)
    return pl.pallas_call(
        _route_kernel,
        grid=(rows // tm,),
        in_specs=[
            pl.BlockSpec((tm, D_MODEL), lambda i: (i, 0)),
            pl.BlockSpec((1, D_MODEL), const),
            pl.BlockSpec((D_MODEL, N_EXPERTS), const),
            pl.BlockSpec((D_MODEL, N_EXPERTS), const),
            pl.BlockSpec((1, N_EXPERTS), const),
        ],
        out_specs=[
            pl.BlockSpec((tm, PACKED), lambda i: (i, 0)),
            pl.BlockSpec((tm, LANES), lambda i: (i, 0)),
            pl.BlockSpec((tm, LANES), lambda i: (i, 0)),
            pl.BlockSpec((1, N_EXPERTS), const),
        ],
        out_shape=[
            jax.ShapeDtypeStruct((rows, PACKED), I32),
            jax.ShapeDtypeStruct((rows, LANES), I32),
            jax.ShapeDtypeStruct((rows, LANES), F32),
            jax.ShapeDtypeStruct((1, N_EXPERTS), I32),
        ],
        scratch_shapes=[pltpu.VMEM((1, N_EXPERTS), F32), pltpu.VMEM((tm, tm), BF16)],
        compiler_params=pltpu.CompilerParams(dimension_semantics=("arbitrary",), vmem_limit_bytes=VMEM_LIMIT),
        name="route",
    )(h2d, g, whi, wlo, b)


def _sc_mesh():
    return plsc.VectorSubcoreMesh(core_axis_name="c", subcore_axis_name="s")


def _sc_workers():
    info = plsc.get_sparse_core_info()
    return info.num_cores, info.num_cores * info.num_subcores


def _sc_dispatch(rows_packed, slots, n_slots):
    n_tok = rows_packed.shape[0]
    n_cores, n_workers = _sc_workers()
    chunks_per_worker = n_tok // SC_WINDOW // n_workers

    @functools.partial(pl.kernel, out_type=jax.ShapeDtypeStruct((n_slots, PACKED), I32), mesh=_sc_mesh(),
                       scratch_types=[pltpu.VMEM((TOP_K, SC_WINDOW), I32), pltpu.VMEM((SC_WINDOW, PACKED), I32)],
                       name="dispatch")
    def kern(x_hbm, i_hbm, o_hbm, idx_v, rows_v):
        wid = lax.axis_index("s") * n_cores + lax.axis_index("c")

        @pl.loop(0, chunks_per_worker)
        def _(j):
            chunk = wid * chunks_per_worker + j
            pltpu.sync_copy(i_hbm.at[chunk], idx_v)
            pltpu.sync_copy(x_hbm.at[pl.ds(chunk * SC_WINDOW, SC_WINDOW)], rows_v)
            for k in range(TOP_K):
                pltpu.sync_copy(rows_v, o_hbm.at[idx_v.at[k]])

    return kern(rows_packed, slots)


def _sc_combine(ys, slots):
    n_tok = slots.shape[0] * SC_WINDOW
    n_cores, n_workers = _sc_workers()
    chunks_per_worker = n_tok // SC_WINDOW // n_workers

    @functools.partial(pl.kernel, out_type=jax.ShapeDtypeStruct((TOP_K, n_tok, PACKED), I32), mesh=_sc_mesh(),
                       scratch_types=[pltpu.VMEM((TOP_K, SC_WINDOW), I32), pltpu.VMEM((SC_WINDOW, PACKED), I32)],
                       name="combine")
    def kern(y_hbm, i_hbm, o_hbm, idx_v, rows_v):
        wid = lax.axis_index("s") * n_cores + lax.axis_index("c")

        @pl.loop(0, chunks_per_worker)
        def _(j):
            chunk = wid * chunks_per_worker + j
            pltpu.sync_copy(i_hbm.at[chunk], idx_v)
            for k in range(TOP_K):
                pltpu.sync_copy(y_hbm.at[idx_v.at[k]], rows_v)
                pltpu.sync_copy(rows_v, o_hbm.at[k, pl.ds(chunk * SC_WINDOW, SC_WINDOW)])

    return kern(ys, slots)


def _expert_kernel(blk_e_ref, nused_ref, x_ref, wu_ref, bu_ref, wd_ref, bd_ref, o_ref):
    del blk_e_ref

    @pl.when(pl.program_id(0) < nused_ref[0])
    def _():
        lo, hi = _unpack_bf16_pair(x_ref[...])
        x = jnp.concatenate([lo.astype(BF16), hi.astype(BF16)], axis=1)
        y = jnp.zeros((EXPERT_BLOCK, D_MODEL), F32)
        for c in range(D_FF // PROJ_CHUNK):
            gs = slice(c * PROJ_CHUNK, (c + 1) * PROJ_CHUNK)
            us = slice(D_FF + c * PROJ_CHUNK, D_FF + (c + 1) * PROJ_CHUNK)
            gate = jnp.dot(x, wu_ref[0, :, gs], preferred_element_type=F32) + bu_ref[0, :, gs]
            up = jnp.dot(x, wu_ref[0, :, us], preferred_element_type=F32) + bu_ref[0, :, us]
            gate = jnp.minimum(gate, SWIGLU_LIMIT)
            up = jnp.clip(up, -SWIGLU_LIMIT, SWIGLU_LIMIT)
            act = (up + 1.0) * (gate * jax.nn.sigmoid(SWIGLU_ALPHA * gate))
            y = y + jnp.dot(act.astype(BF16), wd_ref[0, gs, :], preferred_element_type=F32)
        o_ref[...] = _pack_bf16_pair(y + bd_ref[0])

    @pl.when(pl.program_id(0) >= nused_ref[0])
    def _():
        o_ref[...] = jnp.zeros_like(o_ref)


def _experts(xs, blk_e, nused, wu, bu, wd, bd):
    n_slots = xs.shape[0]
    tb = EXPERT_BLOCK
    grid_spec = pltpu.PrefetchScalarGridSpec(
        num_scalar_prefetch=2,
        grid=(n_slots // tb,),
        in_specs=[
            pl.BlockSpec((tb, PACKED), lambda i, be, nu: (i, 0)),
            pl.BlockSpec((1, D_MODEL, 2 * D_FF), lambda i, be, nu: (be[i], 0, 0)),
            pl.BlockSpec((1, 1, 2 * D_FF), lambda i, be, nu: (be[i], 0, 0)),
            pl.BlockSpec((1, D_FF, D_MODEL), lambda i, be, nu: (be[i], 0, 0)),
            pl.BlockSpec((1, 1, D_MODEL), lambda i, be, nu: (be[i], 0, 0)),
        ],
        out_specs=pl.BlockSpec((tb, PACKED), lambda i, be, nu: (i, 0)),
    )
    return pl.pallas_call(
        _expert_kernel,
        grid_spec=grid_spec,
        out_shape=jax.ShapeDtypeStruct((n_slots, PACKED), I32),
        compiler_params=pltpu.CompilerParams(dimension_semantics=("parallel",), vmem_limit_bytes=VMEM_LIMIT),
        name="experts",
    )(blk_e, nused, xs, wu, bu, wd, bd)


def _final_kernel(h_ref, yg_ref, rg_ref, g_ref, o_ref):
    rg = rg_ref[...]
    lo = jnp.zeros((TOKEN_TILE, PACKED), F32)
    hi = jnp.zeros((TOKEN_TILE, PACKED), F32)
    for k in range(TOP_K):
        yl, yh = _unpack_bf16_pair(yg_ref[k])
        w = rg[:, k:k + 1]
        lo = lo + w * yl
        hi = hi + w * yh
    x = h_ref[...] + jnp.concatenate([lo, hi], axis=1)
    o_ref[...] = _rms(x, g_ref[...])


def _final(h2d, yg, rg, g):
    rows = h2d.shape[0]
    tm = TOKEN_TILE
    return pl.pallas_call(
        _final_kernel,
        grid=(rows // tm,),
        in_specs=[
            pl.BlockSpec((tm, D_MODEL), lambda i: (i, 0)),
            pl.BlockSpec((TOP_K, tm, PACKED), lambda i: (0, i, 0)),
            pl.BlockSpec((tm, LANES), lambda i: (i, 0)),
            pl.BlockSpec((1, D_MODEL), lambda i: (0, 0)),
        ],
        out_specs=pl.BlockSpec((tm, D_MODEL), lambda i: (i, 0)),
        out_shape=jax.ShapeDtypeStruct((rows, D_MODEL), F32),
        compiler_params=pltpu.CompilerParams(dimension_semantics=("parallel",), vmem_limit_bytes=VMEM_LIMIT),
        name="final",
    )(h2d, yg, rg, g)


def _plan(ri, counts, n_tok):
    tb = EXPERT_BLOCK
    n_slots = n_tok * TOP_K + N_EXPERTS * tb
    eid = ri[:, :TOP_K]
    rank = ri[:, TOP_K:2 * TOP_K]
    c = counts.reshape(N_EXPERTS)
    pc = (c + tb - 1) // tb * tb
    pend = jnp.cumsum(pc)
    pstart = pend - pc
    slots = (pstart[eid] + rank).astype(I32)
    slots = slots.reshape(n_tok // SC_WINDOW, SC_WINDOW, TOP_K).transpose(0, 2, 1)
    blk_start = jnp.arange(n_slots // tb, dtype=I32) * tb
    blk_e = jnp.minimum(jnp.sum(blk_start[:, None] >= pend[None, :], axis=1), N_EXPERTS - 1).astype(I32)
    nused = (pend[-1:] // tb).astype(I32)
    return slots, blk_e, nused, n_slots


def _trunk(x, mem, p):
    batch, seq, _ = x.shape
    n_tok = batch * seq
    x2d = x.reshape(n_tok, D_MODEL)
    kv = _memkv(mem.reshape(-1, D_MODEL), p["norm_mem_g"], p["w_mem_kv"])
    kv3 = kv.reshape(batch, mem.shape[1], 2 * CROSS_WIDTH)
    cos_t, s1_t, s2_t = _rope_tables(seq)
    u, qkv, qc, gate = _proj(x2d, p["norm_mix_g"], p["w_in"], cos_t, s1_t, s2_t, seq)
    outs, lses = [], []
    for g in range(len(DIL_CONFIGS)):
        o, lse = _dilattn(qkv, batch, seq, g)
        outs.append(o)
        lses.append(lse)
    h = _mix(x2d, u, qc, gate, outs, lses, kv3, p["w_pool"], p["pool_scale"], p["w_br_pool"], p["w_br_attn"],
             p["w_br_cross"], p["w_out"], batch, seq)
    hn, ri, rg, counts = _route(h, p["norm_ffn_g"], p["w_router_hi"], p["w_router_lo"], p["b_router"])
    slots, blk_e, nused, n_slots = _plan(ri, counts, n_tok)
    xs = _sc_dispatch(hn, slots, n_slots)
    ys = _experts(xs, blk_e, nused, p["w_up"], p["b_up"], p["w_down"], p["b_down"])
    yg = _sc_combine(ys, slots)
    out = _final(h, yg, rg, p["norm_final_g"])
    return out.reshape(batch, seq, D_MODEL)


def _prep_params(norm_mix_g, norm_mem_g, w_in, w_pool, pool_scale, w_mem_kv, w_br_pool, w_br_attn, w_br_cross,
                 w_out, norm_ffn_g, w_router, b_router, w_up, b_up, w_down, b_down, norm_final_g):
    wr = w_router[0]
    wr_hi = wr.astype(BF16)
    return dict(
        norm_mix_g=norm_mix_g[0].reshape(1, D_MODEL),
        norm_mem_g=norm_mem_g[0].reshape(1, D_MODEL),
        w_in=w_in[0].astype(BF16),
        w_pool=w_pool[0].astype(BF16),
        pool_scale=pool_scale[0].reshape(1, POOL_WIDTH),
        w_mem_kv=w_mem_kv[0].astype(BF16),
        w_br_pool=w_br_pool[0].astype(BF16),
        w_br_attn=w_br_attn[0].astype(BF16),
        w_br_cross=w_br_cross[0].astype(BF16),
        w_out=w_out[0].astype(BF16),
        norm_ffn_g=norm_ffn_g[0].reshape(1, D_MODEL),
        w_router_hi=wr_hi,
        w_router_lo=(wr - wr_hi.astype(F32)).astype(BF16),
        b_router=b_router[0].reshape(1, N_EXPERTS),
        w_up=w_up[0].astype(BF16),
        b_up=b_up[0].reshape(N_EXPERTS, 1, 2 * D_FF),
        w_down=w_down[0].astype(BF16),
        b_down=b_down[0].reshape(N_EXPERTS, 1, D_MODEL),
        norm_final_g=norm_final_g.reshape(1, D_MODEL),
    )


def kernel(x_prompt, x_sample, mem_prompt, mem_sample, norm_mix_g, norm_mem_g, w_in, w_pool, pool_scale, w_mem_kv,
           w_br_pool, w_br_attn, w_br_cross, w_out, norm_ffn_g, w_router, b_router, w_up, b_up, w_down, b_down,
           norm_final_g):
    p = _prep_params(norm_mix_g, norm_mem_g, w_in, w_pool, pool_scale, w_mem_kv, w_br_pool, w_br_attn, w_br_cross,
                     w_out, norm_ffn_g, w_router, b_router, w_up, b_up, w_down, b_down, norm_final_g)
    y_prompt = _trunk(x_prompt, mem_prompt, p)
    y_sample = _trunk(x_sample, mem_sample, p)
    return (y_prompt, y_sample)
```

```python
import functools
import math

import jax
import jax.numpy as jnp
from jax import lax
from jax.experimental import pallas as pl
from jax.experimental.pallas import tpu as pltpu
from jax.experimental.pallas import tpu_sc as plsc

F32 = jnp.float32
BF16 = jnp.bfloat16
I32 = jnp.int32
U32 = jnp.uint32

D_MODEL = 1024
POOL_GROUPS = 4
POOL_WIDTH = 512
POOL_GROUP_DIM = 128
POOL_WINDOWS = (2, 4, 8, 16)
POOL_HALO = 16
HEAD_DIM = 64
DIL_CONFIGS = ((128, 1), (512, 4), (2048, 16))
HEADS_PER_DIL = 4
ATTN_WIDTH = 768
ATTN_OUT_WIDTH = 256
BAND_BLOCK = 64
ROPE_DIM = 16
ROPE_THETA = 500000.0
CROSS_HEADS = 4
CROSS_HEAD_DIM = 128
CROSS_WIDTH = 512
N_BRANCHES = 3
QKV_WIDTH = 3 * ATTN_WIDTH
IN_WIDTH = POOL_WIDTH + QKV_WIDTH + CROSS_WIDTH + N_BRANCHES * D_MODEL
N_EXPERTS = 32
TOP_K = 4
D_FF = 1024
SWIGLU_LIMIT = 7.0
SWIGLU_ALPHA = 1.702
EPS = 1e-5
NEG_INF = -1e30

LANES = 128
PACKED = D_MODEL // 2
TOKEN_TILE = 512
PROJ_CHUNK = 256
EXPERT_BLOCK = 512
SC_WINDOW = 128
VMEM_LIMIT = 52 * 1024 * 1024


def _rms(x, g):
    r = lax.rsqrt(jnp.mean(x * x, axis=-1, keepdims=True) + EPS)
    return x * r * g


def _pack_bf16_pair(x):
    bits = lax.bitcast_convert_type(x.astype(BF16).astype(F32), U32)
    packed = (bits[:, :PACKED] >> 16) | bits[:, PACKED:]
    return lax.bitcast_convert_type(packed, I32)


def _unpack_bf16_pair(w):
    u = lax.bitcast_convert_type(w, U32)
    lo = lax.bitcast_convert_type(u << 16, F32)
    hi = lax.bitcast_convert_type(u & jnp.uint32(0xFFFF0000), F32)
    return lo, hi


def _memkv_kernel(mem_ref, g_ref, w_ref, o_ref):
    xn = _rms(mem_ref[...], g_ref[...]).astype(BF16)
    o_ref[...] = jnp.dot(xn, w_ref[...], preferred_element_type=F32).astype(BF16)


def _memkv(mem2d, g, w_bf):
    rows = mem2d.shape[0]
    tm = 256
    return pl.pallas_call(
        _memkv_kernel,
        grid=(rows // tm,),
        in_specs=[
            pl.BlockSpec((tm, D_MODEL), lambda i: (i, 0)),
            pl.BlockSpec((1, D_MODEL), lambda i: (0, 0)),
            pl.BlockSpec((D_MODEL, 2 * CROSS_WIDTH), lambda i: (0, 0)),
        ],
        out_specs=pl.BlockSpec((tm, 2 * CROSS_WIDTH), lambda i: (i, 0)),
        out_shape=jax.ShapeDtypeStruct((rows, 2 * CROSS_WIDTH), BF16),
        compiler_params=pltpu.CompilerParams(dimension_semantics=("parallel",)),
        name="memkv",
    )(mem2d, g, w_bf)


_O_QKV = POOL_WIDTH
_O_QC = _O_QKV + QKV_WIDTH
_O_GATE = _O_QC + CROSS_WIDTH


def _proj_kernel(x_ref, g_ref, w_ref, tab_ref, u_ref, a0_ref, a1_ref, a2_ref, qc_ref, gate_ref, xcols):
    tm = TOKEN_TILE
    n_cols = D_MODEL // LANES
    g = g_ref[...]
    xn = _rms(x_ref[...], g).astype(BF16)
    for c in range(n_cols):
        xcols[c] = x_ref[:, c * LANES:(c + 1) * LANES]

    def mm(lhs, c0):
        return jnp.dot(lhs, w_ref[:, c0:c0 + PROJ_CHUNK], preferred_element_type=F32)

    for j in range(POOL_WIDTH // PROJ_CHUNK):
        u_ref[:, j * PROJ_CHUNK:(j + 1) * PROJ_CHUNK] = mm(xn, j * PROJ_CHUNK)

    for grp, a_ref in enumerate((a0_ref, a1_ref, a2_ref)):
        dil = DIL_CONFIGS[grp][1]
        res_rows = tm // dil
        if dil == 1:
            lhs = xn
        else:
            xp = jnp.concatenate(
                [jnp.concatenate([xcols[c, pl.ds(r, res_rows, stride=dil), :] for c in range(n_cols)], axis=1)
                 for r in range(dil)], axis=0)
            lhs = _rms(xp, g).astype(BF16)
        cos, s1, s2 = tab_ref[0, grp], tab_ref[1, grp], tab_ref[2, grp]
        for which in range(3):
            t = mm(lhs, _O_QKV + which * ATTN_WIDTH + grp * ATTN_OUT_WIDTH)
            if which < 2:
                halves = []
                for hh in range(PROJ_CHUNK // LANES):
                    th = t[:, hh * LANES:(hh + 1) * LANES]
                    th = (th * cos + pltpu.roll(th, LANES - ROPE_DIM // 2, axis=1) * s1
                          + pltpu.roll(th, ROPE_DIM // 2, axis=1) * s2)
                    halves.append(th)
                t = jnp.concatenate(halves, axis=1)
                if which == 0:
                    t = t * (1.0 / math.sqrt(HEAD_DIM))
            tb = t.astype(BF16)
            for r in range(dil):
                c0 = (r * 3 + which) * ATTN_OUT_WIDTH
                a_ref[:, c0:c0 + ATTN_OUT_WIDTH] = tb[r * res_rows:(r + 1) * res_rows, :]

    for j in range(CROSS_WIDTH // PROJ_CHUNK):
        qc_ref[:, j * PROJ_CHUNK:(j + 1) * PROJ_CHUNK] = mm(xn, _O_QC + j * PROJ_CHUNK).astype(BF16)

    for j in range(N_BRANCHES * D_MODEL // PROJ_CHUNK):
        gate_ref[:, j * PROJ_CHUNK:(j + 1) * PROJ_CHUNK] = jax.nn.sigmoid(mm(xn, _O_GATE + j * PROJ_CHUNK)).astype(BF16)


def _proj(x2d, g, w_in_bf, tabs, seq):
    rows = x2d.shape[0]
    tm = TOKEN_TILE
    tiles_per_seq = seq // tm
    n_grp = len(DIL_CONFIGS)
    a_specs = [pl.BlockSpec((tm // d, d * ATTN_WIDTH), lambda i: (i, 0)) for _, d in DIL_CONFIGS]
    a_shapes = [jax.ShapeDtypeStruct((rows // d, d * ATTN_WIDTH), BF16) for _, d in DIL_CONFIGS]
    return pl.pallas_call(
        _proj_kernel,
        grid=(rows // tm,),
        in_specs=[
            pl.BlockSpec((tm, D_MODEL), lambda i: (i, 0)),
            pl.BlockSpec((1, D_MODEL), lambda i: (0, 0)),
            pl.BlockSpec((D_MODEL, IN_WIDTH), lambda i: (0, 0), pipeline_mode=pl.Buffered(1)),
            pl.BlockSpec((3, n_grp, tm, LANES), lambda i: (0, 0, i % tiles_per_seq, 0)),
        ],
        out_specs=[
            pl.BlockSpec((tm, POOL_WIDTH), lambda i: (i, 0)),
            *a_specs,
            pl.BlockSpec((tm, CROSS_WIDTH), lambda i: (i, 0)),
            pl.BlockSpec((tm, N_BRANCHES * D_MODEL), lambda i: (i, 0)),
        ],
        out_shape=[
            jax.ShapeDtypeStruct((rows, POOL_WIDTH), F32),
            *a_shapes,
            jax.ShapeDtypeStruct((rows, CROSS_WIDTH), BF16),
            jax.ShapeDtypeStruct((rows, N_BRANCHES * D_MODEL), BF16),
        ],
        scratch_shapes=[pltpu.VMEM((D_MODEL // LANES, tm, LANES), F32)],
        compiler_params=pltpu.CompilerParams(dimension_semantics=("parallel",), vmem_limit_bytes=VMEM_LIMIT),
        name="proj",
    )(x2d, g, w_in_bf, tabs)


def _rope_tables(seq):
    pos = jnp.arange(seq, dtype=F32)
    inv = jnp.power(jnp.float32(ROPE_THETA), -jnp.arange(0, ROPE_DIM, 2, dtype=F32) / ROPE_DIM)
    ang = pos[:, None] * inv[None, :]
    half = ROPE_DIM // 2
    j = jnp.arange(LANES) % HEAD_DIM
    cos_l = jnp.cos(ang)[:, j % half]
    sin_l = jnp.sin(ang)[:, j % half]
    kinds = jnp.stack([
        jnp.where(j[None, :] < ROPE_DIM, cos_l, 1.0),
        jnp.where(j[None, :] < half, -sin_l, 0.0),
        jnp.where((j[None, :] >= half) & (j[None, :] < ROPE_DIM), sin_l, 0.0),
    ]).astype(F32)
    tm = TOKEN_TILE
    per_group = []
    for _, d in DIL_CONFIGS:
        t = kinds.reshape(3, seq // tm, tm // d, d, LANES).transpose(0, 1, 3, 2, 4).reshape(3, seq, LANES)
        per_group.append(t)
    return jnp.stack(per_group, axis=1)


def _dilattn_kernel(q_ref, k_ref, v_ref, o_ref, lse_ref, *, res_len, tq, win, half):
    t = pl.program_id(2)
    nblk = tq // BAND_BLOCK
    stacked = HEADS_PER_DIL * BAND_BLOCK
    lane_head = lax.broadcasted_iota(I32, (BAND_BLOCK, ATTN_OUT_WIDTH), 1) // HEAD_DIM
    head_sel = [lane_head == h for h in range(HEADS_PER_DIL)]
    head_mask = [jnp.where(sel, 1.0, 0.0).astype(BF16) for sel in head_sel]
    jmi = (lax.broadcasted_iota(I32, (stacked, win), 1)
           - (lax.broadcasted_iota(I32, (stacked, win), 0) & (BAND_BLOCK - 1)))
    ones = jnp.ones((win, LANES), BF16)

    s_parts, v_wins = [], []
    for n in range(nblk):
        qpos0 = t * tq + n * BAND_BLOCK
        start = pl.multiple_of(jnp.clip(qpos0 - BAND_BLOCK, 0, res_len - win), BAND_BLOCK)
        kw = k_ref[0, pl.ds(start, win), :]
        v_wins.append(v_ref[0, pl.ds(start, win), :])
        qb = q_ref[0, n * BAND_BLOCK:(n + 1) * BAND_BLOCK, :]
        qs = jnp.concatenate([qb * hm for hm in head_mask], axis=0)
        s = lax.dot_general(qs, kw, (((1,), (1,)), ((), ())), preferred_element_type=F32)
        diff = jmi + (start - qpos0)
        s_parts.append(jnp.where((diff >= -half) & (diff <= half), s, NEG_INF))
    s_all = jnp.concatenate(s_parts, axis=0)
    m_all = jnp.max(s_all, axis=-1, keepdims=True)
    p_all = jnp.exp(s_all - m_all).astype(BF16)

    for n in range(nblk):
        p = p_all[n * stacked:(n + 1) * stacked]
        o_st = jnp.dot(p, v_wins[n], preferred_element_type=F32)
        l_st = jnp.dot(p, ones, preferred_element_type=F32)
        m_st = m_all[n * stacked:(n + 1) * stacked]
        o = jnp.zeros((BAND_BLOCK, ATTN_OUT_WIDTH), F32)
        l = jnp.ones((BAND_BLOCK, ATTN_OUT_WIDTH), F32)
        m = jnp.zeros((BAND_BLOCK, ATTN_OUT_WIDTH), F32)
        for h in range(HEADS_PER_DIL):
            rs = slice(h * BAND_BLOCK, (h + 1) * BAND_BLOCK)
            o = jnp.where(head_sel[h], o_st[rs], o)
            l = jnp.where(head_sel[h], jnp.concatenate([l_st[rs], l_st[rs]], axis=1), l)
            m = jnp.where(head_sel[h], m_st[rs], m)
        rows = slice(n * BAND_BLOCK, (n + 1) * BAND_BLOCK)
        o_ref[0, rows, :] = (o / l).astype(BF16)
        lse_ref[0, rows, :] = m + jnp.log(l)


def _dilattn(a_grp, batch, seq, group):
    window, dil = DIL_CONFIGS[group]
    half = window // (2 * dil)
    res_len = seq // dil
    tq = min(512, res_len)
    win = min(3 * BAND_BLOCK, res_len)
    qkv3 = a_grp.reshape(batch, res_len, dil * ATTN_WIDTH)
    kern = functools.partial(_dilattn_kernel, res_len=res_len, tq=tq, win=win, half=half)
    o, lse = pl.pallas_call(
        kern,
        grid=(batch, dil, res_len // tq),
        in_specs=[
            pl.BlockSpec((1, tq, ATTN_OUT_WIDTH), lambda b, r, t: (b, t, r * 3)),
            pl.BlockSpec((1, res_len, ATTN_OUT_WIDTH), lambda b, r, t: (b, 0, r * 3 + 1)),
            pl.BlockSpec((1, res_len, ATTN_OUT_WIDTH), lambda b, r, t: (b, 0, r * 3 + 2)),
        ],
        out_specs=[
            pl.BlockSpec((1, tq, ATTN_OUT_WIDTH), lambda b, r, t: (b, t, r)),
            pl.BlockSpec((1, tq, ATTN_OUT_WIDTH), lambda b, r, t: (b, t, r)),
        ],
        out_shape=[
            jax.ShapeDtypeStruct((batch, res_len, dil * ATTN_OUT_WIDTH), BF16),
            jax.ShapeDtypeStruct((batch, res_len, dil * ATTN_OUT_WIDTH), F32),
        ],
        compiler_params=pltpu.CompilerParams(dimension_semantics=("parallel", "parallel", "parallel"),
                                             vmem_limit_bytes=VMEM_LIMIT),
        name=f"dilattn{group}",
    )(qkv3, qkv3, qkv3)
    return o.reshape(batch * seq, ATTN_OUT_WIDTH), lse.reshape(batch * seq, ATTN_OUT_WIDTH)


def _mix_kernel(x_ref, u_ref, up_ref, un_ref, qc_ref, gate_ref, o0_ref, o1_ref, o2_ref, l0_ref, l1_ref, l2_ref,
                kv_ref, wpool_ref, pscale_ref, wbp_ref, wba_ref, wbc_ref, wout_ref, h_ref, ubuf, merged, *, seq):
    tm = TOKEN_TILE
    i = pl.program_id(1)
    nt = pl.num_programs(1)

    ubuf[0:POOL_HALO, :] = jnp.where(i > 0, up_ref[...], 0.0)
    ubuf[POOL_HALO:POOL_HALO + tm, :] = u_ref[...]
    ubuf[POOL_HALO + tm:, :] = jnp.where(i < nt - 1, un_ref[...], 0.0)
    pos = i * tm + lax.broadcasted_iota(I32, (tm, POOL_GROUP_DIM), 0)
    pool_parts = []
    for g, w in enumerate(POOL_WINDOWS):
        cs = slice(g * POOL_GROUP_DIM, (g + 1) * POOL_GROUP_DIM)
        acc = ubuf[POOL_HALO - w // 2:POOL_HALO - w // 2 + tm, cs]
        for j in range(1, w):
            off = POOL_HALO - w // 2 + j
            acc = acc + ubuf[off:off + tm, cs]
        cnt = (jnp.minimum(pos + w // 2, seq) - jnp.maximum(pos - w // 2, 0)).astype(F32)
        z = acc / cnt - ubuf[POOL_HALO:POOL_HALO + tm, cs]
        zp = jnp.dot(z.astype(BF16), wpool_ref[g], preferred_element_type=F32)
        pool_parts.append(zp * pscale_ref[:, cs])
    pool_bf = jnp.concatenate(pool_parts, axis=1).astype(BF16)

    cross_parts = []
    for h in range(CROSS_HEADS):
        cs = slice(h * CROSS_HEAD_DIM, (h + 1) * CROSS_HEAD_DIM)
        kh = kv_ref[0, :, cs]
        vh = kv_ref[0, :, CROSS_WIDTH + h * CROSS_HEAD_DIM:CROSS_WIDTH + (h + 1) * CROSS_HEAD_DIM]
        s = lax.dot_general(qc_ref[:, cs], kh, (((1,), (1,)), ((), ())), preferred_element_type=F32)
        s = s * (1.0 / math.sqrt(CROSS_HEAD_DIM))
        m = jnp.max(s, axis=-1, keepdims=True)
        p = jnp.exp(s - m)
        l = jnp.sum(p, axis=-1, keepdims=True)
        cross_parts.append(jnp.dot(p.astype(BF16), vh, preferred_element_type=F32) / l)
    cross_bf = jnp.concatenate(cross_parts, axis=1).astype(BF16)

    l0, l1, l2 = l0_ref[...], l1_ref[...], l2_ref[...]
    mx = jnp.maximum(jnp.maximum(l0, l1), l2)
    e0, e1, e2 = jnp.exp(l0 - mx), jnp.exp(l1 - mx), jnp.exp(l2 - mx)
    attn = (e0 * o0_ref[...].astype(F32) + e1 * o1_ref[...].astype(F32) + e2 * o2_ref[...].astype(F32)) / (e0 + e1 + e2)
    attn_bf = attn.astype(BF16)

    for c in range(D_MODEL // PROJ_CHUNK):
        cs = slice(c * PROJ_CHUNK, (c + 1) * PROJ_CHUNK)
        mrg = gate_ref[:, cs].astype(F32) * jnp.dot(pool_bf, wbp_ref[:, cs], preferred_element_type=F32)
        mrg = mrg + gate_ref[:, D_MODEL + c * PROJ_CHUNK:D_MODEL + (c + 1) * PROJ_CHUNK].astype(F32) * jnp.dot(
            attn_bf, wba_ref[:, cs], preferred_element_type=F32)
        mrg = mrg + gate_ref[:, 2 * D_MODEL + c * PROJ_CHUNK:2 * D_MODEL + (c + 1) * PROJ_CHUNK].astype(F32) * jnp.dot(
            cross_bf, wbc_ref[:, cs], preferred_element_type=F32)
        merged[:, cs] = mrg.astype(BF16)
    h_ref[...] = x_ref[...] + jnp.dot(merged[...], wout_ref[...], preferred_element_type=F32)


def _mix(x2d, u, qc, gate, outs, lses, kv3, wpool_bf, pscale, wbp, wba, wbc, wout, batch, seq):
    tm = TOKEN_TILE
    ts = seq // tm
    rows = batch * seq
    hb = tm // POOL_HALO
    n_halo = rows // POOL_HALO

    def row(b, i):
        return (b * ts + i, 0)

    def const(b, i):
        return (0, 0)

    tok = lambda w: pl.BlockSpec((tm, w), row)
    in_specs = [
        tok(D_MODEL),
        tok(POOL_WIDTH),
        pl.BlockSpec((POOL_HALO, POOL_WIDTH), lambda b, i: (jnp.maximum((b * ts + i) * hb - 1, 0), 0)),
        pl.BlockSpec((POOL_HALO, POOL_WIDTH), lambda b, i: (jnp.minimum((b * ts + i + 1) * hb, n_halo - 1), 0)),
        tok(CROSS_WIDTH),
        tok(N_BRANCHES * D_MODEL),
        tok(ATTN_OUT_WIDTH), tok(ATTN_OUT_WIDTH), tok(ATTN_OUT_WIDTH),
        tok(ATTN_OUT_WIDTH), tok(ATTN_OUT_WIDTH), tok(ATTN_OUT_WIDTH),
        pl.BlockSpec((1, kv3.shape[1], 2 * CROSS_WIDTH), lambda b, i: (b, 0, 0)),
        pl.BlockSpec((POOL_GROUPS, POOL_GROUP_DIM, POOL_GROUP_DIM), lambda b, i: (0, 0, 0)),
        pl.BlockSpec((1, POOL_WIDTH), const),
        pl.BlockSpec((POOL_WIDTH, D_MODEL), const),
        pl.BlockSpec((ATTN_OUT_WIDTH, D_MODEL), const),
        pl.BlockSpec((CROSS_WIDTH, D_MODEL), const),
        pl.BlockSpec((D_MODEL, D_MODEL), const),
    ]
    return pl.pallas_call(
        functools.partial(_mix_kernel, seq=seq),
        grid=(batch, ts),
        in_specs=in_specs,
        out_specs=pl.BlockSpec((tm, D_MODEL), row),
        out_shape=jax.ShapeDtypeStruct((rows, D_MODEL), F32),
        scratch_shapes=[pltpu.VMEM((tm + 2 * POOL_HALO, POOL_WIDTH), F32), pltpu.VMEM((tm, D_MODEL), BF16)],
        compiler_params=pltpu.CompilerParams(dimension_semantics=("parallel", "parallel"), vmem_limit_bytes=VMEM_LIMIT),
        name="mix",
    )(x2d, u, u, u, qc, gate, *outs, *lses, kv3, wpool_bf, pscale, wbp, wba, wbc, wout)


def _route_kernel(h_ref, g_ref, whi_ref, wlo_ref, b_ref, hn_ref, ri_ref, rg_ref, cnt_ref, carry, tri):
    tm = TOKEN_TILE
    step = pl.program_id(0)

    @pl.when(step == 0)
    def _():
        carry[...] = jnp.zeros_like(carry)
        r = lax.broadcasted_iota(I32, (tm, tm), 0)
        c = lax.broadcasted_iota(I32, (tm, tm), 1)
        tri[...] = jnp.where(c < r, 1.0, 0.0).astype(BF16)

    hn = _rms(h_ref[...], g_ref[...])
    hn_ref[...] = _pack_bf16_pair(hn)
    hi = hn.astype(BF16)
    lo = (hn - hi.astype(F32)).astype(BF16)
    logits = (jnp.dot(hi, whi_ref[...], preferred_element_type=F32)
              + jnp.dot(lo, whi_ref[...], preferred_element_type=F32)
              + jnp.dot(hi, wlo_ref[...], preferred_element_type=F32)) + b_ref[...]

    lane = lax.broadcasted_iota(I32, (tm, N_EXPERTS), 1)
    work = logits
    idxs, vals = [], []
    onehot = jnp.zeros((tm, N_EXPERTS), F32)
    for _ in range(TOP_K):
        m = jnp.max(work, axis=-1, keepdims=True)
        idx = jnp.min(jnp.where(work == m, lane, N_EXPERTS), axis=-1, keepdims=True)
        sel = lane == idx
        onehot = jnp.where(sel, 1.0, onehot)
        work = jnp.where(sel, -jnp.inf, work)
        idxs.append(idx)
        vals.append(m)
    exps = [jnp.exp(v - vals[0]) for v in vals]
    den = exps[0] + exps[1] + exps[2] + exps[3]
    gates = [e / den for e in exps]

    prefix = jnp.dot(tri[...], onehot.astype(BF16), preferred_element_type=F32) + carry[...]
    ranks = [jnp.sum(jnp.where(lane == idx, prefix, 0.0), axis=-1, keepdims=True).astype(I32) for idx in idxs]
    carry[...] = carry[...] + jnp.sum(onehot, axis=0, keepdims=True)

    lane128 = lax.broadcasted_iota(I32, (tm, LANES), 1)
    ri = jnp.zeros((tm, LANES), I32)
    rg = jnp.zeros((tm, LANES), F32)
    for k in range(TOP_K):
        ri = jnp.where(lane128 == k, idxs[k], ri)
        ri = jnp.where(lane128 == TOP_K + k, ranks[k], ri)
        rg = jnp.where(lane128 == k, gates[k], rg)
    ri_ref[...] = ri
    rg_ref[...] = rg
    cnt_ref[...] = carry[...].astype(I32)


def _route(h2d, g, whi, wlo, b):
    rows = h2d.shape[0]
    tm = TOKEN_TILE
    const = lambda i: (0, 0)
    return pl.pallas_call(
        _route_kernel,
        grid=(rows // tm,),
        in_specs=[
            pl.BlockSpec((tm, D_MODEL), lambda i: (i, 0)),
            pl.BlockSpec((1, D_MODEL), const),
            pl.BlockSpec((D_MODEL, N_EXPERTS), const),
            pl.BlockSpec((D_MODEL, N_EXPERTS), const),
            pl.BlockSpec((1, N_EXPERTS), const),
        ],
        out_specs=[
            pl.BlockSpec((tm, PACKED), lambda i: (i, 0)),
            pl.BlockSpec((tm, LANES), lambda i: (i, 0)),
            pl.BlockSpec((tm, LANES), lambda i: (i, 0)),
            pl.BlockSpec((1, N_EXPERTS), const),
        ],
        out_shape=[
            jax.ShapeDtypeStruct((rows, PACKED), I32),
            jax.ShapeDtypeStruct((rows, LANES), I32),
            jax.ShapeDtypeStruct((rows, LANES), F32),
            jax.ShapeDtypeStruct((1, N_EXPERTS), I32),
        ],
        scratch_shapes=[pltpu.VMEM((1, N_EXPERTS), F32), pltpu.VMEM((tm, tm), BF16)],
        compiler_params=pltpu.CompilerParams(dimension_semantics=("arbitrary",), vmem_limit_bytes=VMEM_LIMIT),
        name="route",
    )(h2d, g, whi, wlo, b)


def _sc_mesh():
    return plsc.VectorSubcoreMesh(core_axis_name="c", subcore_axis_name="s")


def _sc_workers():
    info = plsc.get_sparse_core_info()
    return info.num_cores, info.num_cores * info.num_subcores


def _sc_dispatch(rows_packed, slots, n_slots):
    n_tok = rows_packed.shape[0]
    n_cores, n_workers = _sc_workers()
    chunks_per_worker = n_tok // SC_WINDOW // n_workers

    @functools.partial(pl.kernel, out_type=jax.ShapeDtypeStruct((n_slots, PACKED), I32), mesh=_sc_mesh(),
                       scratch_types=[pltpu.VMEM((TOP_K, SC_WINDOW), I32), pltpu.VMEM((SC_WINDOW, PACKED), I32)],
                       name="dispatch")
    def kern(x_hbm, i_hbm, o_hbm, idx_v, rows_v):
        wid = lax.axis_index("s") * n_cores + lax.axis_index("c")

        @pl.loop(0, chunks_per_worker)
        def _(j):
            chunk = wid * chunks_per_worker + j
            pltpu.sync_copy(i_hbm.at[chunk], idx_v)
            pltpu.sync_copy(x_hbm.at[pl.ds(chunk * SC_WINDOW, SC_WINDOW)], rows_v)
            for k in range(TOP_K):
                pltpu.sync_copy(rows_v, o_hbm.at[idx_v.at[k]])

    return kern(rows_packed, slots)


def _sc_combine(ys, slots):
    n_tok = slots.shape[0] * SC_WINDOW
    n_cores, n_workers = _sc_workers()
    chunks_per_worker = n_tok // SC_WINDOW // n_workers

    @functools.partial(pl.kernel, out_type=jax.ShapeDtypeStruct((TOP_K, n_tok, PACKED), I32), mesh=_sc_mesh(),
                       scratch_types=[pltpu.VMEM((TOP_K, SC_WINDOW), I32), pltpu.VMEM((SC_WINDOW, PACKED), I32)],
                       name="combine")
    def kern(y_hbm, i_hbm, o_hbm, idx_v, rows_v):
        wid = lax.axis_index("s") * n_cores + lax.axis_index("c")

        @pl.loop(0, chunks_per_worker)
        def _(j):
            chunk = wid * chunks_per_worker + j
            pltpu.sync_copy(i_hbm.at[chunk], idx_v)
            for k in range(TOP_K):
                pltpu.sync_copy(y_hbm.at[idx_v.at[k]], rows_v)
                pltpu.sync_copy(rows_v, o_hbm.at[k, pl.ds(chunk * SC_WINDOW, SC_WINDOW)])

    return kern(ys, slots)


def _expert_kernel(blk_e_ref, nused_ref, x_ref, wu_ref, bu_ref, wd_ref, bd_ref, o_ref):
    del blk_e_ref

    @pl.when(pl.program_id(0) < nused_ref[0])
    def _():
        lo, hi = _unpack_bf16_pair(x_ref[...])
        x = jnp.concatenate([lo.astype(BF16), hi.astype(BF16)], axis=1)
        y = jnp.zeros((EXPERT_BLOCK, D_MODEL), F32)
        for c in range(D_FF // PROJ_CHUNK):
            gs = slice(c * PROJ_CHUNK, (c + 1) * PROJ_CHUNK)
            us = slice(D_FF + c * PROJ_CHUNK, D_FF + (c + 1) * PROJ_CHUNK)
            gate = jnp.dot(x, wu_ref[0, :, gs], preferred_element_type=F32) + bu_ref[0, :, gs]
            up = jnp.dot(x, wu_ref[0, :, us], preferred_element_type=F32) + bu_ref[0, :, us]
            gate = jnp.minimum(gate, SWIGLU_LIMIT)
            up = jnp.clip(up, -SWIGLU_LIMIT, SWIGLU_LIMIT)
            act = (up + 1.0) * (gate * jax.nn.sigmoid(SWIGLU_ALPHA * gate))
            y = y + jnp.dot(act.astype(BF16), wd_ref[0, gs, :], preferred_element_type=F32)
        o_ref[...] = _pack_bf16_pair(y + bd_ref[0])

    @pl.when(pl.program_id(0) >= nused_ref[0])
    def _():
        o_ref[...] = jnp.zeros_like(o_ref)


def _experts(xs, blk_e, nused, wu, bu, wd, bd):
    n_slots = xs.shape[0]
    tb = EXPERT_BLOCK
    grid_spec = pltpu.PrefetchScalarGridSpec(
        num_scalar_prefetch=2,
        grid=(n_slots // tb,),
        in_specs=[
            pl.BlockSpec((tb, PACKED), lambda i, be, nu: (i, 0)),
            pl.BlockSpec((1, D_MODEL, 2 * D_FF), lambda i, be, nu: (be[i], 0, 0)),
            pl.BlockSpec((1, 1, 2 * D_FF), lambda i, be, nu: (be[i], 0, 0)),
            pl.BlockSpec((1, D_FF, D_MODEL), lambda i, be, nu: (be[i], 0, 0)),
            pl.BlockSpec((1, 1, D_MODEL), lambda i, be, nu: (be[i], 0, 0)),
        ],
        out_specs=pl.BlockSpec((tb, PACKED), lambda i, be, nu: (i, 0)),
    )
    return pl.pallas_call(
        _expert_kernel,
        grid_spec=grid_spec,
        out_shape=jax.ShapeDtypeStruct((n_slots, PACKED), I32),
        compiler_params=pltpu.CompilerParams(dimension_semantics=("parallel",), vmem_limit_bytes=VMEM_LIMIT),
        name="experts",
    )(blk_e, nused, xs, wu, bu, wd, bd)


def _final_kernel(h_ref, yg_ref, rg_ref, g_ref, o_ref):
    rg = rg_ref[...]
    lo = jnp.zeros((TOKEN_TILE, PACKED), F32)
    hi = jnp.zeros((TOKEN_TILE, PACKED), F32)
    for k in range(TOP_K):
        yl, yh = _unpack_bf16_pair(yg_ref[k])
        w = rg[:, k:k + 1]
        lo = lo + w * yl
        hi = hi + w * yh
    x = h_ref[...] + jnp.concatenate([lo, hi], axis=1)
    o_ref[...] = _rms(x, g_ref[...])


def _final(h2d, yg, rg, g):
    rows = h2d.shape[0]
    tm = TOKEN_TILE
    return pl.pallas_call(
        _final_kernel,
        grid=(rows // tm,),
        in_specs=[
            pl.BlockSpec((tm, D_MODEL), lambda i: (i, 0)),
            pl.BlockSpec((TOP_K, tm, PACKED), lambda i: (0, i, 0)),
            pl.BlockSpec((tm, LANES), lambda i: (i, 0)),
            pl.BlockSpec((1, D_MODEL), lambda i: (0, 0)),
        ],
        out_specs=pl.BlockSpec((tm, D_MODEL), lambda i: (i, 0)),
        out_shape=jax.ShapeDtypeStruct((rows, D_MODEL), F32),
        compiler_params=pltpu.CompilerParams(dimension_semantics=("parallel",), vmem_limit_bytes=VMEM_LIMIT),
        name="final",
    )(h2d, yg, rg, g)


def _plan(ri, counts, n_tok):
    tb = EXPERT_BLOCK
    n_slots = n_tok * TOP_K + N_EXPERTS * tb
    eid = ri[:, :TOP_K]
    rank = ri[:, TOP_K:2 * TOP_K]
    c = counts.reshape(N_EXPERTS)
    pc = (c + tb - 1) // tb * tb
    pend = jnp.cumsum(pc)
    pstart = pend - pc
    slots = (pstart[eid] + rank).astype(I32)
    slots = slots.reshape(n_tok // SC_WINDOW, SC_WINDOW, TOP_K).transpose(0, 2, 1)
    blk_start = jnp.arange(n_slots // tb, dtype=I32) * tb
    blk_e = jnp.minimum(jnp.sum(blk_start[:, None] >= pend[None, :], axis=1), N_EXPERTS - 1).astype(I32)
    nused = (pend[-1:] // tb).astype(I32)
    return slots, blk_e, nused, n_slots


def _trunk(x, mem, p):
    batch, seq, _ = x.shape
    n_tok = batch * seq
    x2d = x.reshape(n_tok, D_MODEL)
    kv = _memkv(mem.reshape(-1, D_MODEL), p["norm_mem_g"], p["w_mem_kv"])
    kv3 = kv.reshape(batch, mem.shape[1], 2 * CROSS_WIDTH)
    u, a0, a1, a2, qc, gate = _proj(x2d, p["norm_mix_g"], p["w_in"], _rope_tables(seq), seq)
    outs, lses = [], []
    for g, a_grp in enumerate((a0, a1, a2)):
        o, lse = _dilattn(a_grp, batch, seq, g)
        outs.append(o)
        lses.append(lse)
    h = _mix(x2d, u, qc, gate, outs, lses, kv3, p["w_pool"], p["pool_scale"], p["w_br_pool"], p["w_br_attn"],
             p["w_br_cross"], p["w_out"], batch, seq)
    hn, ri, rg, counts = _route(h, p["norm_ffn_g"], p["w_router_hi"], p["w_router_lo"], p["b_router"])
    slots, blk_e, nused, n_slots = _plan(ri, counts, n_tok)
    xs = _sc_dispatch(hn, slots, n_slots)
    ys = _experts(xs, blk_e, nused, p["w_up"], p["b_up"], p["w_down"], p["b_down"])
    yg = _sc_combine(ys, slots)
    out = _final(h, yg, rg, p["norm_final_g"])
    return out.reshape(batch, seq, D_MODEL)


def _prep_params(norm_mix_g, norm_mem_g, w_in, w_pool, pool_scale, w_mem_kv, w_br_pool, w_br_attn, w_br_cross,
                 w_out, norm_ffn_g, w_router, b_router, w_up, b_up, w_down, b_down, norm_final_g):
    wr = w_router[0]
    wr_hi = wr.astype(BF16)
    return dict(
        norm_mix_g=norm_mix_g[0].reshape(1, D_MODEL),
        norm_mem_g=norm_mem_g[0].reshape(1, D_MODEL),
        w_in=w_in[0].astype(BF16),
        w_pool=w_pool[0].astype(BF16),
        pool_scale=pool_scale[0].reshape(1, POOL_WIDTH),
        w_mem_kv=w_mem_kv[0].astype(BF16),
        w_br_pool=w_br_pool[0].astype(BF16),
        w_br_attn=w_br_attn[0].astype(BF16),
        w_br_cross=w_br_cross[0].astype(BF16),
        w_out=w_out[0].astype(BF16),
        norm_ffn_g=norm_ffn_g[0].reshape(1, D_MODEL),
        w_router_hi=wr_hi,
        w_router_lo=(wr - wr_hi.astype(F32)).astype(BF16),
        b_router=b_router[0].reshape(1, N_EXPERTS),
        w_up=w_up[0].astype(BF16),
        b_up=b_up[0].reshape(N_EXPERTS, 1, 2 * D_FF),
        w_down=w_down[0].astype(BF16),
        b_down=b_down[0].reshape(N_EXPERTS, 1, D_MODEL),
        norm_final_g=norm_final_g.reshape(1, D_MODEL),
    )


def kernel(x_prompt, x_sample, mem_prompt, mem_sample, norm_mix_g, norm_mem_g, w_in, w_pool, pool_scale, w_mem_kv,
           w_br_pool, w_br_attn, w_br_cross, w_out, norm_ffn_g, w_router, b_router, w_up, b_up, w_down, b_down,
           norm_final_g):
    p = _prep_params(norm_mix_g, norm_mem_g, w_in, w_pool, pool_scale, w_mem_kv, w_br_pool, w_br_attn, w_br_cross,
                     w_out, norm_ffn_g, w_router, b_router, w_up, b_up, w_down, b_down, norm_final_g)
    y_prompt = _trunk(x_prompt, mem_prompt, p)
    y_sample = _trunk(x_sample, mem_sample, p)
    return (y_prompt, y_sample)
```

```python
import functools
import math

import jax
import jax.numpy as jnp
from jax import lax
from jax.experimental import pallas as pl
from jax.experimental.pallas import tpu as pltpu
from jax.experimental.pallas import tpu_sc as plsc

F32 = jnp.float32
BF16 = jnp.bfloat16
I32 = jnp.int32
U32 = jnp.uint32

D_MODEL = 1024
POOL_GROUPS = 4
POOL_WIDTH = 512
POOL_GROUP_DIM = 128
POOL_WINDOWS = (2, 4, 8, 16)
POOL_HALO = 16
HEAD_DIM = 64
DIL_CONFIGS = ((128, 1), (512, 4), (2048, 16))
HEADS_PER_DIL = 4
ATTN_WIDTH = 768
ATTN_OUT_WIDTH = 256
BAND_BLOCK = 64
ROPE_DIM = 16
ROPE_THETA = 500000.0
CROSS_HEADS = 4
CROSS_HEAD_DIM = 128
CROSS_WIDTH = 512
N_BRANCHES = 3
QKV_WIDTH = 3 * ATTN_WIDTH
IN_WIDTH = POOL_WIDTH + QKV_WIDTH + CROSS_WIDTH + N_BRANCHES * D_MODEL
N_EXPERTS = 32
TOP_K = 4
D_FF = 1024
SWIGLU_LIMIT = 7.0
SWIGLU_ALPHA = 1.702
EPS = 1e-5
NEG_INF = -1e30

LANES = 128
PACKED = D_MODEL // 2
TOKEN_TILE = 512
PROJ_CHUNK = 256
EXPERT_BLOCK = 512
SC_WINDOW = 128
VMEM_LIMIT = 52 * 1024 * 1024


def _rms(x, g):
    r = lax.rsqrt(jnp.mean(x * x, axis=-1, keepdims=True) + EPS)
    return x * r * g


def _pack_bf16_pair(x):
    bits = lax.bitcast_convert_type(x.astype(BF16).astype(F32), U32)
    packed = (bits[:, :PACKED] >> 16) | bits[:, PACKED:]
    return lax.bitcast_convert_type(packed, I32)


def _unpack_bf16_pair(w):
    u = lax.bitcast_convert_type(w, U32)
    lo = lax.bitcast_convert_type(u << 16, F32)
    hi = lax.bitcast_convert_type(u & jnp.uint32(0xFFFF0000), F32)
    return lo, hi


def _memkv_kernel(mem_ref, g_ref, w_ref, o_ref):
    xn = _rms(mem_ref[...], g_ref[...]).astype(BF16)
    o_ref[...] = jnp.dot(xn, w_ref[...], preferred_element_type=F32).astype(BF16)


def _memkv(mem2d, g, w_bf):
    rows = mem2d.shape[0]
    tm = 256
    return pl.pallas_call(
        _memkv_kernel,
        grid=(rows // tm,),
        in_specs=[
            pl.BlockSpec((tm, D_MODEL), lambda i: (i, 0)),
            pl.BlockSpec((1, D_MODEL), lambda i: (0, 0)),
            pl.BlockSpec((D_MODEL, 2 * CROSS_WIDTH), lambda i: (0, 0)),
        ],
        out_specs=pl.BlockSpec((tm, 2 * CROSS_WIDTH), lambda i: (i, 0)),
        out_shape=jax.ShapeDtypeStruct((rows, 2 * CROSS_WIDTH), BF16),
        compiler_params=pltpu.CompilerParams(dimension_semantics=("parallel",)),
        name="memkv",
    )(mem2d, g, w_bf)


_O_QKV = POOL_WIDTH
_O_QC = _O_QKV + QKV_WIDTH
_O_GATE = _O_QC + CROSS_WIDTH


def _proj_kernel(x_ref, g_ref, w_ref, tab_ref, u_ref, a0_ref, a1_ref, a2_ref, qc_ref, gate_ref, xcols):
    tm = TOKEN_TILE
    n_cols = D_MODEL // LANES
    g = g_ref[...]
    xn = _rms(x_ref[...], g).astype(BF16)
    for c in range(n_cols):
        xcols[c] = x_ref[:, c * LANES:(c + 1) * LANES]

    def mm(lhs, c0):
        return jnp.dot(lhs, w_ref[:, c0:c0 + PROJ_CHUNK], preferred_element_type=F32)

    for j in range(POOL_WIDTH // PROJ_CHUNK):
        u_ref[:, j * PROJ_CHUNK:(j + 1) * PROJ_CHUNK] = mm(xn, j * PROJ_CHUNK)

    for grp, a_ref in enumerate((a0_ref, a1_ref, a2_ref)):
        dil = DIL_CONFIGS[grp][1]
        res_rows = tm // dil
        if dil == 1:
            lhs = xn
        else:
            xp = jnp.concatenate(
                [jnp.concatenate([xcols[c, pl.ds(r, res_rows, stride=dil), :] for c in range(n_cols)], axis=1)
                 for r in range(dil)], axis=0)
            lhs = _rms(xp, g).astype(BF16)
        cos, s1, s2 = tab_ref[0, grp], tab_ref[1, grp], tab_ref[2, grp]
        for which in range(3):
            t = mm(lhs, _O_QKV + which * ATTN_WIDTH + grp * ATTN_OUT_WIDTH)
            if which < 2:
                halves = []
                for hh in range(PROJ_CHUNK // LANES):
                    th = t[:, hh * LANES:(hh + 1) * LANES]
                    th = (th * cos + pltpu.roll(th, LANES - ROPE_DIM // 2, axis=1) * s1
                          + pltpu.roll(th, ROPE_DIM // 2, axis=1) * s2)
                    halves.append(th)
                t = jnp.concatenate(halves, axis=1)
                if which == 0:
                    t = t * (1.0 / math.sqrt(HEAD_DIM))
            tb = t.astype(BF16)
            for r in range(dil):
                c0 = (r * 3 + which) * ATTN_OUT_WIDTH
                a_ref[:, c0:c0 + ATTN_OUT_WIDTH] = tb[r * res_rows:(r + 1) * res_rows, :]

    for j in range(CROSS_WIDTH // PROJ_CHUNK):
        qc_ref[:, j * PROJ_CHUNK:(j + 1) * PROJ_CHUNK] = mm(xn, _O_QC + j * PROJ_CHUNK).astype(BF16)

    for j in range(N_BRANCHES * D_MODEL // PROJ_CHUNK):
        gate_ref[:, j * PROJ_CHUNK:(j + 1) * PROJ_CHUNK] = jax.nn.sigmoid(mm(xn, _O_GATE + j * PROJ_CHUNK)).astype(BF16)


def _proj(x2d, g, w_in_bf, tabs, seq):
    rows = x2d.shape[0]
    tm = TOKEN_TILE
    tiles_per_seq = seq // tm
    n_grp = len(DIL_CONFIGS)
    a_specs = [pl.BlockSpec((tm // d, d * ATTN_WIDTH), lambda i: (i, 0)) for _, d in DIL_CONFIGS]
    a_shapes = [jax.ShapeDtypeStruct((rows // d, d * ATTN_WIDTH), BF16) for _, d in DIL_CONFIGS]
    return pl.pallas_call(
        _proj_kernel,
        grid=(rows // tm,),
        in_specs=[
            pl.BlockSpec((tm, D_MODEL), lambda i: (i, 0)),
            pl.BlockSpec((1, D_MODEL), lambda i: (0, 0)),
            pl.BlockSpec((D_MODEL, IN_WIDTH), lambda i: (0, 0), pipeline_mode=pl.Buffered(1)),
            pl.BlockSpec((3, n_grp, tm, LANES), lambda i: (0, 0, i % tiles_per_seq, 0)),
        ],
        out_specs=[
            pl.BlockSpec((tm, POOL_WIDTH), lambda i: (i, 0)),
            *a_specs,
            pl.BlockSpec((tm, CROSS_WIDTH), lambda i: (i, 0)),
            pl.BlockSpec((tm, N_BRANCHES * D_MODEL), lambda i: (i, 0)),
        ],
        out_shape=[
            jax.ShapeDtypeStruct((rows, POOL_WIDTH), F32),
            *a_shapes,
            jax.ShapeDtypeStruct((rows, CROSS_WIDTH), BF16),
            jax.ShapeDtypeStruct((rows, N_BRANCHES * D_MODEL), BF16),
        ],
        scratch_shapes=[pltpu.VMEM((D_MODEL // LANES, tm, LANES), F32)],
        compiler_params=pltpu.CompilerParams(dimension_semantics=("parallel",), vmem_limit_bytes=VMEM_LIMIT),
        name="proj",
    )(x2d, g, w_in_bf, tabs)


def _rope_tables(seq):
    pos = jnp.arange(seq, dtype=F32)
    inv = jnp.power(jnp.float32(ROPE_THETA), -jnp.arange(0, ROPE_DIM, 2, dtype=F32) / ROPE_DIM)
    ang = pos[:, None] * inv[None, :]
    half = ROPE_DIM // 2
    j = jnp.arange(LANES) % HEAD_DIM
    cos_l = jnp.cos(ang)[:, j % half]
    sin_l = jnp.sin(ang)[:, j % half]
    kinds = jnp.stack([
        jnp.where(j[None, :] < ROPE_DIM, cos_l, 1.0),
        jnp.where(j[None, :] < half, -sin_l, 0.0),
        jnp.where((j[None, :] >= half) & (j[None, :] < ROPE_DIM), sin_l, 0.0),
    ]).astype(F32)
    tm = TOKEN_TILE
    per_group = []
    for _, d in DIL_CONFIGS:
        t = kinds.reshape(3, seq // tm, tm // d, d, LANES).transpose(0, 1, 3, 2, 4).reshape(3, seq, LANES)
        per_group.append(t)
    return jnp.stack(per_group, axis=1)


def _dilattn_kernel(a_ref, o_ref, lse_ref, *, res_len, tq, win, half, n_res):
    t = pl.program_id(2)
    nblk = tq // BAND_BLOCK
    stacked = HEADS_PER_DIL * BAND_BLOCK
    lane_head = lax.broadcasted_iota(I32, (BAND_BLOCK, ATTN_OUT_WIDTH), 1) // HEAD_DIM
    head_sel = [lane_head == h for h in range(HEADS_PER_DIL)]
    head_mask = [jnp.where(sel, 1.0, 0.0).astype(BF16) for sel in head_sel]
    jmi = (lax.broadcasted_iota(I32, (stacked, win), 1)
           - (lax.broadcasted_iota(I32, (stacked, win), 0) & (BAND_BLOCK - 1)))
    ones = jnp.ones((win, LANES), BF16)
    blocks = [(rr, n) for rr in range(n_res) for n in range(nblk)]

    s_parts, v_wins = [], []
    for rr, n in blocks:
        c0 = rr * ATTN_WIDTH
        qpos0 = pl.multiple_of(t * tq + n * BAND_BLOCK, BAND_BLOCK)
        start = pl.multiple_of(jnp.clip(qpos0 - BAND_BLOCK, 0, res_len - win), BAND_BLOCK)
        qb = a_ref[0, pl.ds(qpos0, BAND_BLOCK), c0:c0 + ATTN_OUT_WIDTH]
        kw = a_ref[0, pl.ds(start, win), c0 + ATTN_OUT_WIDTH:c0 + 2 * ATTN_OUT_WIDTH]
        v_wins.append(a_ref[0, pl.ds(start, win), c0 + 2 * ATTN_OUT_WIDTH:c0 + 3 * ATTN_OUT_WIDTH])
        qs = jnp.concatenate([qb * hm for hm in head_mask], axis=0)
        s = lax.dot_general(qs, kw, (((1,), (1,)), ((), ())), preferred_element_type=F32)
        diff = jmi + (start - qpos0)
        s_parts.append(jnp.where((diff >= -half) & (diff <= half), s, NEG_INF))
    s_all = jnp.concatenate(s_parts, axis=0)
    m_all = jnp.max(s_all, axis=-1, keepdims=True)
    p_all = jnp.exp(s_all - m_all).astype(BF16)

    for i, (rr, n) in enumerate(blocks):
        p = p_all[i * stacked:(i + 1) * stacked]
        o_st = jnp.dot(p, v_wins[i], preferred_element_type=F32)
        l_st = jnp.dot(p, ones, preferred_element_type=F32)
        m_st = m_all[i * stacked:(i + 1) * stacked]
        o = jnp.zeros((BAND_BLOCK, ATTN_OUT_WIDTH), F32)
        l = jnp.ones((BAND_BLOCK, ATTN_OUT_WIDTH), F32)
        m = jnp.zeros((BAND_BLOCK, ATTN_OUT_WIDTH), F32)
        for h in range(HEADS_PER_DIL):
            rs = slice(h * BAND_BLOCK, (h + 1) * BAND_BLOCK)
            o = jnp.where(head_sel[h], o_st[rs], o)
            l = jnp.where(head_sel[h], jnp.concatenate([l_st[rs], l_st[rs]], axis=1), l)
            m = jnp.where(head_sel[h], m_st[rs], m)
        rows = slice(n * BAND_BLOCK, (n + 1) * BAND_BLOCK)
        cols = slice(rr * ATTN_OUT_WIDTH, (rr + 1) * ATTN_OUT_WIDTH)
        o_ref[0, rows, cols] = (o / l).astype(BF16)
        lse_ref[0, rows, cols] = m + jnp.log(l)


ATTN_BLOCKS_PER_STEP = 16


def _dilattn(a_grp, batch, seq, group):
    window, dil = DIL_CONFIGS[group]
    half = window // (2 * dil)
    res_len = seq // dil
    win = min(3 * BAND_BLOCK, res_len)
    n_res = min(dil, 4)
    tq = min(res_len, ATTN_BLOCKS_PER_STEP // n_res * BAND_BLOCK)
    qkv3 = a_grp.reshape(batch, res_len, dil * ATTN_WIDTH)
    kern = functools.partial(_dilattn_kernel, res_len=res_len, tq=tq, win=win, half=half, n_res=n_res)
    out_spec = pl.BlockSpec((1, tq, n_res * ATTN_OUT_WIDTH), lambda b, r, t: (b, t, r))
    o, lse = pl.pallas_call(
        kern,
        grid=(batch, dil // n_res, res_len // tq),
        in_specs=[pl.BlockSpec((1, res_len, n_res * ATTN_WIDTH), lambda b, r, t: (b, 0, r))],
        out_specs=[out_spec, out_spec],
        out_shape=[
            jax.ShapeDtypeStruct((batch, res_len, dil * ATTN_OUT_WIDTH), BF16),
            jax.ShapeDtypeStruct((batch, res_len, dil * ATTN_OUT_WIDTH), F32),
        ],
        compiler_params=pltpu.CompilerParams(dimension_semantics=("parallel", "parallel", "parallel"),
                                             vmem_limit_bytes=VMEM_LIMIT),
        name=f"dilattn{group}",
    )(qkv3)
    return o.reshape(batch * seq, ATTN_OUT_WIDTH), lse.reshape(batch * seq, ATTN_OUT_WIDTH)


def _mix_kernel(x_ref, u_ref, up_ref, un_ref, qc_ref, gate_ref, o0_ref, o1_ref, o2_ref, l0_ref, l1_ref, l2_ref,
                kv_ref, wpool_ref, pscale_ref, wbp_ref, wba_ref, wbc_ref, wout_ref, h_ref, ubuf, merged, *, seq):
    tm = TOKEN_TILE
    i = pl.program_id(1)
    nt = pl.num_programs(1)

    ubuf[0:POOL_HALO, :] = jnp.where(i > 0, up_ref[...], 0.0)
    ubuf[POOL_HALO:POOL_HALO + tm, :] = u_ref[...]
    ubuf[POOL_HALO + tm:, :] = jnp.where(i < nt - 1, un_ref[...], 0.0)
    pos = i * tm + lax.broadcasted_iota(I32, (tm, POOL_GROUP_DIM), 0)
    pool_parts = []
    for g, w in enumerate(POOL_WINDOWS):
        cs = slice(g * POOL_GROUP_DIM, (g + 1) * POOL_GROUP_DIM)
        acc = ubuf[POOL_HALO - w // 2:POOL_HALO - w // 2 + tm, cs]
        for j in range(1, w):
            off = POOL_HALO - w // 2 + j
            acc = acc + ubuf[off:off + tm, cs]
        cnt = (jnp.minimum(pos + w // 2, seq) - jnp.maximum(pos - w // 2, 0)).astype(F32)
        z = acc / cnt - ubuf[POOL_HALO:POOL_HALO + tm, cs]
        zp = jnp.dot(z.astype(BF16), wpool_ref[g], preferred_element_type=F32)
        pool_parts.append(zp * pscale_ref[:, cs])
    pool_bf = jnp.concatenate(pool_parts, axis=1).astype(BF16)

    cross_parts = []
    for h in range(CROSS_HEADS):
        cs = slice(h * CROSS_HEAD_DIM, (h + 1) * CROSS_HEAD_DIM)
        kh = kv_ref[0, :, cs]
        vh = kv_ref[0, :, CROSS_WIDTH + h * CROSS_HEAD_DIM:CROSS_WIDTH + (h + 1) * CROSS_HEAD_DIM]
        s = lax.dot_general(qc_ref[:, cs], kh, (((1,), (1,)), ((), ())), preferred_element_type=F32)
        s = s * (1.0 / math.sqrt(CROSS_HEAD_DIM))
        m = jnp.max(s, axis=-1, keepdims=True)
        p = jnp.exp(s - m)
        l = jnp.sum(p, axis=-1, keepdims=True)
        cross_parts.append(jnp.dot(p.astype(BF16), vh, preferred_element_type=F32) / l)
    cross_bf = jnp.concatenate(cross_parts, axis=1).astype(BF16)

    l0, l1, l2 = l0_ref[...], l1_ref[...], l2_ref[...]
    mx = jnp.maximum(jnp.maximum(l0, l1), l2)
    e0, e1, e2 = jnp.exp(l0 - mx), jnp.exp(l1 - mx), jnp.exp(l2 - mx)
    attn = (e0 * o0_ref[...].astype(F32) + e1 * o1_ref[...].astype(F32) + e2 * o2_ref[...].astype(F32)) / (e0 + e1 + e2)
    attn_bf = attn.astype(BF16)

    for c in range(D_MODEL // PROJ_CHUNK):
        cs = slice(c * PROJ_CHUNK, (c + 1) * PROJ_CHUNK)
        mrg = gate_ref[:, cs].astype(F32) * jnp.dot(pool_bf, wbp_ref[:, cs], preferred_element_type=F32)
        mrg = mrg + gate_ref[:, D_MODEL + c * PROJ_CHUNK:D_MODEL + (c + 1) * PROJ_CHUNK].astype(F32) * jnp.dot(
            attn_bf, wba_ref[:, cs], preferred_element_type=F32)
        mrg = mrg + gate_ref[:, 2 * D_MODEL + c * PROJ_CHUNK:2 * D_MODEL + (c + 1) * PROJ_CHUNK].astype(F32) * jnp.dot(
            cross_bf, wbc_ref[:, cs], preferred_element_type=F32)
        merged[:, cs] = mrg.astype(BF16)
    h_ref[...] = x_ref[...] + jnp.dot(merged[...], wout_ref[...], preferred_element_type=F32)


def _mix(x2d, u, qc, gate, outs, lses, kv3, wpool_bf, pscale, wbp, wba, wbc, wout, batch, seq):
    tm = TOKEN_TILE
    ts = seq // tm
    rows = batch * seq
    hb = tm // POOL_HALO
    n_halo = rows // POOL_HALO

    def row(b, i):
        return (b * ts + i, 0)

    def const(b, i):
        return (0, 0)

    tok = lambda w: pl.BlockSpec((tm, w), row)
    in_specs = [
        tok(D_MODEL),
        tok(POOL_WIDTH),
        pl.BlockSpec((POOL_HALO, POOL_WIDTH), lambda b, i: (jnp.maximum((b * ts + i) * hb - 1, 0), 0)),
        pl.BlockSpec((POOL_HALO, POOL_WIDTH), lambda b, i: (jnp.minimum((b * ts + i + 1) * hb, n_halo - 1), 0)),
        tok(CROSS_WIDTH),
        tok(N_BRANCHES * D_MODEL),
        tok(ATTN_OUT_WIDTH), tok(ATTN_OUT_WIDTH), tok(ATTN_OUT_WIDTH),
        tok(ATTN_OUT_WIDTH), tok(ATTN_OUT_WIDTH), tok(ATTN_OUT_WIDTH),
        pl.BlockSpec((1, kv3.shape[1], 2 * CROSS_WIDTH), lambda b, i: (b, 0, 0)),
        pl.BlockSpec((POOL_GROUPS, POOL_GROUP_DIM, POOL_GROUP_DIM), lambda b, i: (0, 0, 0)),
        pl.BlockSpec((1, POOL_WIDTH), const),
        pl.BlockSpec((POOL_WIDTH, D_MODEL), const),
        pl.BlockSpec((ATTN_OUT_WIDTH, D_MODEL), const),
        pl.BlockSpec((CROSS_WIDTH, D_MODEL), const),
        pl.BlockSpec((D_MODEL, D_MODEL), const),
    ]
    return pl.pallas_call(
        functools.partial(_mix_kernel, seq=seq),
        grid=(batch, ts),
        in_specs=in_specs,
        out_specs=pl.BlockSpec((tm, D_MODEL), row),
        out_shape=jax.ShapeDtypeStruct((rows, D_MODEL), F32),
        scratch_shapes=[pltpu.VMEM((tm + 2 * POOL_HALO, POOL_WIDTH), F32), pltpu.VMEM((tm, D_MODEL), BF16)],
        compiler_params=pltpu.CompilerParams(dimension_semantics=("parallel", "parallel"), vmem_limit_bytes=VMEM_LIMIT),
        name="mix",
    )(x2d, u, u, u, qc, gate, *outs, *lses, kv3, wpool_bf, pscale, wbp, wba, wbc, wout)


def _route_kernel(h_ref, g_ref, whi_ref, wlo_ref, b_ref, hn_ref, ri_ref, rg_ref, cnt_ref, carry, tri):
    tm = TOKEN_TILE
    step = pl.program_id(0)

    @pl.when(step == 0)
    def _():
        carry[...] = jnp.zeros_like(carry)
        r = lax.broadcasted_iota(I32, (tm, tm), 0)
        c = lax.broadcasted_iota(I32, (tm, tm), 1)
        tri[...] = jnp.where(r < c, 1.0, 0.0).astype(BF16)

    hn = _rms(h_ref[...], g_ref[...])
    hn_ref[...] = _pack_bf16_pair(hn)
    hi = hn.astype(BF16)
    lo = (hn - hi.astype(F32)).astype(BF16)
    logits = (jnp.dot(hi, whi_ref[...], preferred_element_type=F32)
              + jnp.dot(lo, whi_ref[...], preferred_element_type=F32)
              + jnp.dot(hi, wlo_ref[...], preferred_element_type=F32)) + b_ref[...]
    work = logits.T[:N_EXPERTS]
    row = lax.broadcasted_iota(I32, (N_EXPERTS, tm), 0).astype(F32)
    idxs, vals = [], []
    onehot = jnp.zeros((N_EXPERTS, tm), F32)
    for _ in range(TOP_K):
        m = jnp.max(work, axis=0, keepdims=True)
        idx = jnp.min(jnp.where(work == m, row, float(N_EXPERTS)), axis=0, keepdims=True)
        sel = row == idx
        onehot = jnp.where(sel, 1.0, onehot)
        work = jnp.where(sel, -jnp.inf, work)
        idxs.append(idx)
        vals.append(m)
    exps = [jnp.exp(v - vals[0]) for v in vals]
    den = exps[0] + exps[1] + exps[2] + exps[3]
    gates = [e / den for e in exps]

    prefix = jnp.dot(onehot.astype(BF16), tri[...], preferred_element_type=F32) + carry[:, 0:1]
    ranks = [jnp.sum(jnp.where(row == idx, prefix, 0.0), axis=0, keepdims=True) for idx in idxs]
    carry[...] = carry[...] + jnp.sum(onehot, axis=1, keepdims=True)

    row8 = lax.broadcasted_iota(I32, (2 * TOP_K, tm), 0)
    ri = jnp.zeros((2 * TOP_K, tm), F32)
    row128 = lax.broadcasted_iota(I32, (LANES, tm), 0)
    rg = jnp.zeros((LANES, tm), F32)
    for k in range(TOP_K):
        ri = jnp.where(row8 == k, idxs[k], ri)
        ri = jnp.where(row8 == TOP_K + k, ranks[k], ri)
        rg = jnp.where(row128 == k, gates[k], rg)
    ri_ref[...] = ri.astype(I32)
    rg_ref[...] = rg.T
    cnt_ref[...] = carry[...].astype(I32)


def _route(h2d, g, whi, wlo, b):
    rows = h2d.shape[0]
    tm = TOKEN_TILE
    const = lambda i: (0, 0)
    return pl.pallas_call(
        _route_kernel,
        grid=(rows // tm,),
        in_specs=[
            pl.BlockSpec((tm, D_MODEL), lambda i: (i, 0)),
            pl.BlockSpec((1, D_MODEL), const),
            pl.BlockSpec((D_MODEL, LANES), const),
            pl.BlockSpec((D_MODEL, LANES), const),
            pl.BlockSpec((1, LANES), const),
        ],
        out_specs=[
            pl.BlockSpec((tm, PACKED), lambda i: (i, 0)),
            pl.BlockSpec((2 * TOP_K, tm), lambda i: (0, i)),
            pl.BlockSpec((tm, LANES), lambda i: (i, 0)),
            pl.BlockSpec((N_EXPERTS, LANES), const),
        ],
        out_shape=[
            jax.ShapeDtypeStruct((rows, PACKED), I32),
            jax.ShapeDtypeStruct((2 * TOP_K, rows), I32),
            jax.ShapeDtypeStruct((rows, LANES), F32),
            jax.ShapeDtypeStruct((N_EXPERTS, LANES), I32),
        ],
        scratch_shapes=[pltpu.VMEM((N_EXPERTS, LANES), F32), pltpu.VMEM((tm, tm), BF16)],
        compiler_params=pltpu.CompilerParams(dimension_semantics=("arbitrary",), vmem_limit_bytes=VMEM_LIMIT),
        name="route",
    )(h2d, g, whi, wlo, b)


def _sc_mesh():
    return plsc.VectorSubcoreMesh(core_axis_name="c", subcore_axis_name="s")


def _sc_workers():
    info = plsc.get_sparse_core_info()
    return info.num_cores, info.num_cores * info.num_subcores


def _sc_dispatch(rows_packed, slots, n_slots):
    n_tok = rows_packed.shape[0]
    n_cores, n_workers = _sc_workers()
    chunks_per_worker = n_tok // SC_WINDOW // n_workers

    @functools.partial(pl.kernel, out_type=jax.ShapeDtypeStruct((n_slots, PACKED), I32), mesh=_sc_mesh(),
                       scratch_types=[pltpu.VMEM((TOP_K, SC_WINDOW), I32), pltpu.VMEM((SC_WINDOW, PACKED), I32)],
                       name="dispatch")
    def kern(x_hbm, i_hbm, o_hbm, idx_v, rows_v):
        wid = lax.axis_index("s") * n_cores + lax.axis_index("c")

        @pl.loop(0, chunks_per_worker)
        def _(j):
            chunk = wid * chunks_per_worker + j
            pltpu.sync_copy(i_hbm.at[chunk], idx_v)
            pltpu.sync_copy(x_hbm.at[pl.ds(chunk * SC_WINDOW, SC_WINDOW)], rows_v)
            for k in range(TOP_K):
                pltpu.sync_copy(rows_v, o_hbm.at[idx_v.at[k]])

    return kern(rows_packed, slots)


def _sc_combine(ys, slots):
    n_tok = slots.shape[0] * SC_WINDOW
    n_cores, n_workers = _sc_workers()
    chunks_per_worker = n_tok // SC_WINDOW // n_workers

    @functools.partial(pl.kernel, out_type=jax.ShapeDtypeStruct((TOP_K, n_tok, PACKED), I32), mesh=_sc_mesh(),
                       scratch_types=[pltpu.VMEM((TOP_K, SC_WINDOW), I32), pltpu.VMEM((SC_WINDOW, PACKED), I32)],
                       name="combine")
    def kern(y_hbm, i_hbm, o_hbm, idx_v, rows_v):
        wid = lax.axis_index("s") * n_cores + lax.axis_index("c")

        @pl.loop(0, chunks_per_worker)
        def _(j):
            chunk = wid * chunks_per_worker + j
            pltpu.sync_copy(i_hbm.at[chunk], idx_v)
            for k in range(TOP_K):
                pltpu.sync_copy(y_hbm.at[idx_v.at[k]], rows_v)
                pltpu.sync_copy(rows_v, o_hbm.at[k, pl.ds(chunk * SC_WINDOW, SC_WINDOW)])

    return kern(ys, slots)


def _expert_kernel(blk_e_ref, nused_ref, x_ref, wu_ref, bu_ref, wd_ref, bd_ref, o_ref, wu_bf, wd_bf):
    i = pl.program_id(0)
    used = i < nused_ref[0]

    @pl.when(used & ((i == 0) | (blk_e_ref[i] != blk_e_ref[jnp.maximum(i - 1, 0)])))
    def _():
        for c in range(2 * D_FF // PROJ_CHUNK):
            cs = slice(c * PROJ_CHUNK, (c + 1) * PROJ_CHUNK)
            wu_bf[:, cs] = wu_ref[0, :, cs].astype(BF16)
        for c in range(D_MODEL // PROJ_CHUNK):
            cs = slice(c * PROJ_CHUNK, (c + 1) * PROJ_CHUNK)
            wd_bf[:, cs] = wd_ref[0, :, cs].astype(BF16)

    @pl.when(used)
    def _():
        lo, hi = _unpack_bf16_pair(x_ref[...])
        x = jnp.concatenate([lo.astype(BF16), hi.astype(BF16)], axis=1)
        y = jnp.zeros((EXPERT_BLOCK, D_MODEL), F32)
        for c in range(D_FF // PROJ_CHUNK):
            gs = slice(c * PROJ_CHUNK, (c + 1) * PROJ_CHUNK)
            us = slice(D_FF + c * PROJ_CHUNK, D_FF + (c + 1) * PROJ_CHUNK)
            gate = jnp.dot(x, wu_bf[:, gs], preferred_element_type=F32) + bu_ref[0, :, gs]
            up = jnp.dot(x, wu_bf[:, us], preferred_element_type=F32) + bu_ref[0, :, us]
            gate = jnp.minimum(gate, SWIGLU_LIMIT)
            up = jnp.clip(up, -SWIGLU_LIMIT, SWIGLU_LIMIT)
            act = (up + 1.0) * (gate * jax.nn.sigmoid(SWIGLU_ALPHA * gate))
            y = y + jnp.dot(act.astype(BF16), wd_bf[gs, :], preferred_element_type=F32)
        o_ref[...] = _pack_bf16_pair(y + bd_ref[0])

    @pl.when(jnp.logical_not(used))
    def _():
        o_ref[...] = jnp.zeros_like(o_ref)


def _experts(xs, blk_e, nused, wu, bu, wd, bd):
    n_slots = xs.shape[0]
    tb = EXPERT_BLOCK
    grid_spec = pltpu.PrefetchScalarGridSpec(
        num_scalar_prefetch=2,
        grid=(n_slots // tb,),
        in_specs=[
            pl.BlockSpec((tb, PACKED), lambda i, be, nu: (i, 0)),
            pl.BlockSpec((1, D_MODEL, 2 * D_FF), lambda i, be, nu: (be[i], 0, 0)),
            pl.BlockSpec((1, 1, 2 * D_FF), lambda i, be, nu: (be[i], 0, 0)),
            pl.BlockSpec((1, D_FF, D_MODEL), lambda i, be, nu: (be[i], 0, 0)),
            pl.BlockSpec((1, 1, D_MODEL), lambda i, be, nu: (be[i], 0, 0)),
        ],
        out_specs=pl.BlockSpec((tb, PACKED), lambda i, be, nu: (i, 0)),
        scratch_shapes=[pltpu.VMEM((D_MODEL, 2 * D_FF), BF16), pltpu.VMEM((D_FF, D_MODEL), BF16)],
    )
    return pl.pallas_call(
        _expert_kernel,
        grid_spec=grid_spec,
        out_shape=jax.ShapeDtypeStruct((n_slots, PACKED), I32),
        compiler_params=pltpu.CompilerParams(dimension_semantics=("arbitrary",), vmem_limit_bytes=VMEM_LIMIT),
        name="experts",
    )(blk_e, nused, xs, wu, bu, wd, bd)


def _final_kernel(h_ref, yg_ref, rg_ref, g_ref, o_ref):
    rg = rg_ref[...]
    lo = jnp.zeros((TOKEN_TILE, PACKED), F32)
    hi = jnp.zeros((TOKEN_TILE, PACKED), F32)
    for k in range(TOP_K):
        yl, yh = _unpack_bf16_pair(yg_ref[k])
        w = rg[:, k:k + 1]
        lo = lo + w * yl
        hi = hi + w * yh
    x = h_ref[...] + jnp.concatenate([lo, hi], axis=1)
    o_ref[...] = _rms(x, g_ref[...])


def _final(h2d, yg, rg, g):
    rows = h2d.shape[0]
    tm = TOKEN_TILE
    return pl.pallas_call(
        _final_kernel,
        grid=(rows // tm,),
        in_specs=[
            pl.BlockSpec((tm, D_MODEL), lambda i: (i, 0)),
            pl.BlockSpec((TOP_K, tm, PACKED), lambda i: (0, i, 0)),
            pl.BlockSpec((tm, LANES), lambda i: (i, 0)),
            pl.BlockSpec((1, D_MODEL), lambda i: (0, 0)),
        ],
        out_specs=pl.BlockSpec((tm, D_MODEL), lambda i: (i, 0)),
        out_shape=jax.ShapeDtypeStruct((rows, D_MODEL), F32),
        compiler_params=pltpu.CompilerParams(dimension_semantics=("parallel",), vmem_limit_bytes=VMEM_LIMIT),
        name="final",
    )(h2d, yg, rg, g)


def _plan(ri, counts, n_tok):
    tb = EXPERT_BLOCK
    n_slots = n_tok * TOP_K + N_EXPERTS * tb
    eid = ri[:TOP_K]
    rank = ri[TOP_K:]
    c = counts[:, 0]
    pc = (c + tb - 1) // tb * tb
    pend = jnp.cumsum(pc)
    pstart = pend - pc
    base = jnp.zeros_like(eid)
    for e in range(N_EXPERTS):
        base = jnp.where(eid == e, pstart[e], base)
    slots = (base + rank).astype(I32)
    slots = slots.reshape(TOP_K, n_tok // SC_WINDOW, SC_WINDOW).transpose(1, 0, 2)
    blk_start = jnp.arange(n_slots // tb, dtype=I32) * tb
    blk_e = jnp.minimum(jnp.sum(blk_start[:, None] >= pend[None, :], axis=1), N_EXPERTS - 1).astype(I32)
    nused = (pend[-1:] // tb).astype(I32)
    return slots, blk_e, nused, n_slots


def _trunk(x, mem, p):
    batch, seq, _ = x.shape
    n_tok = batch * seq
    x2d = x.reshape(n_tok, D_MODEL)
    kv = _memkv(mem.reshape(-1, D_MODEL), p["norm_mem_g"], p["w_mem_kv"])
    kv3 = kv.reshape(batch, mem.shape[1], 2 * CROSS_WIDTH)
    u, a0, a1, a2, qc, gate = _proj(x2d, p["norm_mix_g"], p["w_in"], _rope_tables(seq), seq)
    outs, lses = [], []
    for g, a_grp in enumerate((a0, a1, a2)):
        o, lse = _dilattn(a_grp, batch, seq, g)
        outs.append(o)
        lses.append(lse)
    h = _mix(x2d, u, qc, gate, outs, lses, kv3, p["w_pool"], p["pool_scale"], p["w_br_pool"], p["w_br_attn"],
             p["w_br_cross"], p["w_out"], batch, seq)
    hn, ri, rg, counts = _route(h, p["norm_ffn_g"], p["w_router_hi"], p["w_router_lo"], p["b_router"])
    slots, blk_e, nused, n_slots = _plan(ri, counts, n_tok)
    xs = _sc_dispatch(hn, slots, n_slots)
    ys = _experts(xs, blk_e, nused, p["w_up"], p["b_up"], p["w_down"], p["b_down"])
    yg = _sc_combine(ys, slots)
    out = _final(h, yg, rg, p["norm_final_g"])
    return out.reshape(batch, seq, D_MODEL)


def _prep_params(norm_mix_g, norm_mem_g, w_in, w_pool, pool_scale, w_mem_kv, w_br_pool, w_br_attn, w_br_cross,
                 w_out, norm_ffn_g, w_router, b_router, w_up, b_up, w_down, b_down, norm_final_g):
    wr = w_router[0]
    wr_hi = wr.astype(BF16)
    return dict(
        norm_mix_g=norm_mix_g[0].reshape(1, D_MODEL),
        norm_mem_g=norm_mem_g[0].reshape(1, D_MODEL),
        w_in=w_in[0].astype(BF16),
        w_pool=w_pool[0].astype(BF16),
        pool_scale=pool_scale[0].reshape(1, POOL_WIDTH),
        w_mem_kv=w_mem_kv[0].astype(BF16),
        w_br_pool=w_br_pool[0].astype(BF16),
        w_br_attn=w_br_attn[0].astype(BF16),
        w_br_cross=w_br_cross[0].astype(BF16),
        w_out=w_out[0].astype(BF16),
        norm_ffn_g=norm_ffn_g[0].reshape(1, D_MODEL),
        w_router_hi=jnp.pad(wr_hi, ((0, 0), (0, LANES - N_EXPERTS))),
        w_router_lo=jnp.pad((wr - wr_hi.astype(F32)).astype(BF16), ((0, 0), (0, LANES - N_EXPERTS))),
        b_router=jnp.pad(b_router[0].reshape(1, N_EXPERTS), ((0, 0), (0, LANES - N_EXPERTS))),
        w_up=w_up[0],
        b_up=b_up[0].reshape(N_EXPERTS, 1, 2 * D_FF),
        w_down=w_down[0],
        b_down=b_down[0].reshape(N_EXPERTS, 1, D_MODEL),
        norm_final_g=norm_final_g.reshape(1, D_MODEL),
    )


def kernel(x_prompt, x_sample, mem_prompt, mem_sample, norm_mix_g, norm_mem_g, w_in, w_pool, pool_scale, w_mem_kv,
           w_br_pool, w_br_attn, w_br_cross, w_out, norm_ffn_g, w_router, b_router, w_up, b_up, w_down, b_down,
           norm_final_g):
    p = _prep_params(norm_mix_g, norm_mem_g, w_in, w_pool, pool_scale, w_mem_kv, w_br_pool, w_br_attn, w_br_cross,
                     w_out, norm_ffn_g, w_router, b_router, w_up, b_up, w_down, b_down, norm_final_g)
    y_prompt = _trunk(x_prompt, mem_prompt, p)
    y_sample = _trunk(x_sample, mem_sample, p)
    return (y_prompt, y_sample)
```

```python
import functools
import math

import jax
import jax.numpy as jnp
from jax import lax
from jax.experimental import pallas as pl
from jax.experimental.pallas import tpu as pltpu
from jax.experimental.pallas import tpu_sc as plsc

F32 = jnp.float32
BF16 = jnp.bfloat16
I32 = jnp.int32
U32 = jnp.uint32

D_MODEL = 1024
POOL_GROUPS = 4
POOL_WIDTH = 512
POOL_GROUP_DIM = 128
POOL_WINDOWS = (2, 4, 8, 16)
POOL_HALO = 16
HEAD_DIM = 64
DIL_CONFIGS = ((128, 1), (512, 4), (2048, 16))
HEADS_PER_DIL = 4
ATTN_WIDTH = 768
ATTN_OUT_WIDTH = 256
BAND_BLOCK = 64
ROPE_DIM = 16
ROPE_THETA = 500000.0
CROSS_HEADS = 4
CROSS_HEAD_DIM = 128
CROSS_WIDTH = 512
N_BRANCHES = 3
QKV_WIDTH = 3 * ATTN_WIDTH
IN_WIDTH = POOL_WIDTH + QKV_WIDTH + CROSS_WIDTH + N_BRANCHES * D_MODEL
N_EXPERTS = 32
TOP_K = 4
D_FF = 1024
SWIGLU_LIMIT = 7.0
SWIGLU_ALPHA = 1.702
EPS = 1e-5
NEG_INF = -1e30

LANES = 128
PACKED = D_MODEL // 2
TOKEN_TILE = 512
PROJ_CHUNK = 512
MIX_CHUNK = 256
FF_CHUNK = 512
CAST_CHUNK = 512
EXPERT_BLOCK = 512
SC_WINDOW = 128
VMEM_LIMIT = 52 * 1024 * 1024


def _rms(x, g):
    r = lax.rsqrt(jnp.mean(x * x, axis=-1, keepdims=True) + EPS)
    return x * r * g


def _pack_bf16_pair(x):
    bits = lax.bitcast_convert_type(x.astype(BF16).astype(F32), U32)
    packed = (bits[:, :PACKED] >> 16) | bits[:, PACKED:]
    return lax.bitcast_convert_type(packed, I32)


def _unpack_bf16_pair(w):
    u = lax.bitcast_convert_type(w, U32)
    lo = lax.bitcast_convert_type(u << 16, F32)
    hi = lax.bitcast_convert_type(u & jnp.uint32(0xFFFF0000), F32)
    return lo, hi


def _memkv_kernel(mem_ref, g_ref, w_ref, o_ref):
    xn = _rms(mem_ref[...], g_ref[...]).astype(BF16)
    o_ref[...] = jnp.dot(xn, w_ref[...], preferred_element_type=F32).astype(BF16)


def _memkv(mem2d, g, w_bf):
    rows = mem2d.shape[0]
    tm = 256
    return pl.pallas_call(
        _memkv_kernel,
        grid=(rows // tm,),
        in_specs=[
            pl.BlockSpec((tm, D_MODEL), lambda i: (i, 0)),
            pl.BlockSpec((1, D_MODEL), lambda i: (0, 0)),
            pl.BlockSpec((D_MODEL, 2 * CROSS_WIDTH), lambda i: (0, 0)),
        ],
        out_specs=pl.BlockSpec((tm, 2 * CROSS_WIDTH), lambda i: (i, 0)),
        out_shape=jax.ShapeDtypeStruct((rows, 2 * CROSS_WIDTH), BF16),
        compiler_params=pltpu.CompilerParams(dimension_semantics=("parallel",)),
        name="memkv",
    )(mem2d, g, w_bf)


_O_QKV = POOL_WIDTH
_O_QC = _O_QKV + QKV_WIDTH
_O_GATE = _O_QC + CROSS_WIDTH


def _proj_kernel(x_ref, g_ref, w_ref, tab_ref, u_ref, a0_ref, a1_ref, a2_ref, qc_ref, gate_ref, xcols):
    tm = TOKEN_TILE
    n_cols = D_MODEL // LANES
    g = g_ref[...]
    xn = _rms(x_ref[...], g).astype(BF16)
    for c in range(n_cols):
        xcols[c] = x_ref[:, c * LANES:(c + 1) * LANES]

    def mm(lhs, c0, width=PROJ_CHUNK):
        return jnp.dot(lhs, w_ref[:, c0:c0 + width], preferred_element_type=F32)

    for j in range(POOL_WIDTH // PROJ_CHUNK):
        u_ref[:, j * PROJ_CHUNK:(j + 1) * PROJ_CHUNK] = mm(xn, j * PROJ_CHUNK)

    for grp, a_ref in enumerate((a0_ref, a1_ref, a2_ref)):
        dil = DIL_CONFIGS[grp][1]
        res_rows = tm // dil
        if dil == 1:
            lhs = xn
        else:
            xp = jnp.concatenate(
                [jnp.concatenate([xcols[c, pl.ds(r, res_rows, stride=dil), :] for c in range(n_cols)], axis=1)
                 for r in range(dil)], axis=0)
            lhs = _rms(xp, g).astype(BF16)
        cos, s1, s2 = tab_ref[0, grp], tab_ref[1, grp], tab_ref[2, grp]
        for which in range(3):
            t = mm(lhs, _O_QKV + which * ATTN_WIDTH + grp * ATTN_OUT_WIDTH, ATTN_OUT_WIDTH)
            if which < 2:
                halves = []
                for hh in range(ATTN_OUT_WIDTH // LANES):
                    th = t[:, hh * LANES:(hh + 1) * LANES]
                    th = (th * cos + pltpu.roll(th, LANES - ROPE_DIM // 2, axis=1) * s1
                          + pltpu.roll(th, ROPE_DIM // 2, axis=1) * s2)
                    halves.append(th)
                t = jnp.concatenate(halves, axis=1)
                if which == 0:
                    t = t * (1.0 / math.sqrt(HEAD_DIM))
            tb = t.astype(BF16)
            for r in range(dil):
                c0 = (r * 3 + which) * ATTN_OUT_WIDTH
                a_ref[:, c0:c0 + ATTN_OUT_WIDTH] = tb[r * res_rows:(r + 1) * res_rows, :]

    for j in range(CROSS_WIDTH // PROJ_CHUNK):
        qc_ref[:, j * PROJ_CHUNK:(j + 1) * PROJ_CHUNK] = mm(xn, _O_QC + j * PROJ_CHUNK).astype(BF16)

    for j in range(N_BRANCHES * D_MODEL // PROJ_CHUNK):
        gate_ref[:, j * PROJ_CHUNK:(j + 1) * PROJ_CHUNK] = jax.nn.sigmoid(mm(xn, _O_GATE + j * PROJ_CHUNK)).astype(BF16)


def _proj(x2d, g, w_in_bf, tabs, seq):
    rows = x2d.shape[0]
    tm = TOKEN_TILE
    tiles_per_seq = seq // tm
    n_grp = len(DIL_CONFIGS)
    a_specs = [pl.BlockSpec((tm // d, d * ATTN_WIDTH), lambda i: (i, 0)) for _, d in DIL_CONFIGS]
    a_shapes = [jax.ShapeDtypeStruct((rows // d, d * ATTN_WIDTH), BF16) for _, d in DIL_CONFIGS]
    return pl.pallas_call(
        _proj_kernel,
        grid=(rows // tm,),
        in_specs=[
            pl.BlockSpec((tm, D_MODEL), lambda i: (i, 0)),
            pl.BlockSpec((1, D_MODEL), lambda i: (0, 0)),
            pl.BlockSpec((D_MODEL, IN_WIDTH), lambda i: (0, 0), pipeline_mode=pl.Buffered(1)),
            pl.BlockSpec((3, n_grp, tm, LANES), lambda i: (0, 0, i % tiles_per_seq, 0)),
        ],
        out_specs=[
            pl.BlockSpec((tm, POOL_WIDTH), lambda i: (i, 0)),
            *a_specs,
            pl.BlockSpec((tm, CROSS_WIDTH), lambda i: (i, 0)),
            pl.BlockSpec((tm, N_BRANCHES * D_MODEL), lambda i: (i, 0)),
        ],
        out_shape=[
            jax.ShapeDtypeStruct((rows, POOL_WIDTH), F32),
            *a_shapes,
            jax.ShapeDtypeStruct((rows, CROSS_WIDTH), BF16),
            jax.ShapeDtypeStruct((rows, N_BRANCHES * D_MODEL), BF16),
        ],
        scratch_shapes=[pltpu.VMEM((D_MODEL // LANES, tm, LANES), F32)],
        compiler_params=pltpu.CompilerParams(dimension_semantics=("parallel",), vmem_limit_bytes=VMEM_LIMIT),
        name="proj",
    )(x2d, g, w_in_bf, tabs)


def _rope_tables(seq):
    pos = jnp.arange(seq, dtype=F32)
    inv = jnp.power(jnp.float32(ROPE_THETA), -jnp.arange(0, ROPE_DIM, 2, dtype=F32) / ROPE_DIM)
    ang = pos[:, None] * inv[None, :]
    half = ROPE_DIM // 2
    j = jnp.arange(LANES) % HEAD_DIM
    cos_l = jnp.cos(ang)[:, j % half]
    sin_l = jnp.sin(ang)[:, j % half]
    kinds = jnp.stack([
        jnp.where(j[None, :] < ROPE_DIM, cos_l, 1.0),
        jnp.where(j[None, :] < half, -sin_l, 0.0),
        jnp.where((j[None, :] >= half) & (j[None, :] < ROPE_DIM), sin_l, 0.0),
    ]).astype(F32)
    tm = TOKEN_TILE
    per_group = []
    for _, d in DIL_CONFIGS:
        t = kinds.reshape(3, seq // tm, tm // d, d, LANES).transpose(0, 1, 3, 2, 4).reshape(3, seq, LANES)
        per_group.append(t)
    return jnp.stack(per_group, axis=1)


def _dilattn_kernel(a_ref, o_ref, lse_ref, *, res_len, tq, win, half, n_res):
    t = pl.program_id(2)
    nblk = tq // BAND_BLOCK
    stacked = HEADS_PER_DIL * BAND_BLOCK
    lane_head = lax.broadcasted_iota(I32, (BAND_BLOCK, ATTN_OUT_WIDTH), 1) // HEAD_DIM
    head_sel = [lane_head == h for h in range(HEADS_PER_DIL)]
    head_mask = [jnp.where(sel, 1.0, 0.0).astype(BF16) for sel in head_sel]
    jmi = (lax.broadcasted_iota(I32, (stacked, win), 1)
           - (lax.broadcasted_iota(I32, (stacked, win), 0) & (BAND_BLOCK - 1)))
    ones = jnp.ones((win, LANES), BF16)
    blocks = [(rr, n) for rr in range(n_res) for n in range(nblk)]

    s_parts, v_wins = [], []
    for rr, n in blocks:
        c0 = rr * ATTN_WIDTH
        qpos0 = pl.multiple_of(t * tq + n * BAND_BLOCK, BAND_BLOCK)
        start = pl.multiple_of(jnp.clip(qpos0 - BAND_BLOCK, 0, res_len - win), BAND_BLOCK)
        qb = a_ref[0, pl.ds(qpos0, BAND_BLOCK), c0:c0 + ATTN_OUT_WIDTH]
        kw = a_ref[0, pl.ds(start, win), c0 + ATTN_OUT_WIDTH:c0 + 2 * ATTN_OUT_WIDTH]
        v_wins.append(a_ref[0, pl.ds(start, win), c0 + 2 * ATTN_OUT_WIDTH:c0 + 3 * ATTN_OUT_WIDTH])
        qs = jnp.concatenate([qb * hm for hm in head_mask], axis=0)
        s = lax.dot_general(qs, kw, (((1,), (1,)), ((), ())), preferred_element_type=F32)
        diff = jmi + (start - qpos0)
        s_parts.append(jnp.where((diff >= -half) & (diff <= half), s, NEG_INF))
    s_all = jnp.concatenate(s_parts, axis=0)
    m_all = jnp.max(s_all, axis=-1, keepdims=True)
    p_all = jnp.exp(s_all - m_all).astype(BF16)

    for i, (rr, n) in enumerate(blocks):
        p = p_all[i * stacked:(i + 1) * stacked]
        o_st = jnp.dot(p, v_wins[i], preferred_element_type=F32)
        l_st = jnp.dot(p, ones, preferred_element_type=F32)
        m_st = m_all[i * stacked:(i + 1) * stacked]
        o = jnp.zeros((BAND_BLOCK, ATTN_OUT_WIDTH), F32)
        l = jnp.ones((BAND_BLOCK, ATTN_OUT_WIDTH), F32)
        m = jnp.zeros((BAND_BLOCK, ATTN_OUT_WIDTH), F32)
        for h in range(HEADS_PER_DIL):
            rs = slice(h * BAND_BLOCK, (h + 1) * BAND_BLOCK)
            o = jnp.where(head_sel[h], o_st[rs], o)
            l = jnp.where(head_sel[h], jnp.concatenate([l_st[rs], l_st[rs]], axis=1), l)
            m = jnp.where(head_sel[h], m_st[rs], m)
        rows = slice(n * BAND_BLOCK, (n + 1) * BAND_BLOCK)
        cols = slice(rr * ATTN_OUT_WIDTH, (rr + 1) * ATTN_OUT_WIDTH)
        o_ref[0, rows, cols] = (o / l).astype(BF16)
        lse_ref[0, rows, cols] = m + jnp.log(l)


ATTN_BLOCKS_PER_STEP = 16


def _dilattn(a_grp, batch, seq, group):
    window, dil = DIL_CONFIGS[group]
    half = window // (2 * dil)
    res_len = seq // dil
    win = min(3 * BAND_BLOCK, res_len)
    n_res = min(dil, 4)
    tq = min(res_len, ATTN_BLOCKS_PER_STEP // n_res * BAND_BLOCK)
    qkv3 = a_grp.reshape(batch, res_len, dil * ATTN_WIDTH)
    kern = functools.partial(_dilattn_kernel, res_len=res_len, tq=tq, win=win, half=half, n_res=n_res)
    out_spec = pl.BlockSpec((1, tq, n_res * ATTN_OUT_WIDTH), lambda b, r, t: (b, t, r))
    o, lse = pl.pallas_call(
        kern,
        grid=(batch, dil // n_res, res_len // tq),
        in_specs=[pl.BlockSpec((1, res_len, n_res * ATTN_WIDTH), lambda b, r, t: (b, 0, r))],
        out_specs=[out_spec, out_spec],
        out_shape=[
            jax.ShapeDtypeStruct((batch, res_len, dil * ATTN_OUT_WIDTH), BF16),
            jax.ShapeDtypeStruct((batch, res_len, dil * ATTN_OUT_WIDTH), F32),
        ],
        compiler_params=pltpu.CompilerParams(dimension_semantics=("parallel", "parallel", "parallel"),
                                             vmem_limit_bytes=VMEM_LIMIT),
        name=f"dilattn{group}",
    )(qkv3)
    return (o.reshape(batch * res_len, dil * ATTN_OUT_WIDTH), lse.reshape(batch * res_len, dil * ATTN_OUT_WIDTH))


def _mix_kernel(x_ref, u_ref, up_ref, un_ref, qc_ref, gate_ref, o0_ref, o1_ref, o2_ref, l0_ref, l1_ref, l2_ref,
                kv_ref, wpool_ref, pscale_ref, wbp_ref, wba_ref, wbc_ref, wout_ref, h_ref, ubuf, merged, relay, *, seq):
    tm = TOKEN_TILE
    i = pl.program_id(1)
    nt = pl.num_programs(1)

    ubuf[0:POOL_HALO, :] = jnp.where(i > 0, up_ref[...], 0.0)
    ubuf[POOL_HALO:POOL_HALO + tm, :] = u_ref[...]
    ubuf[POOL_HALO + tm:, :] = jnp.where(i < nt - 1, un_ref[...], 0.0)
    pos = i * tm + lax.broadcasted_iota(I32, (tm, POOL_GROUP_DIM), 0)
    pool_parts = []
    for g, w in enumerate(POOL_WINDOWS):
        cs = slice(g * POOL_GROUP_DIM, (g + 1) * POOL_GROUP_DIM)
        acc = ubuf[POOL_HALO - w // 2:POOL_HALO - w // 2 + tm, cs]
        for j in range(1, w):
            off = POOL_HALO - w // 2 + j
            acc = acc + ubuf[off:off + tm, cs]
        cnt = (jnp.minimum(pos + w // 2, seq) - jnp.maximum(pos - w // 2, 0)).astype(F32)
        z = acc / cnt - ubuf[POOL_HALO:POOL_HALO + tm, cs]
        zp = jnp.dot(z.astype(BF16), wpool_ref[g], preferred_element_type=F32)
        pool_parts.append(zp * pscale_ref[:, cs])
    pool_bf = jnp.concatenate(pool_parts, axis=1).astype(BF16)

    cross_parts = []
    for h in range(CROSS_HEADS):
        cs = slice(h * CROSS_HEAD_DIM, (h + 1) * CROSS_HEAD_DIM)
        kh = kv_ref[0, :, cs]
        vh = kv_ref[0, :, CROSS_WIDTH + h * CROSS_HEAD_DIM:CROSS_WIDTH + (h + 1) * CROSS_HEAD_DIM]
        s = lax.dot_general(qc_ref[:, cs], kh, (((1,), (1,)), ((), ())), preferred_element_type=F32)
        s = s * (1.0 / math.sqrt(CROSS_HEAD_DIM))
        m = jnp.max(s, axis=-1, keepdims=True)
        p = jnp.exp(s - m)
        l = jnp.sum(p, axis=-1, keepdims=True)
        cross_parts.append(jnp.dot(p.astype(BF16), vh, preferred_element_type=F32) / l)
    cross_bf = jnp.concatenate(cross_parts, axis=1).astype(BF16)

    def token_major(ref, slot, dil):
        if dil == 1:
            return ref[...].astype(F32)
        res_rows = tm // dil
        for r in range(dil):
            for hh in range(ATTN_OUT_WIDTH // LANES):
                c0 = r * ATTN_OUT_WIDTH + hh * LANES
                relay[slot, hh, pl.ds(r, res_rows, stride=dil), :] = ref[:, c0:c0 + LANES].astype(F32)
        return jnp.concatenate([relay[slot, hh] for hh in range(ATTN_OUT_WIDTH // LANES)], axis=1)

    dils = [d for _, d in DIL_CONFIGS]
    o0, o1, o2 = [token_major(r, s, d) for r, s, d in zip((o0_ref, o1_ref, o2_ref), (0, 1, 2), dils)]
    l0, l1, l2 = [token_major(r, s, d) for r, s, d in zip((l0_ref, l1_ref, l2_ref), (3, 4, 5), dils)]
    mx = jnp.maximum(jnp.maximum(l0, l1), l2)
    e0, e1, e2 = jnp.exp(l0 - mx), jnp.exp(l1 - mx), jnp.exp(l2 - mx)
    attn = (e0 * o0 + e1 * o1 + e2 * o2) / (e0 + e1 + e2)
    attn_bf = attn.astype(BF16)

    for c in range(D_MODEL // MIX_CHUNK):
        cs = slice(c * MIX_CHUNK, (c + 1) * MIX_CHUNK)
        mrg = gate_ref[:, cs].astype(F32) * jnp.dot(pool_bf, wbp_ref[:, cs], preferred_element_type=F32)
        mrg = mrg + gate_ref[:, D_MODEL + c * MIX_CHUNK:D_MODEL + (c + 1) * MIX_CHUNK].astype(F32) * jnp.dot(
            attn_bf, wba_ref[:, cs], preferred_element_type=F32)
        mrg = mrg + gate_ref[:, 2 * D_MODEL + c * MIX_CHUNK:2 * D_MODEL + (c + 1) * MIX_CHUNK].astype(F32) * jnp.dot(
            cross_bf, wbc_ref[:, cs], preferred_element_type=F32)
        merged[:, cs] = mrg.astype(BF16)
    h_ref[...] = x_ref[...] + jnp.dot(merged[...], wout_ref[...], preferred_element_type=F32)


def _mix(x2d, u, qc, gate, outs, lses, kv3, wpool_bf, pscale, wbp, wba, wbc, wout, batch, seq):
    tm = TOKEN_TILE
    ts = seq // tm
    rows = batch * seq
    hb = tm // POOL_HALO
    n_halo = rows // POOL_HALO

    def row(b, i):
        return (b * ts + i, 0)

    def const(b, i):
        return (0, 0)

    tok = lambda w: pl.BlockSpec((tm, w), row)
    in_specs = [
        tok(D_MODEL),
        tok(POOL_WIDTH),
        pl.BlockSpec((POOL_HALO, POOL_WIDTH), lambda b, i: (jnp.maximum((b * ts + i) * hb - 1, 0), 0)),
        pl.BlockSpec((POOL_HALO, POOL_WIDTH), lambda b, i: (jnp.minimum((b * ts + i + 1) * hb, n_halo - 1), 0)),
        tok(CROSS_WIDTH),
        tok(N_BRANCHES * D_MODEL),
        *[pl.BlockSpec((tm // d, d * ATTN_OUT_WIDTH), row) for _, d in DIL_CONFIGS],
        *[pl.BlockSpec((tm // d, d * ATTN_OUT_WIDTH), row) for _, d in DIL_CONFIGS],
        pl.BlockSpec((1, kv3.shape[1], 2 * CROSS_WIDTH), lambda b, i: (b, 0, 0)),
        pl.BlockSpec((POOL_GROUPS, POOL_GROUP_DIM, POOL_GROUP_DIM), lambda b, i: (0, 0, 0)),
        pl.BlockSpec((1, POOL_WIDTH), const),
        pl.BlockSpec((POOL_WIDTH, D_MODEL), const),
        pl.BlockSpec((ATTN_OUT_WIDTH, D_MODEL), const),
        pl.BlockSpec((CROSS_WIDTH, D_MODEL), const),
        pl.BlockSpec((D_MODEL, D_MODEL), const),
    ]
    return pl.pallas_call(
        functools.partial(_mix_kernel, seq=seq),
        grid=(batch, ts),
        in_specs=in_specs,
        out_specs=pl.BlockSpec((tm, D_MODEL), row),
        out_shape=jax.ShapeDtypeStruct((rows, D_MODEL), F32),
        scratch_shapes=[pltpu.VMEM((tm + 2 * POOL_HALO, POOL_WIDTH), F32), pltpu.VMEM((tm, D_MODEL), BF16),
                        pltpu.VMEM((2 * len(DIL_CONFIGS), ATTN_OUT_WIDTH // LANES, tm, LANES), F32)],
        compiler_params=pltpu.CompilerParams(dimension_semantics=("parallel", "parallel"), vmem_limit_bytes=VMEM_LIMIT),
        name="mix",
    )(x2d, u, u, u, qc, gate, *outs, *lses, kv3, wpool_bf, pscale, wbp, wba, wbc, wout)


def _route_kernel(h_ref, g_ref, whi_ref, wlo_ref, b_ref, hn_ref, ri_ref, rg_ref, cnt_ref, carry, tri):
    tm = TOKEN_TILE
    step = pl.program_id(0)

    @pl.when(step == 0)
    def _():
        carry[...] = jnp.zeros_like(carry)
        r = lax.broadcasted_iota(I32, (tm, tm), 0)
        c = lax.broadcasted_iota(I32, (tm, tm), 1)
        tri[...] = jnp.where(r < c, 1.0, 0.0).astype(BF16)

    hn = _rms(h_ref[...], g_ref[...])
    hn_ref[...] = _pack_bf16_pair(hn)
    hi = hn.astype(BF16)
    lo = (hn - hi.astype(F32)).astype(BF16)
    logits = (jnp.dot(hi, whi_ref[...], preferred_element_type=F32)
              + jnp.dot(lo, whi_ref[...], preferred_element_type=F32)
              + jnp.dot(hi, wlo_ref[...], preferred_element_type=F32)) + b_ref[...]
    work = logits.T[:N_EXPERTS]
    row = lax.broadcasted_iota(I32, (N_EXPERTS, tm), 0).astype(F32)
    idxs, vals = [], []
    onehot = jnp.zeros((N_EXPERTS, tm), F32)
    for _ in range(TOP_K):
        m = jnp.max(work, axis=0, keepdims=True)
        idx = jnp.min(jnp.where(work == m, row, float(N_EXPERTS)), axis=0, keepdims=True)
        sel = row == idx
        onehot = jnp.where(sel, 1.0, onehot)
        work = jnp.where(sel, -jnp.inf, work)
        idxs.append(idx)
        vals.append(m)
    exps = [jnp.exp(v - vals[0]) for v in vals]
    den = exps[0] + exps[1] + exps[2] + exps[3]
    gates = [e / den for e in exps]

    prefix = jnp.dot(onehot.astype(BF16), tri[...], preferred_element_type=F32) + carry[:, 0:1]
    ranks = [jnp.sum(jnp.where(row == idx, prefix, 0.0), axis=0, keepdims=True) for idx in idxs]
    carry[...] = carry[...] + jnp.sum(onehot, axis=1, keepdims=True)

    row8 = lax.broadcasted_iota(I32, (2 * TOP_K, tm), 0)
    ri = jnp.zeros((2 * TOP_K, tm), F32)
    row128 = lax.broadcasted_iota(I32, (LANES, tm), 0)
    rg = jnp.zeros((LANES, tm), F32)
    for k in range(TOP_K):
        ri = jnp.where(row8 == k, idxs[k], ri)
        ri = jnp.where(row8 == TOP_K + k, ranks[k], ri)
        rg = jnp.where(row128 == k, gates[k], rg)
    ri_ref[...] = ri.astype(I32)
    rg_ref[...] = rg.T
    cnt_ref[...] = carry[...].astype(I32)


def _route(h2d, g, whi, wlo, b):
    rows = h2d.shape[0]
    tm = TOKEN_TILE
    const = lambda i: (0, 0)
    return pl.pallas_call(
        _route_kernel,
        grid=(rows // tm,),
        in_specs=[
            pl.BlockSpec((tm, D_MODEL), lambda i: (i, 0)),
            pl.BlockSpec((1, D_MODEL), const),
            pl.BlockSpec((D_MODEL, LANES), const),
            pl.BlockSpec((D_MODEL, LANES), const),
            pl.BlockSpec((1, LANES), const),
        ],
        out_specs=[
            pl.BlockSpec((tm, PACKED), lambda i: (i, 0)),
            pl.BlockSpec((2 * TOP_K, tm), lambda i: (0, i)),
            pl.BlockSpec((tm, LANES), lambda i: (i, 0)),
            pl.BlockSpec((N_EXPERTS, LANES), const),
        ],
        out_shape=[
            jax.ShapeDtypeStruct((rows, PACKED), I32),
            jax.ShapeDtypeStruct((2 * TOP_K, rows), I32),
            jax.ShapeDtypeStruct((rows, LANES), F32),
            jax.ShapeDtypeStruct((N_EXPERTS, LANES), I32),
        ],
        scratch_shapes=[pltpu.VMEM((N_EXPERTS, LANES), F32), pltpu.VMEM((tm, tm), BF16)],
        compiler_params=pltpu.CompilerParams(dimension_semantics=("arbitrary",), vmem_limit_bytes=VMEM_LIMIT),
        name="route",
    )(h2d, g, whi, wlo, b)


def _sc_mesh():
    return plsc.VectorSubcoreMesh(core_axis_name="c", subcore_axis_name="s")


def _sc_workers():
    info = plsc.get_sparse_core_info()
    return info.num_cores, info.num_cores * info.num_subcores


def _sc_dispatch(rows_packed, slots, n_slots):
    n_tok = rows_packed.shape[0]
    n_cores, n_workers = _sc_workers()
    chunks_per_worker = n_tok // SC_WINDOW // n_workers

    @functools.partial(pl.kernel, out_type=jax.ShapeDtypeStruct((n_slots, PACKED), I32), mesh=_sc_mesh(),
                       scratch_types=[pltpu.VMEM((TOP_K, SC_WINDOW), I32), pltpu.VMEM((SC_WINDOW, PACKED), I32)],
                       name="dispatch")
    def kern(x_hbm, i_hbm, o_hbm, idx_v, rows_v):
        wid = lax.axis_index("s") * n_cores + lax.axis_index("c")

        @pl.loop(0, chunks_per_worker)
        def _(j):
            chunk = wid * chunks_per_worker + j
            pltpu.sync_copy(i_hbm.at[chunk], idx_v)
            pltpu.sync_copy(x_hbm.at[pl.ds(chunk * SC_WINDOW, SC_WINDOW)], rows_v)
            for k in range(TOP_K):
                pltpu.sync_copy(rows_v, o_hbm.at[idx_v.at[k]])

    return kern(rows_packed, slots)


def _sc_combine(ys, slots):
    n_tok = slots.shape[0] * SC_WINDOW
    n_cores, n_workers = _sc_workers()
    chunks_per_worker = n_tok // SC_WINDOW // n_workers

    @functools.partial(pl.kernel, out_type=jax.ShapeDtypeStruct((TOP_K, n_tok, PACKED), I32), mesh=_sc_mesh(),
                       scratch_types=[pltpu.VMEM((TOP_K, SC_WINDOW), I32), pltpu.VMEM((SC_WINDOW, PACKED), I32)],
                       name="combine")
    def kern(y_hbm, i_hbm, o_hbm, idx_v, rows_v):
        wid = lax.axis_index("s") * n_cores + lax.axis_index("c")

        @pl.loop(0, chunks_per_worker)
        def _(j):
            chunk = wid * chunks_per_worker + j
            pltpu.sync_copy(i_hbm.at[chunk], idx_v)
            for k in range(TOP_K):
                pltpu.sync_copy(y_hbm.at[idx_v.at[k]], rows_v)
                pltpu.sync_copy(rows_v, o_hbm.at[k, pl.ds(chunk * SC_WINDOW, SC_WINDOW)])

    return kern(ys, slots)


def _expert_kernel(blk_e_ref, nused_ref, x_ref, wu_ref, bu_ref, wd_ref, bd_ref, o_ref, wu_bf, wd_bf):
    i = pl.program_id(0)
    used = i < nused_ref[0]

    @pl.when(used & ((i == 0) | (blk_e_ref[i] != blk_e_ref[jnp.maximum(i - 1, 0)])))
    def _():
        for c in range(2 * D_FF // CAST_CHUNK):
            cs = slice(c * CAST_CHUNK, (c + 1) * CAST_CHUNK)
            wu_bf[:, cs] = wu_ref[0, :, cs].astype(BF16)
        for c in range(D_MODEL // CAST_CHUNK):
            cs = slice(c * CAST_CHUNK, (c + 1) * CAST_CHUNK)
            wd_bf[:, cs] = wd_ref[0, :, cs].astype(BF16)

    @pl.when(used)
    def _():
        lo, hi = _unpack_bf16_pair(x_ref[...])
        x = jnp.concatenate([lo.astype(BF16), hi.astype(BF16)], axis=1)
        y = jnp.zeros((EXPERT_BLOCK, D_MODEL), F32)
        for c in range(D_FF // FF_CHUNK):
            gs = slice(c * FF_CHUNK, (c + 1) * FF_CHUNK)
            us = slice(D_FF + c * FF_CHUNK, D_FF + (c + 1) * FF_CHUNK)
            gate = jnp.dot(x, wu_bf[:, gs], preferred_element_type=F32) + bu_ref[0, :, gs]
            up = jnp.dot(x, wu_bf[:, us], preferred_element_type=F32) + bu_ref[0, :, us]
            gate = jnp.minimum(gate, SWIGLU_LIMIT)
            up = jnp.clip(up, -SWIGLU_LIMIT, SWIGLU_LIMIT)
            act = (up + 1.0) * (gate * jax.nn.sigmoid(SWIGLU_ALPHA * gate))
            y = y + jnp.dot(act.astype(BF16), wd_bf[gs, :], preferred_element_type=F32)
        o_ref[...] = _pack_bf16_pair(y + bd_ref[0])

    @pl.when(jnp.logical_not(used))
    def _():
        o_ref[...] = jnp.zeros_like(o_ref)


def _experts(xs, blk_e, nused, wu, bu, wd, bd):
    n_slots = xs.shape[0]
    tb = EXPERT_BLOCK
    grid_spec = pltpu.PrefetchScalarGridSpec(
        num_scalar_prefetch=2,
        grid=(n_slots // tb,),
        in_specs=[
            pl.BlockSpec((tb, PACKED), lambda i, be, nu: (i, 0)),
            pl.BlockSpec((1, D_MODEL, 2 * D_FF), lambda i, be, nu: (be[i], 0, 0)),
            pl.BlockSpec((1, 1, 2 * D_FF), lambda i, be, nu: (be[i], 0, 0)),
            pl.BlockSpec((1, D_FF, D_MODEL), lambda i, be, nu: (be[i], 0, 0)),
            pl.BlockSpec((1, 1, D_MODEL), lambda i, be, nu: (be[i], 0, 0)),
        ],
        out_specs=pl.BlockSpec((tb, PACKED), lambda i, be, nu: (i, 0)),
        scratch_shapes=[pltpu.VMEM((D_MODEL, 2 * D_FF), BF16), pltpu.VMEM((D_FF, D_MODEL), BF16)],
    )
    return pl.pallas_call(
        _expert_kernel,
        grid_spec=grid_spec,
        out_shape=jax.ShapeDtypeStruct((n_slots, PACKED), I32),
        compiler_params=pltpu.CompilerParams(dimension_semantics=("arbitrary",), vmem_limit_bytes=VMEM_LIMIT),
        name="experts",
    )(blk_e, nused, xs, wu, bu, wd, bd)


def _final_kernel(h_ref, yg_ref, rg_ref, g_ref, o_ref):
    rg = rg_ref[...]
    lo = jnp.zeros((TOKEN_TILE, PACKED), F32)
    hi = jnp.zeros((TOKEN_TILE, PACKED), F32)
    for k in range(TOP_K):
        yl, yh = _unpack_bf16_pair(yg_ref[k])
        w = rg[:, k:k + 1]
        lo = lo + w * yl
        hi = hi + w * yh
    x = h_ref[...] + jnp.concatenate([lo, hi], axis=1)
    o_ref[...] = _rms(x, g_ref[...])


def _final(h2d, yg, rg, g):
    rows = h2d.shape[0]
    tm = TOKEN_TILE
    return pl.pallas_call(
        _final_kernel,
        grid=(rows // tm,),
        in_specs=[
            pl.BlockSpec((tm, D_MODEL), lambda i: (i, 0)),
            pl.BlockSpec((TOP_K, tm, PACKED), lambda i: (0, i, 0)),
            pl.BlockSpec((tm, LANES), lambda i: (i, 0)),
            pl.BlockSpec((1, D_MODEL), lambda i: (0, 0)),
        ],
        out_specs=pl.BlockSpec((tm, D_MODEL), lambda i: (i, 0)),
        out_shape=jax.ShapeDtypeStruct((rows, D_MODEL), F32),
        compiler_params=pltpu.CompilerParams(dimension_semantics=("parallel",), vmem_limit_bytes=VMEM_LIMIT),
        name="final",
    )(h2d, yg, rg, g)


def _plan(ri, counts, n_tok):
    tb = EXPERT_BLOCK
    n_slots = n_tok * TOP_K + N_EXPERTS * tb
    eid = ri[:TOP_K]
    rank = ri[TOP_K:]
    c = counts[:, 0]
    pc = (c + tb - 1) // tb * tb
    pend = jnp.cumsum(pc)
    pstart = pend - pc
    base = jnp.zeros_like(eid)
    for e in range(N_EXPERTS):
        base = jnp.where(eid == e, pstart[e], base)
    slots = (base + rank).astype(I32)
    slots = slots.reshape(TOP_K, n_tok // SC_WINDOW, SC_WINDOW).transpose(1, 0, 2)
    blk_start = jnp.arange(n_slots // tb, dtype=I32) * tb
    blk_e = jnp.minimum(jnp.sum(blk_start[:, None] >= pend[None, :], axis=1), N_EXPERTS - 1).astype(I32)
    nused = (pend[-1:] // tb).astype(I32)
    return slots, blk_e, nused, n_slots


def _trunk(x, mem, p):
    batch, seq, _ = x.shape
    n_tok = batch * seq
    x2d = x.reshape(n_tok, D_MODEL)
    kv = _memkv(mem.reshape(-1, D_MODEL), p["norm_mem_g"], p["w_mem_kv"])
    kv3 = kv.reshape(batch, mem.shape[1], 2 * CROSS_WIDTH)
    u, a0, a1, a2, qc, gate = _proj(x2d, p["norm_mix_g"], p["w_in"], _rope_tables(seq), seq)
    outs, lses = [], []
    for g, a_grp in enumerate((a0, a1, a2)):
        o, lse = _dilattn(a_grp, batch, seq, g)
        outs.append(o)
        lses.append(lse)
    h = _mix(x2d, u, qc, gate, outs, lses, kv3, p["w_pool"], p["pool_scale"], p["w_br_pool"], p["w_br_attn"],
             p["w_br_cross"], p["w_out"], batch, seq)
    hn, ri, rg, counts = _route(h, p["norm_ffn_g"], p["w_router_hi"], p["w_router_lo"], p["b_router"])
    slots, blk_e, nused, n_slots = _plan(ri, counts, n_tok)
    xs = _sc_dispatch(hn, slots, n_slots)
    ys = _experts(xs, blk_e, nused, p["w_up"], p["b_up"], p["w_down"], p["b_down"])
    yg = _sc_combine(ys, slots)
    out = _final(h, yg, rg, p["norm_final_g"])
    return out.reshape(batch, seq, D_MODEL)


def _prep_params(norm_mix_g, norm_mem_g, w_in, w_pool, pool_scale, w_mem_kv, w_br_pool, w_br_attn, w_br_cross,
                 w_out, norm_ffn_g, w_router, b_router, w_up, b_up, w_down, b_down, norm_final_g):
    wr = w_router[0]
    wr_hi = wr.astype(BF16)
    return dict(
        norm_mix_g=norm_mix_g[0].reshape(1, D_MODEL),
        norm_mem_g=norm_mem_g[0].reshape(1, D_MODEL),
        w_in=w_in[0].astype(BF16),
        w_pool=w_pool[0].astype(BF16),
        pool_scale=pool_scale[0].reshape(1, POOL_WIDTH),
        w_mem_kv=w_mem_kv[0].astype(BF16),
        w_br_pool=w_br_pool[0].astype(BF16),
        w_br_attn=w_br_attn[0].astype(BF16),
        w_br_cross=w_br_cross[0].astype(BF16),
        w_out=w_out[0].astype(BF16),
        norm_ffn_g=norm_ffn_g[0].reshape(1, D_MODEL),
        w_router_hi=jnp.pad(wr_hi, ((0, 0), (0, LANES - N_EXPERTS))),
        w_router_lo=jnp.pad((wr - wr_hi.astype(F32)).astype(BF16), ((0, 0), (0, LANES - N_EXPERTS))),
        b_router=jnp.pad(b_router[0].reshape(1, N_EXPERTS), ((0, 0), (0, LANES - N_EXPERTS))),
        w_up=w_up[0],
        b_up=b_up[0].reshape(N_EXPERTS, 1, 2 * D_FF),
        w_down=w_down[0],
        b_down=b_down[0].reshape(N_EXPERTS, 1, D_MODEL),
        norm_final_g=norm_final_g.reshape(1, D_MODEL),
    )


def kernel(x_prompt, x_sample, mem_prompt, mem_sample, norm_mix_g, norm_mem_g, w_in, w_pool, pool_scale, w_mem_kv,
           w_br_pool, w_br_attn, w_br_cross, w_out, norm_ffn_g, w_router, b_router, w_up, b_up, w_down, b_down,
           norm_final_g):
    p = _prep_params(norm_mix_g, norm_mem_g, w_in, w_pool, pool_scale, w_mem_kv, w_br_pool, w_br_attn, w_br_cross,
                     w_out, norm_ffn_g, w_router, b_router, w_up, b_up, w_down, b_down, norm_final_g)
    y_prompt = _trunk(x_prompt, mem_prompt, p)
    y_sample = _trunk(x_sample, mem_sample, p)
    return (y_prompt, y_sample)
```

```python
import functools
import math

import jax
import jax.numpy as jnp
from jax import lax
from jax.experimental import pallas as pl
from jax.experimental.pallas import tpu as pltpu
from jax.experimental.pallas import tpu_sc as plsc

F32 = jnp.float32
BF16 = jnp.bfloat16
I32 = jnp.int32
U32 = jnp.uint32

D_MODEL = 1024
POOL_GROUPS = 4
POOL_WIDTH = 512
POOL_GROUP_DIM = 128
POOL_WINDOWS = (2, 4, 8, 16)
POOL_HALO = 16
HEAD_DIM = 64
DIL_CONFIGS = ((128, 1), (512, 4), (2048, 16))
HEADS_PER_DIL = 4
ATTN_WIDTH = 768
ATTN_OUT_WIDTH = 256
BAND_BLOCK = 64
ROPE_DIM = 16
ROPE_THETA = 500000.0
CROSS_HEADS = 4
CROSS_HEAD_DIM = 128
CROSS_WIDTH = 512
N_BRANCHES = 3
QKV_WIDTH = 3 * ATTN_WIDTH
IN_WIDTH = POOL_WIDTH + QKV_WIDTH + CROSS_WIDTH + N_BRANCHES * D_MODEL
N_EXPERTS = 32
TOP_K = 4
D_FF = 1024
SWIGLU_LIMIT = 7.0
SWIGLU_ALPHA = 1.702
EPS = 1e-5
NEG_INF = -1e30

LANES = 128
PACKED = D_MODEL // 2
TOKEN_TILE = 512
PROJ_CHUNK = 512
MIX_CHUNK = 256
FF_CHUNK = 512
CAST_CHUNK = 512
EXPERT_BLOCK = 512
SC_WINDOW = 128
VMEM_LIMIT = 52 * 1024 * 1024


def _rms(x, g):
    r = lax.rsqrt(jnp.mean(x * x, axis=-1, keepdims=True) + EPS)
    return x * r * g


def _pack_bf16_pair(x):
    bits = lax.bitcast_convert_type(x.astype(BF16).astype(F32), U32)
    packed = (bits[:, :PACKED] >> 16) | bits[:, PACKED:]
    return lax.bitcast_convert_type(packed, I32)


def _unpack_bf16_pair(w):
    u = lax.bitcast_convert_type(w, U32)
    lo = lax.bitcast_convert_type(u << 16, F32)
    hi = lax.bitcast_convert_type(u & jnp.uint32(0xFFFF0000), F32)
    return lo, hi


def _memkv_kernel(mem_ref, g_ref, w_ref, o_ref):
    xn = _rms(mem_ref[...], g_ref[...]).astype(BF16)
    o_ref[...] = jnp.dot(xn, w_ref[...], preferred_element_type=F32).astype(BF16)


def _memkv(mem2d, g, w_bf):
    rows = mem2d.shape[0]
    tm = 256
    return pl.pallas_call(
        _memkv_kernel,
        grid=(rows // tm,),
        in_specs=[
            pl.BlockSpec((tm, D_MODEL), lambda i: (i, 0)),
            pl.BlockSpec((1, D_MODEL), lambda i: (0, 0)),
            pl.BlockSpec((D_MODEL, 2 * CROSS_WIDTH), lambda i: (0, 0)),
        ],
        out_specs=pl.BlockSpec((tm, 2 * CROSS_WIDTH), lambda i: (i, 0)),
        out_shape=jax.ShapeDtypeStruct((rows, 2 * CROSS_WIDTH), BF16),
        compiler_params=pltpu.CompilerParams(dimension_semantics=("parallel",)),
        name="memkv",
    )(mem2d, g, w_bf)


_O_QKV = POOL_WIDTH
_O_QC = _O_QKV + QKV_WIDTH
_O_GATE = _O_QC + CROSS_WIDTH


def _proj_kernel(x_ref, g_ref, w_ref, tab_ref, u_ref, a0_ref, a1_ref, a2_ref, qc_ref, gate_ref, xcols):
    tm = TOKEN_TILE
    n_cols = D_MODEL // LANES
    g = g_ref[...]
    xn = _rms(x_ref[...], g).astype(BF16)
    for c in range(n_cols):
        xcols[c] = x_ref[:, c * LANES:(c + 1) * LANES]

    def mm(lhs, c0, width=PROJ_CHUNK):
        return jnp.dot(lhs, w_ref[:, c0:c0 + width], preferred_element_type=F32)

    for j in range(POOL_WIDTH // PROJ_CHUNK):
        u_ref[:, j * PROJ_CHUNK:(j + 1) * PROJ_CHUNK] = mm(xn, j * PROJ_CHUNK)

    for grp, a_ref in enumerate((a0_ref, a1_ref, a2_ref)):
        dil = DIL_CONFIGS[grp][1]
        res_rows = tm // dil
        if dil == 1:
            lhs = xn
        else:
            xp = jnp.concatenate(
                [jnp.concatenate([xcols[c, pl.ds(r, res_rows, stride=dil), :] for c in range(n_cols)], axis=1)
                 for r in range(dil)], axis=0)
            lhs = _rms(xp, g).astype(BF16)
        cos, s1, s2 = tab_ref[0, grp], tab_ref[1, grp], tab_ref[2, grp]
        for which in range(3):
            t = mm(lhs, _O_QKV + which * ATTN_WIDTH + grp * ATTN_OUT_WIDTH, ATTN_OUT_WIDTH)
            if which < 2:
                halves = []
                for hh in range(ATTN_OUT_WIDTH // LANES):
                    th = t[:, hh * LANES:(hh + 1) * LANES]
                    th = (th * cos + pltpu.roll(th, LANES - ROPE_DIM // 2, axis=1) * s1
                          + pltpu.roll(th, ROPE_DIM // 2, axis=1) * s2)
                    halves.append(th)
                t = jnp.concatenate(halves, axis=1)
                if which == 0:
                    t = t * (1.0 / math.sqrt(HEAD_DIM))
            tb = t.astype(BF16)
            for r in range(dil):
                c0 = (r * 3 + which) * ATTN_OUT_WIDTH
                a_ref[:, c0:c0 + ATTN_OUT_WIDTH] = tb[r * res_rows:(r + 1) * res_rows, :]

    for j in range(CROSS_WIDTH // PROJ_CHUNK):
        qc_ref[:, j * PROJ_CHUNK:(j + 1) * PROJ_CHUNK] = mm(xn, _O_QC + j * PROJ_CHUNK).astype(BF16)

    for j in range(N_BRANCHES * D_MODEL // PROJ_CHUNK):
        gate_ref[:, j * PROJ_CHUNK:(j + 1) * PROJ_CHUNK] = jax.nn.sigmoid(mm(xn, _O_GATE + j * PROJ_CHUNK)).astype(BF16)


def _proj(x2d, g, w_in_bf, tabs, seq):
    rows = x2d.shape[0]
    tm = TOKEN_TILE
    tiles_per_seq = seq // tm
    n_grp = len(DIL_CONFIGS)
    a_specs = [pl.BlockSpec((tm // d, d * ATTN_WIDTH), lambda i: (i, 0)) for _, d in DIL_CONFIGS]
    a_shapes = [jax.ShapeDtypeStruct((rows // d, d * ATTN_WIDTH), BF16) for _, d in DIL_CONFIGS]
    return pl.pallas_call(
        _proj_kernel,
        grid=(rows // tm,),
        in_specs=[
            pl.BlockSpec((tm, D_MODEL), lambda i: (i, 0)),
            pl.BlockSpec((1, D_MODEL), lambda i: (0, 0)),
            pl.BlockSpec((D_MODEL, IN_WIDTH), lambda i: (0, 0), pipeline_mode=pl.Buffered(1)),
            pl.BlockSpec((3, n_grp, tm, LANES), lambda i: (0, 0, i % tiles_per_seq, 0)),
        ],
        out_specs=[
            pl.BlockSpec((tm, POOL_WIDTH), lambda i: (i, 0)),
            *a_specs,
            pl.BlockSpec((tm, CROSS_WIDTH), lambda i: (i, 0)),
            pl.BlockSpec((tm, N_BRANCHES * D_MODEL), lambda i: (i, 0)),
        ],
        out_shape=[
            jax.ShapeDtypeStruct((rows, POOL_WIDTH), F32),
            *a_shapes,
            jax.ShapeDtypeStruct((rows, CROSS_WIDTH), BF16),
            jax.ShapeDtypeStruct((rows, N_BRANCHES * D_MODEL), BF16),
        ],
        scratch_shapes=[pltpu.VMEM((D_MODEL // LANES, tm, LANES), F32)],
        compiler_params=pltpu.CompilerParams(dimension_semantics=("parallel",), vmem_limit_bytes=VMEM_LIMIT),
        name="proj",
    )(x2d, g, w_in_bf, tabs)


def _rope_tables(seq):
    pos = jnp.arange(seq, dtype=F32)
    inv = jnp.power(jnp.float32(ROPE_THETA), -jnp.arange(0, ROPE_DIM, 2, dtype=F32) / ROPE_DIM)
    ang = pos[:, None] * inv[None, :]
    half = ROPE_DIM // 2
    j = jnp.arange(LANES) % HEAD_DIM
    cos_l = jnp.cos(ang)[:, j % half]
    sin_l = jnp.sin(ang)[:, j % half]
    kinds = jnp.stack([
        jnp.where(j[None, :] < ROPE_DIM, cos_l, 1.0),
        jnp.where(j[None, :] < half, -sin_l, 0.0),
        jnp.where((j[None, :] >= half) & (j[None, :] < ROPE_DIM), sin_l, 0.0),
    ]).astype(F32)
    tm = TOKEN_TILE
    per_group = []
    for _, d in DIL_CONFIGS:
        t = kinds.reshape(3, seq // tm, tm // d, d, LANES).transpose(0, 1, 3, 2, 4).reshape(3, seq, LANES)
        per_group.append(t)
    return jnp.stack(per_group, axis=1)


def _dilattn_kernel(a_ref, o_ref, lse_ref, *, res_len, tq, win, half, n_res):
    t = pl.program_id(2)
    nblk = tq // BAND_BLOCK
    stacked = HEADS_PER_DIL * BAND_BLOCK
    lane_head = lax.broadcasted_iota(I32, (BAND_BLOCK, ATTN_OUT_WIDTH), 1) // HEAD_DIM
    head_sel = [lane_head == h for h in range(HEADS_PER_DIL)]
    head_mask = [jnp.where(sel, 1.0, 0.0).astype(BF16) for sel in head_sel]
    jmi = (lax.broadcasted_iota(I32, (stacked, win), 1)
           - (lax.broadcasted_iota(I32, (stacked, win), 0) & (BAND_BLOCK - 1)))
    ones = jnp.ones((win, LANES), BF16)
    blocks = [(rr, n) for rr in range(n_res) for n in range(nblk)]

    s_parts, v_wins = [], []
    for rr, n in blocks:
        c0 = rr * ATTN_WIDTH
        qpos0 = pl.multiple_of(t * tq + n * BAND_BLOCK, BAND_BLOCK)
        start = pl.multiple_of(jnp.clip(qpos0 - BAND_BLOCK, 0, res_len - win), BAND_BLOCK)
        qb = a_ref[0, pl.ds(qpos0, BAND_BLOCK), c0:c0 + ATTN_OUT_WIDTH]
        kw = a_ref[0, pl.ds(start, win), c0 + ATTN_OUT_WIDTH:c0 + 2 * ATTN_OUT_WIDTH]
        v_wins.append(a_ref[0, pl.ds(start, win), c0 + 2 * ATTN_OUT_WIDTH:c0 + 3 * ATTN_OUT_WIDTH])
        qs = jnp.concatenate([qb * hm for hm in head_mask], axis=0)
        s = lax.dot_general(qs, kw, (((1,), (1,)), ((), ())), preferred_element_type=F32)
        diff = jmi + (start - qpos0)
        s_parts.append(jnp.where((diff >= -half) & (diff <= half), s, NEG_INF))
    s_all = jnp.concatenate(s_parts, axis=0)
    m_all = jnp.max(s_all, axis=-1, keepdims=True)
    p_all = jnp.exp(s_all - m_all).astype(BF16)

    for i, (rr, n) in enumerate(blocks):
        p = p_all[i * stacked:(i + 1) * stacked]
        o_st = jnp.dot(p, v_wins[i], preferred_element_type=F32)
        l_st = jnp.dot(p, ones, preferred_element_type=F32)
        m_st = m_all[i * stacked:(i + 1) * stacked]
        o = jnp.zeros((BAND_BLOCK, ATTN_OUT_WIDTH), F32)
        l = jnp.ones((BAND_BLOCK, ATTN_OUT_WIDTH), F32)
        m = jnp.zeros((BAND_BLOCK, ATTN_OUT_WIDTH), F32)
        for h in range(HEADS_PER_DIL):
            rs = slice(h * BAND_BLOCK, (h + 1) * BAND_BLOCK)
            o = jnp.where(head_sel[h], o_st[rs], o)
            l = jnp.where(head_sel[h], jnp.concatenate([l_st[rs], l_st[rs]], axis=1), l)
            m = jnp.where(head_sel[h], m_st[rs], m)
        rows = slice(n * BAND_BLOCK, (n + 1) * BAND_BLOCK)
        cols = slice(rr * ATTN_OUT_WIDTH, (rr + 1) * ATTN_OUT_WIDTH)
        o_ref[0, rows, cols] = (o / l).astype(BF16)
        lse_ref[0, rows, cols] = m + jnp.log(l)


ATTN_BLOCKS_PER_STEP = 16


def _dilattn(a_grp, batch, seq, group):
    window, dil = DIL_CONFIGS[group]
    half = window // (2 * dil)
    res_len = seq // dil
    win = min(3 * BAND_BLOCK, res_len)
    n_res = min(dil, 4)
    tq = min(res_len, ATTN_BLOCKS_PER_STEP // n_res * BAND_BLOCK)
    qkv3 = a_grp.reshape(batch, res_len, dil * ATTN_WIDTH)
    kern = functools.partial(_dilattn_kernel, res_len=res_len, tq=tq, win=win, half=half, n_res=n_res)
    out_spec = pl.BlockSpec((1, tq, n_res * ATTN_OUT_WIDTH), lambda b, r, t: (b, t, r))
    o, lse = pl.pallas_call(
        kern,
        grid=(batch, dil // n_res, res_len // tq),
        in_specs=[pl.BlockSpec((1, res_len, n_res * ATTN_WIDTH), lambda b, r, t: (b, 0, r))],
        out_specs=[out_spec, out_spec],
        out_shape=[
            jax.ShapeDtypeStruct((batch, res_len, dil * ATTN_OUT_WIDTH), BF16),
            jax.ShapeDtypeStruct((batch, res_len, dil * ATTN_OUT_WIDTH), F32),
        ],
        compiler_params=pltpu.CompilerParams(dimension_semantics=("parallel", "parallel", "parallel"),
                                             vmem_limit_bytes=VMEM_LIMIT),
        name=f"dilattn{group}",
    )(qkv3)
    return (o.reshape(batch * res_len, dil * ATTN_OUT_WIDTH), lse.reshape(batch * res_len, dil * ATTN_OUT_WIDTH))


def _mix_kernel(x_ref, u_ref, up_ref, un_ref, qc_ref, gate_ref, o0_ref, o1_ref, o2_ref, l0_ref, l1_ref, l2_ref,
                kv_ref, wpool_ref, pscale_ref, wbp_ref, wba_ref, wbc_ref, wout_ref, h_ref, ubuf, merged, relay, *, seq):
    tm = TOKEN_TILE
    i = pl.program_id(1)
    nt = pl.num_programs(1)

    ubuf[0:POOL_HALO, :] = jnp.where(i > 0, up_ref[...], 0.0)
    ubuf[POOL_HALO:POOL_HALO + tm, :] = u_ref[...]
    ubuf[POOL_HALO + tm:, :] = jnp.where(i < nt - 1, un_ref[...], 0.0)
    pos = i * tm + lax.broadcasted_iota(I32, (tm, POOL_GROUP_DIM), 0)
    pool_parts = []
    for g, w in enumerate(POOL_WINDOWS):
        cs = slice(g * POOL_GROUP_DIM, (g + 1) * POOL_GROUP_DIM)
        acc = ubuf[POOL_HALO - w // 2:POOL_HALO - w // 2 + tm, cs]
        for j in range(1, w):
            off = POOL_HALO - w // 2 + j
            acc = acc + ubuf[off:off + tm, cs]
        cnt = (jnp.minimum(pos + w // 2, seq) - jnp.maximum(pos - w // 2, 0)).astype(F32)
        z = acc / cnt - ubuf[POOL_HALO:POOL_HALO + tm, cs]
        zp = jnp.dot(z.astype(BF16), wpool_ref[g], preferred_element_type=F32)
        pool_parts.append(zp * pscale_ref[:, cs])
    pool_bf = jnp.concatenate(pool_parts, axis=1).astype(BF16)

    cross_parts = []
    for h in range(CROSS_HEADS):
        cs = slice(h * CROSS_HEAD_DIM, (h + 1) * CROSS_HEAD_DIM)
        kh = kv_ref[0, :, cs]
        vh = kv_ref[0, :, CROSS_WIDTH + h * CROSS_HEAD_DIM:CROSS_WIDTH + (h + 1) * CROSS_HEAD_DIM]
        s = lax.dot_general(qc_ref[:, cs], kh, (((1,), (1,)), ((), ())), preferred_element_type=F32)
        s = s * (1.0 / math.sqrt(CROSS_HEAD_DIM))
        m = jnp.max(s, axis=-1, keepdims=True)
        p = jnp.exp(s - m)
        l = jnp.sum(p, axis=-1, keepdims=True)
        cross_parts.append(jnp.dot(p.astype(BF16), vh, preferred_element_type=F32) / l)
    cross_bf = jnp.concatenate(cross_parts, axis=1).astype(BF16)

    def token_major(ref, slot, dil):
        if dil == 1:
            return ref[...].astype(F32)
        res_rows = tm // dil
        for r in range(dil):
            for hh in range(ATTN_OUT_WIDTH // LANES):
                c0 = r * ATTN_OUT_WIDTH + hh * LANES
                relay[slot, hh, pl.ds(r, res_rows, stride=dil), :] = ref[:, c0:c0 + LANES].astype(F32)
        return jnp.concatenate([relay[slot, hh] for hh in range(ATTN_OUT_WIDTH // LANES)], axis=1)

    dils = [d for _, d in DIL_CONFIGS]
    o0, o1, o2 = [token_major(r, s, d) for r, s, d in zip((o0_ref, o1_ref, o2_ref), (0, 1, 2), dils)]
    l0, l1, l2 = [token_major(r, s, d) for r, s, d in zip((l0_ref, l1_ref, l2_ref), (3, 4, 5), dils)]
    mx = jnp.maximum(jnp.maximum(l0, l1), l2)
    e0, e1, e2 = jnp.exp(l0 - mx), jnp.exp(l1 - mx), jnp.exp(l2 - mx)
    attn = (e0 * o0 + e1 * o1 + e2 * o2) / (e0 + e1 + e2)
    attn_bf = attn.astype(BF16)

    for c in range(D_MODEL // MIX_CHUNK):
        cs = slice(c * MIX_CHUNK, (c + 1) * MIX_CHUNK)
        mrg = gate_ref[:, cs].astype(F32) * jnp.dot(pool_bf, wbp_ref[:, cs], preferred_element_type=F32)
        mrg = mrg + gate_ref[:, D_MODEL + c * MIX_CHUNK:D_MODEL + (c + 1) * MIX_CHUNK].astype(F32) * jnp.dot(
            attn_bf, wba_ref[:, cs], preferred_element_type=F32)
        mrg = mrg + gate_ref[:, 2 * D_MODEL + c * MIX_CHUNK:2 * D_MODEL + (c + 1) * MIX_CHUNK].astype(F32) * jnp.dot(
            cross_bf, wbc_ref[:, cs], preferred_element_type=F32)
        merged[:, cs] = mrg.astype(BF16)
    h_ref[...] = x_ref[...] + jnp.dot(merged[...], wout_ref[...], preferred_element_type=F32)


def _mix(x2d, u, qc, gate, outs, lses, kv3, wpool_bf, pscale, wbp, wba, wbc, wout, batch, seq):
    tm = TOKEN_TILE
    ts = seq // tm
    rows = batch * seq
    hb = tm // POOL_HALO
    n_halo = rows // POOL_HALO

    def row(b, i):
        return (b * ts + i, 0)

    def const(b, i):
        return (0, 0)

    tok = lambda w: pl.BlockSpec((tm, w), row)
    in_specs = [
        tok(D_MODEL),
        tok(POOL_WIDTH),
        pl.BlockSpec((POOL_HALO, POOL_WIDTH), lambda b, i: (jnp.maximum((b * ts + i) * hb - 1, 0), 0)),
        pl.BlockSpec((POOL_HALO, POOL_WIDTH), lambda b, i: (jnp.minimum((b * ts + i + 1) * hb, n_halo - 1), 0)),
        tok(CROSS_WIDTH),
        tok(N_BRANCHES * D_MODEL),
        *[pl.BlockSpec((tm // d, d * ATTN_OUT_WIDTH), row) for _, d in DIL_CONFIGS],
        *[pl.BlockSpec((tm // d, d * ATTN_OUT_WIDTH), row) for _, d in DIL_CONFIGS],
        pl.BlockSpec((1, kv3.shape[1], 2 * CROSS_WIDTH), lambda b, i: (b, 0, 0)),
        pl.BlockSpec((POOL_GROUPS, POOL_GROUP_DIM, POOL_GROUP_DIM), lambda b, i: (0, 0, 0)),
        pl.BlockSpec((1, POOL_WIDTH), const),
        pl.BlockSpec((POOL_WIDTH, D_MODEL), const),
        pl.BlockSpec((ATTN_OUT_WIDTH, D_MODEL), const),
        pl.BlockSpec((CROSS_WIDTH, D_MODEL), const),
        pl.BlockSpec((D_MODEL, D_MODEL), const),
    ]
    return pl.pallas_call(
        functools.partial(_mix_kernel, seq=seq),
        grid=(batch, ts),
        in_specs=in_specs,
        out_specs=pl.BlockSpec((tm, D_MODEL), row),
        out_shape=jax.ShapeDtypeStruct((rows, D_MODEL), F32),
        scratch_shapes=[pltpu.VMEM((tm + 2 * POOL_HALO, POOL_WIDTH), F32), pltpu.VMEM((tm, D_MODEL), BF16),
                        pltpu.VMEM((2 * len(DIL_CONFIGS), ATTN_OUT_WIDTH // LANES, tm, LANES), F32)],
        compiler_params=pltpu.CompilerParams(dimension_semantics=("parallel", "parallel"), vmem_limit_bytes=VMEM_LIMIT),
        name="mix",
    )(x2d, u, u, u, qc, gate, *outs, *lses, kv3, wpool_bf, pscale, wbp, wba, wbc, wout)


def _route_kernel(h_ref, g_ref, whi_ref, wlo_ref, b_ref, hn_ref, ri_ref, rg_ref, cnt_ref, carry, tri):
    tm = TOKEN_TILE
    step = pl.program_id(0)

    @pl.when(step == 0)
    def _():
        carry[...] = jnp.zeros_like(carry)
        r = lax.broadcasted_iota(I32, (tm, tm), 0)
        c = lax.broadcasted_iota(I32, (tm, tm), 1)
        tri[...] = jnp.where(r < c, 1.0, 0.0).astype(BF16)

    hn = _rms(h_ref[...], g_ref[...])
    hn_ref[...] = _pack_bf16_pair(hn)
    hi = hn.astype(BF16)
    lo = (hn - hi.astype(F32)).astype(BF16)
    logits = (jnp.dot(hi, whi_ref[...], preferred_element_type=F32)
              + jnp.dot(lo, whi_ref[...], preferred_element_type=F32)
              + jnp.dot(hi, wlo_ref[...], preferred_element_type=F32)) + b_ref[...]
    work = logits.T[:N_EXPERTS]
    row = lax.broadcasted_iota(I32, (N_EXPERTS, tm), 0).astype(F32)
    idxs, vals = [], []
    onehot = jnp.zeros((N_EXPERTS, tm), F32)
    for _ in range(TOP_K):
        m = jnp.max(work, axis=0, keepdims=True)
        idx = jnp.min(jnp.where(work == m, row, float(N_EXPERTS)), axis=0, keepdims=True)
        sel = row == idx
        onehot = jnp.where(sel, 1.0, onehot)
        work = jnp.where(sel, -jnp.inf, work)
        idxs.append(idx)
        vals.append(m)
    exps = [jnp.exp(v - vals[0]) for v in vals]
    den = exps[0] + exps[1] + exps[2] + exps[3]
    gates = [e / den for e in exps]

    prefix = jnp.dot(onehot.astype(BF16), tri[...], preferred_element_type=F32) + carry[:, 0:1]
    ranks = [jnp.sum(jnp.where(row == idx, prefix, 0.0), axis=0, keepdims=True) for idx in idxs]
    carry[...] = carry[...] + jnp.sum(onehot, axis=1, keepdims=True)

    row8 = lax.broadcasted_iota(I32, (2 * TOP_K, tm), 0)
    ri = jnp.zeros((2 * TOP_K, tm), F32)
    row128 = lax.broadcasted_iota(I32, (LANES, tm), 0)
    rg = jnp.zeros((LANES, tm), F32)
    for k in range(TOP_K):
        ri = jnp.where(row8 == k, idxs[k], ri)
        ri = jnp.where(row8 == TOP_K + k, ranks[k], ri)
        rg = jnp.where(row128 == k, gates[k], rg)
    ri_ref[...] = ri.astype(I32)
    rg_ref[...] = rg.T
    cnt_ref[...] = carry[...].astype(I32)


def _route(h2d, g, whi, wlo, b):
    rows = h2d.shape[0]
    tm = TOKEN_TILE
    const = lambda i: (0, 0)
    return pl.pallas_call(
        _route_kernel,
        grid=(rows // tm,),
        in_specs=[
            pl.BlockSpec((tm, D_MODEL), lambda i: (i, 0)),
            pl.BlockSpec((1, D_MODEL), const),
            pl.BlockSpec((D_MODEL, LANES), const),
            pl.BlockSpec((D_MODEL, LANES), const),
            pl.BlockSpec((1, LANES), const),
        ],
        out_specs=[
            pl.BlockSpec((tm, PACKED), lambda i: (i, 0)),
            pl.BlockSpec((2 * TOP_K, tm), lambda i: (0, i)),
            pl.BlockSpec((tm, LANES), lambda i: (i, 0)),
            pl.BlockSpec((N_EXPERTS, LANES), const),
        ],
        out_shape=[
            jax.ShapeDtypeStruct((rows, PACKED), I32),
            jax.ShapeDtypeStruct((2 * TOP_K, rows), I32),
            jax.ShapeDtypeStruct((rows, LANES), F32),
            jax.ShapeDtypeStruct((N_EXPERTS, LANES), I32),
        ],
        scratch_shapes=[pltpu.VMEM((N_EXPERTS, LANES), F32), pltpu.VMEM((tm, tm), BF16)],
        compiler_params=pltpu.CompilerParams(dimension_semantics=("arbitrary",), vmem_limit_bytes=VMEM_LIMIT),
        name="route",
    )(h2d, g, whi, wlo, b)


def _sc_mesh():
    return plsc.VectorSubcoreMesh(core_axis_name="c", subcore_axis_name="s")


def _sc_workers():
    info = plsc.get_sparse_core_info()
    return info.num_cores, info.num_cores * info.num_subcores


def _sc_dispatch(rows_packed, slots, n_slots):
    n_tok = rows_packed.shape[0]
    n_cores, n_workers = _sc_workers()
    chunks_per_worker = n_tok // SC_WINDOW // n_workers

    @functools.partial(pl.kernel, out_type=jax.ShapeDtypeStruct((n_slots, PACKED), I32), mesh=_sc_mesh(),
                       scratch_types=[pltpu.VMEM((TOP_K, SC_WINDOW), I32), pltpu.VMEM((SC_WINDOW, PACKED), I32)],
                       name="dispatch")
    def kern(x_hbm, i_hbm, o_hbm, idx_v, rows_v):
        wid = lax.axis_index("s") * n_cores + lax.axis_index("c")

        @pl.loop(0, chunks_per_worker)
        def _(j):
            chunk = wid * chunks_per_worker + j
            pltpu.sync_copy(i_hbm.at[chunk], idx_v)
            pltpu.sync_copy(x_hbm.at[pl.ds(chunk * SC_WINDOW, SC_WINDOW)], rows_v)
            for k in range(TOP_K):
                pltpu.sync_copy(rows_v, o_hbm.at[idx_v.at[k]])

    return kern(rows_packed, slots)


def _sc_combine(ys, slots):
    n_tok = slots.shape[0] * SC_WINDOW
    n_cores, n_workers = _sc_workers()
    chunks_per_worker = n_tok // SC_WINDOW // n_workers

    @functools.partial(pl.kernel, out_type=jax.ShapeDtypeStruct((TOP_K, n_tok, PACKED), I32), mesh=_sc_mesh(),
                       scratch_types=[pltpu.VMEM((TOP_K, SC_WINDOW), I32), pltpu.VMEM((SC_WINDOW, PACKED), I32)],
                       name="combine")
    def kern(y_hbm, i_hbm, o_hbm, idx_v, rows_v):
        wid = lax.axis_index("s") * n_cores + lax.axis_index("c")

        @pl.loop(0, chunks_per_worker)
        def _(j):
            chunk = wid * chunks_per_worker + j
            pltpu.sync_copy(i_hbm.at[chunk], idx_v)
            for k in range(TOP_K):
                pltpu.sync_copy(y_hbm.at[idx_v.at[k]], rows_v)
                pltpu.sync_copy(rows_v, o_hbm.at[k, pl.ds(chunk * SC_WINDOW, SC_WINDOW)])

    return kern(ys, slots)


def _expert_kernel(blk_e_ref, nused_ref, x_ref, wu_ref, bu_ref, wd_ref, bd_ref, *refs, block_lo):
    o_ref, done_ref, wu_bf, wd_bf = refs[-4:]
    step = pl.program_id(0)
    i = step + block_lo
    used = i < nused_ref[0]

    @pl.when(step == 0)
    def _():
        done_ref[...] = jnp.zeros_like(done_ref)

    @pl.when(used & ((step == 0) | (blk_e_ref[i] != blk_e_ref[jnp.maximum(i - 1, 0)])))
    def _():
        for c in range(2 * D_FF // CAST_CHUNK):
            cs = slice(c * CAST_CHUNK, (c + 1) * CAST_CHUNK)
            wu_bf[:, cs] = wu_ref[0, :, cs].astype(BF16)
        for c in range(D_MODEL // CAST_CHUNK):
            cs = slice(c * CAST_CHUNK, (c + 1) * CAST_CHUNK)
            wd_bf[:, cs] = wd_ref[0, :, cs].astype(BF16)

    @pl.when(used)
    def _():
        lo, hi = _unpack_bf16_pair(x_ref[...])
        x = jnp.concatenate([lo.astype(BF16), hi.astype(BF16)], axis=1)
        y = jnp.zeros((EXPERT_BLOCK, D_MODEL), F32)
        for c in range(D_FF // FF_CHUNK):
            gs = slice(c * FF_CHUNK, (c + 1) * FF_CHUNK)
            us = slice(D_FF + c * FF_CHUNK, D_FF + (c + 1) * FF_CHUNK)
            gate = jnp.dot(x, wu_bf[:, gs], preferred_element_type=F32) + bu_ref[0, :, gs]
            up = jnp.dot(x, wu_bf[:, us], preferred_element_type=F32) + bu_ref[0, :, us]
            gate = jnp.minimum(gate, SWIGLU_LIMIT)
            up = jnp.clip(up, -SWIGLU_LIMIT, SWIGLU_LIMIT)
            act = (up + 1.0) * (gate * jax.nn.sigmoid(SWIGLU_ALPHA * gate))
            y = y + jnp.dot(act.astype(BF16), wd_bf[gs, :], preferred_element_type=F32)
        o_ref[...] = _pack_bf16_pair(y + bd_ref[0])

    @pl.when(jnp.logical_not(used))
    def _():
        o_ref[...] = jnp.zeros_like(o_ref)


def _experts(xs, blk_e, nused, wu, bu, wd, bd, block_lo, block_hi, ys_prev=None, after=()):
    n_slots = xs.shape[0]
    tb = EXPERT_BLOCK
    extra = ([] if ys_prev is None else [ys_prev]) + list(after)
    grid_spec = pltpu.PrefetchScalarGridSpec(
        num_scalar_prefetch=2,
        grid=(block_hi - block_lo,),
        in_specs=[
            pl.BlockSpec((tb, PACKED), lambda i, be, nu: (i + block_lo, 0)),
            pl.BlockSpec((1, D_MODEL, 2 * D_FF), lambda i, be, nu: (be[i + block_lo], 0, 0)),
            pl.BlockSpec((1, 1, 2 * D_FF), lambda i, be, nu: (be[i + block_lo], 0, 0)),
            pl.BlockSpec((1, D_FF, D_MODEL), lambda i, be, nu: (be[i + block_lo], 0, 0)),
            pl.BlockSpec((1, 1, D_MODEL), lambda i, be, nu: (be[i + block_lo], 0, 0)),
            *[pl.BlockSpec(memory_space=pl.ANY) for _ in extra],
        ],
        out_specs=[pl.BlockSpec((tb, PACKED), lambda i, be, nu: (i + block_lo, 0)),
                   pl.BlockSpec((8, LANES), lambda i, be, nu: (0, 0))],
        scratch_shapes=[pltpu.VMEM((D_MODEL, 2 * D_FF), BF16), pltpu.VMEM((D_FF, D_MODEL), BF16)],
    )
    n_fixed_inputs = 7
    return pl.pallas_call(
        functools.partial(_expert_kernel, block_lo=block_lo),
        grid_spec=grid_spec,
        out_shape=[jax.ShapeDtypeStruct((n_slots, PACKED), I32), jax.ShapeDtypeStruct((8, LANES), I32)],
        input_output_aliases={} if ys_prev is None else {n_fixed_inputs: 0},
        compiler_params=pltpu.CompilerParams(dimension_semantics=("arbitrary",), vmem_limit_bytes=VMEM_LIMIT),
        name="experts",
    )(blk_e, nused, xs, wu, bu, wd, bd, *extra)


def _final_kernel(h_ref, yg_ref, rg_ref, g_ref, *refs):
    o_ref = refs[-1]
    rg = rg_ref[...]
    lo = jnp.zeros((TOKEN_TILE, PACKED), F32)
    hi = jnp.zeros((TOKEN_TILE, PACKED), F32)
    for k in range(TOP_K):
        yl, yh = _unpack_bf16_pair(yg_ref[k])
        w = rg[:, k:k + 1]
        lo = lo + w * yl
        hi = hi + w * yh
    x = h_ref[...] + jnp.concatenate([lo, hi], axis=1)
    o_ref[...] = _rms(x, g_ref[...])


def _final(h2d, yg, rg, g, after=()):
    rows = h2d.shape[0]
    tm = TOKEN_TILE
    return pl.pallas_call(
        _final_kernel,
        grid=(rows // tm,),
        in_specs=[
            pl.BlockSpec((tm, D_MODEL), lambda i: (i, 0)),
            pl.BlockSpec((TOP_K, tm, PACKED), lambda i: (0, i, 0)),
            pl.BlockSpec((tm, LANES), lambda i: (i, 0)),
            pl.BlockSpec((1, D_MODEL), lambda i: (0, 0)),
            *[pl.BlockSpec(memory_space=pl.ANY) for _ in after],
        ],
        out_specs=pl.BlockSpec((tm, D_MODEL), lambda i: (i, 0)),
        out_shape=jax.ShapeDtypeStruct((rows, D_MODEL), F32),
        compiler_params=pltpu.CompilerParams(dimension_semantics=("parallel",), vmem_limit_bytes=VMEM_LIMIT),
        name="final",
    )(h2d, yg, rg, g, *after)


def _plan(ri, counts, n_tok):
    tb = EXPERT_BLOCK
    n_slots = n_tok * TOP_K + N_EXPERTS * tb
    eid = ri[:TOP_K]
    rank = ri[TOP_K:]
    c = counts[:, 0]
    pc = (c + tb - 1) // tb * tb
    pend = jnp.cumsum(pc)
    pstart = pend - pc
    base = jnp.zeros_like(eid)
    for e in range(N_EXPERTS):
        base = jnp.where(eid == e, pstart[e], base)
    slots = (base + rank).astype(I32)
    slots = slots.reshape(TOP_K, n_tok // SC_WINDOW, SC_WINDOW).transpose(1, 0, 2)
    blk_start = jnp.arange(n_slots // tb, dtype=I32) * tb
    blk_e = jnp.minimum(jnp.sum(blk_start[:, None] >= pend[None, :], axis=1), N_EXPERTS - 1).astype(I32)
    nused = (pend[-1:] // tb).astype(I32)
    return slots, blk_e, nused, n_slots


def _front(x, mem, p):
    batch, seq, _ = x.shape
    n_tok = batch * seq
    x2d = x.reshape(n_tok, D_MODEL)
    kv = _memkv(mem.reshape(-1, D_MODEL), p["norm_mem_g"], p["w_mem_kv"])
    kv3 = kv.reshape(batch, mem.shape[1], 2 * CROSS_WIDTH)
    u, a0, a1, a2, qc, gate = _proj(x2d, p["norm_mix_g"], p["w_in"], _rope_tables(seq), seq)
    outs, lses = [], []
    for g, a_grp in enumerate((a0, a1, a2)):
        o, lse = _dilattn(a_grp, batch, seq, g)
        outs.append(o)
        lses.append(lse)
    h = _mix(x2d, u, qc, gate, outs, lses, kv3, p["w_pool"], p["pool_scale"], p["w_br_pool"], p["w_br_attn"],
             p["w_br_cross"], p["w_out"], batch, seq)
    hn, ri, rg, counts = _route(h, p["norm_ffn_g"], p["w_router_hi"], p["w_router_lo"], p["b_router"])
    slots, blk_e, nused, n_slots = _plan(ri, counts, n_tok)
    xs = _sc_dispatch(hn, slots, n_slots)
    return dict(h=h, rg=rg, slots=slots, blk_e=blk_e, nused=nused, xs=xs, shape=(batch, seq, D_MODEL))


EXPERT_SPLIT_BLOCKS = 160


def _moe_tail(first, last, p):
    w = (p["w_up"], p["b_up"], p["w_down"], p["b_down"])
    n_first = first["xs"].shape[0] // EXPERT_BLOCK
    n_last = last["xs"].shape[0] // EXPERT_BLOCK
    split = min(EXPERT_SPLIT_BLOCKS, n_last // 2)
    ys_f, _ = _experts(first["xs"], first["blk_e"], first["nused"], *w, 0, n_first)
    yg_f = _sc_combine(ys_f, first["slots"])
    ys_l, done = _experts(last["xs"], last["blk_e"], last["nused"], *w, 0, split)
    out_f = _final(first["h"], yg_f, first["rg"], p["norm_final_g"], after=(done,))
    ys_l, _ = _experts(last["xs"], last["blk_e"], last["nused"], *w, split, n_last, ys_prev=ys_l, after=(out_f,))
    yg_l = _sc_combine(ys_l, last["slots"])
    out_l = _final(last["h"], yg_l, last["rg"], p["norm_final_g"])
    return out_f.reshape(first["shape"]), out_l.reshape(last["shape"])


def _prep_params(norm_mix_g, norm_mem_g, w_in, w_pool, pool_scale, w_mem_kv, w_br_pool, w_br_attn, w_br_cross,
                 w_out, norm_ffn_g, w_router, b_router, w_up, b_up, w_down, b_down, norm_final_g):
    wr = w_router[0]
    wr_hi = wr.astype(BF16)
    return dict(
        norm_mix_g=norm_mix_g[0].reshape(1, D_MODEL),
        norm_mem_g=norm_mem_g[0].reshape(1, D_MODEL),
        w_in=w_in[0].astype(BF16),
        w_pool=w_pool[0].astype(BF16),
        pool_scale=pool_scale[0].reshape(1, POOL_WIDTH),
        w_mem_kv=w_mem_kv[0].astype(BF16),
        w_br_pool=w_br_pool[0].astype(BF16),
        w_br_attn=w_br_attn[0].astype(BF16),
        w_br_cross=w_br_cross[0].astype(BF16),
        w_out=w_out[0].astype(BF16),
        norm_ffn_g=norm_ffn_g[0].reshape(1, D_MODEL),
        w_router_hi=jnp.pad(wr_hi, ((0, 0), (0, LANES - N_EXPERTS))),
        w_router_lo=jnp.pad((wr - wr_hi.astype(F32)).astype(BF16), ((0, 0), (0, LANES - N_EXPERTS))),
        b_router=jnp.pad(b_router[0].reshape(1, N_EXPERTS), ((0, 0), (0, LANES - N_EXPERTS))),
        w_up=w_up[0],
        b_up=b_up[0].reshape(N_EXPERTS, 1, 2 * D_FF),
        w_down=w_down[0],
        b_down=b_down[0].reshape(N_EXPERTS, 1, D_MODEL),
        norm_final_g=norm_final_g.reshape(1, D_MODEL),
    )


def kernel(x_prompt, x_sample, mem_prompt, mem_sample, norm_mix_g, norm_mem_g, w_in, w_pool, pool_scale, w_mem_kv,
           w_br_pool, w_br_attn, w_br_cross, w_out, norm_ffn_g, w_router, b_router, w_up, b_up, w_down, b_down,
           norm_final_g):
    p = _prep_params(norm_mix_g, norm_mem_g, w_in, w_pool, pool_scale, w_mem_kv, w_br_pool, w_br_attn, w_br_cross,
                     w_out, norm_ffn_g, w_router, b_router, w_up, b_up, w_down, b_down, norm_final_g)
    sample = _front(x_sample, mem_sample, p)
    prompt = _front(x_prompt, mem_prompt, p)
    y_sample, y_prompt = _moe_tail(sample, prompt, p)
    return (y_prompt, y_sample)
```

```python
import functools
import math

import jax
import jax.numpy as jnp
from jax import lax
from jax.experimental import pallas as pl
from jax.experimental.pallas import tpu as pltpu
from jax.experimental.pallas import tpu_sc as plsc

F32 = jnp.float32
BF16 = jnp.bfloat16
I32 = jnp.int32
U32 = jnp.uint32

D_MODEL = 1024
POOL_GROUPS = 4
POOL_WIDTH = 512
POOL_GROUP_DIM = 128
POOL_WINDOWS = (2, 4, 8, 16)
POOL_HALO = 16
POOL_EDGE = 8
HEAD_DIM = 64
DIL_CONFIGS = ((128, 1), (512, 4), (2048, 16))
HEADS_PER_DIL = 4
ATTN_WIDTH = 768
ATTN_OUT_WIDTH = 256
BAND_BLOCK = 64
ROPE_DIM = 16
ROPE_THETA = 500000.0
CROSS_HEADS = 4
CROSS_HEAD_DIM = 128
CROSS_WIDTH = 512
N_BRANCHES = 3
QKV_WIDTH = 3 * ATTN_WIDTH
IN_WIDTH = POOL_WIDTH + QKV_WIDTH + CROSS_WIDTH + N_BRANCHES * D_MODEL
N_EXPERTS = 32
TOP_K = 4
D_FF = 1024
SWIGLU_LIMIT = 7.0
SWIGLU_ALPHA = 1.702
EPS = 1e-5
NEG_INF = -1e30

LANES = 128
PACKED = D_MODEL // 2
TOKEN_TILE = 512
FINAL_TILE = 1024
PROJ_CHUNK = 512
MIX_CHUNK = 256
FF_CHUNK = 512
CAST_CHUNK = 512
EXPERT_BLOCK = 512
SC_WINDOW = 128
VMEM_LIMIT = 52 * 1024 * 1024


def _rms(x, g):
    r = lax.rsqrt(jnp.mean(x * x, axis=-1, keepdims=True) + EPS)
    return x * r * g


def _pack_bf16_pair(x):
    bits = lax.bitcast_convert_type(x.astype(BF16).astype(F32), U32)
    packed = (bits[:, :PACKED] >> 16) | bits[:, PACKED:]
    return lax.bitcast_convert_type(packed, I32)


def _unpack_bf16_pair(w):
    u = lax.bitcast_convert_type(w, U32)
    lo = lax.bitcast_convert_type(u << 16, F32)
    hi = lax.bitcast_convert_type(u & jnp.uint32(0xFFFF0000), F32)
    return lo, hi


def _memkv_kernel(mem_ref, g_ref, w_ref, o_ref):
    xn = _rms(mem_ref[...], g_ref[...]).astype(BF16)
    o_ref[...] = jnp.dot(xn, w_ref[...], preferred_element_type=F32).astype(BF16)


def _memkv(mem2d, g, w_bf):
    rows = mem2d.shape[0]
    tm = 256
    return pl.pallas_call(
        _memkv_kernel,
        grid=(rows // tm,),
        in_specs=[
            pl.BlockSpec((tm, D_MODEL), lambda i: (i, 0)),
            pl.BlockSpec((1, D_MODEL), lambda i: (0, 0)),
            pl.BlockSpec((D_MODEL, 2 * CROSS_WIDTH), lambda i: (0, 0)),
        ],
        out_specs=pl.BlockSpec((tm, 2 * CROSS_WIDTH), lambda i: (i, 0)),
        out_shape=jax.ShapeDtypeStruct((rows, 2 * CROSS_WIDTH), BF16),
        compiler_params=pltpu.CompilerParams(dimension_semantics=("parallel",)),
        name="memkv",
    )(mem2d, g, w_bf)


_O_QKV = POOL_WIDTH
_O_QC = _O_QKV + QKV_WIDTH
_O_GATE = _O_QC + CROSS_WIDTH


def _proj_kernel(x_ref, g_ref, w_ref, tab_ref, u_ref, a0_ref, a1_ref, a2_ref, qc_ref, gate_ref, xcols):
    tm = TOKEN_TILE
    n_cols = D_MODEL // LANES
    g = g_ref[...]
    xn = _rms(x_ref[...], g).astype(BF16)
    for c in range(n_cols):
        xcols[c] = x_ref[:, c * LANES:(c + 1) * LANES]

    def mm(lhs, c0, width=PROJ_CHUNK):
        return jnp.dot(lhs, w_ref[:, c0:c0 + width], preferred_element_type=F32)

    for j in range(POOL_WIDTH // PROJ_CHUNK):
        u_ref[:, j * PROJ_CHUNK:(j + 1) * PROJ_CHUNK] = mm(xn, j * PROJ_CHUNK)

    for grp, a_ref in enumerate((a0_ref, a1_ref, a2_ref)):
        dil = DIL_CONFIGS[grp][1]
        res_rows = tm // dil
        if dil == 1:
            lhs = xn
        else:
            xp = jnp.concatenate(
                [jnp.concatenate([xcols[c, pl.ds(r, res_rows, stride=dil), :] for c in range(n_cols)], axis=1)
                 for r in range(dil)], axis=0)
            lhs = _rms(xp, g).astype(BF16)
        cos, s1, s2 = tab_ref[0, grp], tab_ref[1, grp], tab_ref[2, grp]
        for which in range(3):
            t = mm(lhs, _O_QKV + which * ATTN_WIDTH + grp * ATTN_OUT_WIDTH, ATTN_OUT_WIDTH)
            if which < 2:
                halves = []
                for hh in range(ATTN_OUT_WIDTH // LANES):
                    th = t[:, hh * LANES:(hh + 1) * LANES]
                    th = (th * cos + pltpu.roll(th, LANES - ROPE_DIM // 2, axis=1) * s1
                          + pltpu.roll(th, ROPE_DIM // 2, axis=1) * s2)
                    halves.append(th)
                t = jnp.concatenate(halves, axis=1)
                if which == 0:
                    t = t * (1.0 / math.sqrt(HEAD_DIM))
            tb = t.astype(BF16)
            for r in range(dil):
                c0 = (r * 3 + which) * ATTN_OUT_WIDTH
                a_ref[:, c0:c0 + ATTN_OUT_WIDTH] = tb[r * res_rows:(r + 1) * res_rows, :]

    for j in range(CROSS_WIDTH // PROJ_CHUNK):
        qc_ref[:, j * PROJ_CHUNK:(j + 1) * PROJ_CHUNK] = mm(xn, _O_QC + j * PROJ_CHUNK).astype(BF16)

    for j in range(N_BRANCHES * D_MODEL // PROJ_CHUNK):
        gate_ref[:, j * PROJ_CHUNK:(j + 1) * PROJ_CHUNK] = jax.nn.sigmoid(mm(xn, _O_GATE + j * PROJ_CHUNK)).astype(BF16)


def _proj(x2d, g, w_in_bf, tabs, seq):
    rows = x2d.shape[0]
    tm = TOKEN_TILE
    tiles_per_seq = seq // tm
    n_grp = len(DIL_CONFIGS)
    a_specs = [pl.BlockSpec((tm // d, d * ATTN_WIDTH), lambda i: (i, 0)) for _, d in DIL_CONFIGS]
    a_shapes = [jax.ShapeDtypeStruct((rows // d, d * ATTN_WIDTH), BF16) for _, d in DIL_CONFIGS]
    return pl.pallas_call(
        _proj_kernel,
        grid=(rows // tm,),
        in_specs=[
            pl.BlockSpec((tm, D_MODEL), lambda i: (i, 0)),
            pl.BlockSpec((1, D_MODEL), lambda i: (0, 0)),
            pl.BlockSpec((D_MODEL, IN_WIDTH), lambda i: (0, 0), pipeline_mode=pl.Buffered(1)),
            pl.BlockSpec((3, n_grp, tm, LANES), lambda i: (0, 0, i % tiles_per_seq, 0)),
        ],
        out_specs=[
            pl.BlockSpec((tm, POOL_WIDTH), lambda i: (i, 0)),
            *a_specs,
            pl.BlockSpec((tm, CROSS_WIDTH), lambda i: (i, 0)),
            pl.BlockSpec((tm, N_BRANCHES * D_MODEL), lambda i: (i, 0)),
        ],
        out_shape=[
            jax.ShapeDtypeStruct((rows, POOL_WIDTH), F32),
            *a_shapes,
            jax.ShapeDtypeStruct((rows, CROSS_WIDTH), BF16),
            jax.ShapeDtypeStruct((rows, N_BRANCHES * D_MODEL), BF16),
        ],
        scratch_shapes=[pltpu.VMEM((D_MODEL // LANES, tm, LANES), F32)],
        compiler_params=pltpu.CompilerParams(dimension_semantics=("parallel",), vmem_limit_bytes=VMEM_LIMIT),
        name="proj",
    )(x2d, g, w_in_bf, tabs)


def _rope_tables(seq):
    pos = jnp.arange(seq, dtype=F32)
    inv = jnp.power(jnp.float32(ROPE_THETA), -jnp.arange(0, ROPE_DIM, 2, dtype=F32) / ROPE_DIM)
    ang = pos[:, None] * inv[None, :]
    half = ROPE_DIM // 2
    j = jnp.arange(LANES) % HEAD_DIM
    cos_l = jnp.cos(ang)[:, j % half]
    sin_l = jnp.sin(ang)[:, j % half]
    kinds = jnp.stack([
        jnp.where(j[None, :] < ROPE_DIM, cos_l, 1.0),
        jnp.where(j[None, :] < half, -sin_l, 0.0),
        jnp.where((j[None, :] >= half) & (j[None, :] < ROPE_DIM), sin_l, 0.0),
    ]).astype(F32)
    tm = TOKEN_TILE
    per_group = []
    for _, d in DIL_CONFIGS:
        t = kinds.reshape(3, seq // tm, tm // d, d, LANES).transpose(0, 1, 3, 2, 4).reshape(3, seq, LANES)
        per_group.append(t)
    return jnp.stack(per_group, axis=1)


def _dilattn_kernel(a_ref, bias_ref, o_ref, lse_ref, *, res_len, tq, win, n_res):
    t = pl.program_id(2)
    nblk = tq // BAND_BLOCK
    stacked = HEADS_PER_DIL * BAND_BLOCK
    lane_head = lax.broadcasted_iota(I32, (BAND_BLOCK, ATTN_OUT_WIDTH), 1) // HEAD_DIM
    head_sel = [lane_head == h for h in range(HEADS_PER_DIL)]
    head_mask = [jnp.where(sel, 1.0, 0.0).astype(BF16) for sel in head_sel]
    ones = jnp.ones((win, LANES), BF16)
    blocks = [(rr, n) for rr in range(n_res) for n in range(nblk)]

    s_parts, v_wins = [], []
    for rr, n in blocks:
        c0 = rr * ATTN_WIDTH
        qpos0 = pl.multiple_of(t * tq + n * BAND_BLOCK, BAND_BLOCK)
        start = pl.multiple_of(jnp.clip(qpos0 - BAND_BLOCK, 0, res_len - win), BAND_BLOCK)
        qb = a_ref[0, pl.ds(qpos0, BAND_BLOCK), c0:c0 + ATTN_OUT_WIDTH]
        kw = a_ref[0, pl.ds(start, win), c0 + ATTN_OUT_WIDTH:c0 + 2 * ATTN_OUT_WIDTH]
        v_wins.append(a_ref[0, pl.ds(start, win), c0 + 2 * ATTN_OUT_WIDTH:c0 + 3 * ATTN_OUT_WIDTH])
        qs = jnp.concatenate([qb * hm for hm in head_mask], axis=0)
        s = lax.dot_general(qs, kw, (((1,), (1,)), ((), ())), preferred_element_type=F32)
        s_parts.append(s + bias_ref[(qpos0 - start) // BAND_BLOCK])
    s_all = jnp.concatenate(s_parts, axis=0)
    m_all = jnp.max(s_all, axis=-1, keepdims=True)
    p_all = jnp.exp(s_all - m_all).astype(BF16)

    for i, (rr, n) in enumerate(blocks):
        p = p_all[i * stacked:(i + 1) * stacked]
        o_st = jnp.dot(p, v_wins[i], preferred_element_type=F32)
        l_st = jnp.dot(p, ones, preferred_element_type=F32)
        m_st = m_all[i * stacked:(i + 1) * stacked]
        o = jnp.zeros((BAND_BLOCK, ATTN_OUT_WIDTH), F32)
        l = jnp.ones((BAND_BLOCK, ATTN_OUT_WIDTH), F32)
        m = jnp.zeros((BAND_BLOCK, ATTN_OUT_WIDTH), F32)
        for h in range(HEADS_PER_DIL):
            rs = slice(h * BAND_BLOCK, (h + 1) * BAND_BLOCK)
            o = jnp.where(head_sel[h], o_st[rs], o)
            l = jnp.where(head_sel[h], jnp.concatenate([l_st[rs], l_st[rs]], axis=1), l)
            m = jnp.where(head_sel[h], m_st[rs], m)
        rows = slice(n * BAND_BLOCK, (n + 1) * BAND_BLOCK)
        cols = slice(rr * ATTN_OUT_WIDTH, (rr + 1) * ATTN_OUT_WIDTH)
        o_ref[0, rows, cols] = (o / l).astype(BF16)
        lse_ref[0, rows, cols] = m + jnp.log(l)


ATTN_BLOCKS_PER_STEP = 16


def _dilattn(a_grp, batch, seq, group):
    window, dil = DIL_CONFIGS[group]
    half = window // (2 * dil)
    res_len = seq // dil
    win = min(3 * BAND_BLOCK, res_len)
    n_res = min(dil, 4)
    tq = min(res_len, ATTN_BLOCKS_PER_STEP // n_res * BAND_BLOCK)
    qkv3 = a_grp.reshape(batch, res_len, dil * ATTN_WIDTH)
    kern = functools.partial(_dilattn_kernel, res_len=res_len, tq=tq, win=win, n_res=n_res)
    out_spec = pl.BlockSpec((1, tq, n_res * ATTN_OUT_WIDTH), lambda b, r, t: (b, t, r))
    n_off = win // BAND_BLOCK
    stacked = HEADS_PER_DIL * BAND_BLOCK
    key_minus_query = (jnp.arange(win)[None, None, :] - (jnp.arange(stacked) % BAND_BLOCK)[None, :, None]
                       - BAND_BLOCK * jnp.arange(n_off)[:, None, None])
    bias = jnp.where(jnp.abs(key_minus_query) <= half, 0.0, NEG_INF).astype(F32)
    o, lse = pl.pallas_call(
        kern,
        grid=(batch, dil // n_res, res_len // tq),
        in_specs=[pl.BlockSpec((1, res_len, n_res * ATTN_WIDTH), lambda b, r, t: (b, 0, r)),
                  pl.BlockSpec((n_off, stacked, win), lambda b, r, t: (0, 0, 0))],
        out_specs=[out_spec, out_spec],
        out_shape=[
            jax.ShapeDtypeStruct((batch, res_len, dil * ATTN_OUT_WIDTH), BF16),
            jax.ShapeDtypeStruct((batch, res_len, dil * ATTN_OUT_WIDTH), F32),
        ],
        compiler_params=pltpu.CompilerParams(dimension_semantics=("parallel", "parallel", "parallel"),
                                             vmem_limit_bytes=VMEM_LIMIT),
        name=f"dilattn{group}",
    )(qkv3, bias)
    return (o.reshape(batch * res_len, dil * ATTN_OUT_WIDTH), lse.reshape(batch * res_len, dil * ATTN_OUT_WIDTH))


def _mix_kernel(x_ref, u_ref, up_ref, un_ref, qc_ref, gate_ref, o0_ref, o1_ref, o2_ref, l0_ref, l1_ref, l2_ref,
                kv_ref, wpool_ref, pscale_ref, wbp_ref, wba_ref, wbc_ref, wout_ref, h_ref, ubuf, merged, relay, *, seq):
    tm = TOKEN_TILE
    i = pl.program_id(1)
    nt = pl.num_programs(1)

    ubuf[0:POOL_HALO, :] = jnp.where(i > 0, up_ref[...], 0.0)
    ubuf[POOL_HALO:POOL_HALO + tm, :] = u_ref[...]
    ubuf[POOL_HALO + tm:, :] = jnp.where(i < nt - 1, un_ref[...], 0.0)
    edge_row = lax.broadcasted_iota(I32, (POOL_EDGE, POOL_GROUP_DIM), 0)
    pos_top = i * tm + edge_row
    pos_bot = i * tm + (tm - POOL_EDGE) + edge_row
    pool_parts = []
    for g, w in enumerate(POOL_WINDOWS):
        cs = slice(g * POOL_GROUP_DIM, (g + 1) * POOL_GROUP_DIM)
        acc = ubuf[POOL_HALO - w // 2:POOL_HALO - w // 2 + tm, cs]
        for j in range(1, w):
            off = POOL_HALO - w // 2 + j
            acc = acc + ubuf[off:off + tm, cs]
        cnt_top = (jnp.minimum(pos_top + w // 2, seq) - jnp.maximum(pos_top - w // 2, 0)).astype(F32)
        cnt_bot = (jnp.minimum(pos_bot + w // 2, seq) - jnp.maximum(pos_bot - w // 2, 0)).astype(F32)
        mean = jnp.concatenate([acc[:POOL_EDGE] / cnt_top,
                                acc[POOL_EDGE:tm - POOL_EDGE] * (1.0 / w),
                                acc[tm - POOL_EDGE:] / cnt_bot], axis=0)
        z = mean - ubuf[POOL_HALO:POOL_HALO + tm, cs]
        zp = jnp.dot(z.astype(BF16), wpool_ref[g], preferred_element_type=F32)
        pool_parts.append(zp * pscale_ref[:, cs])
    pool_bf = jnp.concatenate(pool_parts, axis=1).astype(BF16)

    cross_parts = []
    for h in range(CROSS_HEADS):
        cs = slice(h * CROSS_HEAD_DIM, (h + 1) * CROSS_HEAD_DIM)
        kh = kv_ref[0, :, cs]
        vh = kv_ref[0, :, CROSS_WIDTH + h * CROSS_HEAD_DIM:CROSS_WIDTH + (h + 1) * CROSS_HEAD_DIM]
        s = lax.dot_general(qc_ref[:, cs], kh, (((1,), (1,)), ((), ())), preferred_element_type=F32)
        s = s * (1.0 / math.sqrt(CROSS_HEAD_DIM))
        m = jnp.max(s, axis=-1, keepdims=True)
        p = jnp.exp(s - m)
        l = jnp.sum(p, axis=-1, keepdims=True)
        cross_parts.append(jnp.dot(p.astype(BF16), vh, preferred_element_type=F32) / l)
    cross_bf = jnp.concatenate(cross_parts, axis=1).astype(BF16)

    def token_major(ref, slot, dil):
        if dil == 1:
            return ref[...].astype(F32)
        res_rows = tm // dil
        for r in range(dil):
            for hh in range(ATTN_OUT_WIDTH // LANES):
                c0 = r * ATTN_OUT_WIDTH + hh * LANES
                relay[slot, hh, pl.ds(r, res_rows, stride=dil), :] = ref[:, c0:c0 + LANES].astype(F32)
        return jnp.concatenate([relay[slot, hh] for hh in range(ATTN_OUT_WIDTH // LANES)], axis=1)

    dils = [d for _, d in DIL_CONFIGS]
    o0, o1, o2 = [token_major(r, s, d) for r, s, d in zip((o0_ref, o1_ref, o2_ref), (0, 1, 2), dils)]
    l0, l1, l2 = [token_major(r, s, d) for r, s, d in zip((l0_ref, l1_ref, l2_ref), (3, 4, 5), dils)]
    mx = jnp.maximum(jnp.maximum(l0, l1), l2)
    e0, e1, e2 = jnp.exp(l0 - mx), jnp.exp(l1 - mx), jnp.exp(l2 - mx)
    attn = (e0 * o0 + e1 * o1 + e2 * o2) / (e0 + e1 + e2)
    attn_bf = attn.astype(BF16)

    for c in range(D_MODEL // MIX_CHUNK):
        cs = slice(c * MIX_CHUNK, (c + 1) * MIX_CHUNK)
        mrg = gate_ref[:, cs].astype(F32) * jnp.dot(pool_bf, wbp_ref[:, cs], preferred_element_type=F32)
        mrg = mrg + gate_ref[:, D_MODEL + c * MIX_CHUNK:D_MODEL + (c + 1) * MIX_CHUNK].astype(F32) * jnp.dot(
            attn_bf, wba_ref[:, cs], preferred_element_type=F32)
        mrg = mrg + gate_ref[:, 2 * D_MODEL + c * MIX_CHUNK:2 * D_MODEL + (c + 1) * MIX_CHUNK].astype(F32) * jnp.dot(
            cross_bf, wbc_ref[:, cs], preferred_element_type=F32)
        merged[:, cs] = mrg.astype(BF16)
    h_ref[...] = x_ref[...] + jnp.dot(merged[...], wout_ref[...], preferred_element_type=F32)


def _mix(x2d, u, qc, gate, outs, lses, kv3, wpool_bf, pscale, wbp, wba, wbc, wout, batch, seq):
    tm = TOKEN_TILE
    ts = seq // tm
    rows = batch * seq
    hb = tm // POOL_HALO
    n_halo = rows // POOL_HALO

    def row(b, i):
        return (b * ts + i, 0)

    def const(b, i):
        return (0, 0)

    tok = lambda w: pl.BlockSpec((tm, w), row)
    in_specs = [
        tok(D_MODEL),
        tok(POOL_WIDTH),
        pl.BlockSpec((POOL_HALO, POOL_WIDTH), lambda b, i: (jnp.maximum((b * ts + i) * hb - 1, 0), 0)),
        pl.BlockSpec((POOL_HALO, POOL_WIDTH), lambda b, i: (jnp.minimum((b * ts + i + 1) * hb, n_halo - 1), 0)),
        tok(CROSS_WIDTH),
        tok(N_BRANCHES * D_MODEL),
        *[pl.BlockSpec((tm // d, d * ATTN_OUT_WIDTH), row) for _, d in DIL_CONFIGS],
        *[pl.BlockSpec((tm // d, d * ATTN_OUT_WIDTH), row) for _, d in DIL_CONFIGS],
        pl.BlockSpec((1, kv3.shape[1], 2 * CROSS_WIDTH), lambda b, i: (b, 0, 0)),
        pl.BlockSpec((POOL_GROUPS, POOL_GROUP_DIM, POOL_GROUP_DIM), lambda b, i: (0, 0, 0)),
        pl.BlockSpec((1, POOL_WIDTH), const),
        pl.BlockSpec((POOL_WIDTH, D_MODEL), const),
        pl.BlockSpec((ATTN_OUT_WIDTH, D_MODEL), const),
        pl.BlockSpec((CROSS_WIDTH, D_MODEL), const),
        pl.BlockSpec((D_MODEL, D_MODEL), const),
    ]
    return pl.pallas_call(
        functools.partial(_mix_kernel, seq=seq),
        grid=(batch, ts),
        in_specs=in_specs,
        out_specs=pl.BlockSpec((tm, D_MODEL), row),
        out_shape=jax.ShapeDtypeStruct((rows, D_MODEL), F32),
        scratch_shapes=[pltpu.VMEM((tm + 2 * POOL_HALO, POOL_WIDTH), F32), pltpu.VMEM((tm, D_MODEL), BF16),
                        pltpu.VMEM((2 * len(DIL_CONFIGS), ATTN_OUT_WIDTH // LANES, tm, LANES), F32)],
        compiler_params=pltpu.CompilerParams(dimension_semantics=("parallel", "parallel"), vmem_limit_bytes=VMEM_LIMIT),
        name="mix",
    )(x2d, u, u, u, qc, gate, *outs, *lses, kv3, wpool_bf, pscale, wbp, wba, wbc, wout)


def _route_kernel(h_ref, g_ref, whi_ref, wlo_ref, b_ref, hn_ref, ri_ref, rg_ref, cnt_ref, carry, tri):
    tm = TOKEN_TILE
    step = pl.program_id(0)

    @pl.when(step == 0)
    def _():
        carry[...] = jnp.zeros_like(carry)
        r = lax.broadcasted_iota(I32, (tm, tm), 0)
        c = lax.broadcasted_iota(I32, (tm, tm), 1)
        tri[...] = jnp.where(r < c, 1.0, 0.0).astype(BF16)

    hn = _rms(h_ref[...], g_ref[...])
    hn_ref[...] = _pack_bf16_pair(hn)
    hi = hn.astype(BF16)
    lo = (hn - hi.astype(F32)).astype(BF16)
    logits = (jnp.dot(hi, whi_ref[...], preferred_element_type=F32)
              + jnp.dot(lo, whi_ref[...], preferred_element_type=F32)
              + jnp.dot(hi, wlo_ref[...], preferred_element_type=F32)) + b_ref[...]
    work = logits.T[:N_EXPERTS]
    row = lax.broadcasted_iota(I32, (N_EXPERTS, tm), 0).astype(F32)
    idxs, vals = [], []
    onehot = jnp.zeros((N_EXPERTS, tm), F32)
    for _ in range(TOP_K):
        m = jnp.max(work, axis=0, keepdims=True)
        idx = jnp.min(jnp.where(work == m, row, float(N_EXPERTS)), axis=0, keepdims=True)
        sel = row == idx
        onehot = jnp.where(sel, 1.0, onehot)
        work = jnp.where(sel, -jnp.inf, work)
        idxs.append(idx)
        vals.append(m)
    exps = [jnp.exp(v - vals[0]) for v in vals]
    den = exps[0] + exps[1] + exps[2] + exps[3]
    gates = [e / den for e in exps]

    prefix = jnp.dot(onehot.astype(BF16), tri[...], preferred_element_type=F32) + carry[:, 0:1]
    ranks = [jnp.sum(jnp.where(row == idx, prefix, 0.0), axis=0, keepdims=True) for idx in idxs]
    carry[...] = carry[...] + jnp.sum(onehot, axis=1, keepdims=True)

    row8 = lax.broadcasted_iota(I32, (2 * TOP_K, tm), 0)
    ri = jnp.zeros((2 * TOP_K, tm), F32)
    row128 = lax.broadcasted_iota(I32, (LANES, tm), 0)
    rg = jnp.zeros((LANES, tm), F32)
    for k in range(TOP_K):
        ri = jnp.where(row8 == k, idxs[k], ri)
        ri = jnp.where(row8 == TOP_K + k, ranks[k], ri)
        rg = jnp.where(row128 == k, gates[k], rg)
    ri_ref[...] = ri.astype(I32)
    rg_ref[...] = rg.T
    cnt_ref[...] = carry[...].astype(I32)


def _route(h2d, g, whi, wlo, b):
    rows = h2d.shape[0]
    tm = TOKEN_TILE
    const = lambda i: (0, 0)
    return pl.pallas_call(
        _route_kernel,
        grid=(rows // tm,),
        in_specs=[
            pl.BlockSpec((tm, D_MODEL), lambda i: (i, 0)),
            pl.BlockSpec((1, D_MODEL), const),
            pl.BlockSpec((D_MODEL, LANES), const),
            pl.BlockSpec((D_MODEL, LANES), const),
            pl.BlockSpec((1, LANES), const),
        ],
        out_specs=[
            pl.BlockSpec((tm, PACKED), lambda i: (i, 0)),
            pl.BlockSpec((2 * TOP_K, tm), lambda i: (0, i)),
            pl.BlockSpec((tm, LANES), lambda i: (i, 0)),
            pl.BlockSpec((N_EXPERTS, LANES), const),
        ],
        out_shape=[
            jax.ShapeDtypeStruct((rows, PACKED), I32),
            jax.ShapeDtypeStruct((2 * TOP_K, rows), I32),
            jax.ShapeDtypeStruct((rows, LANES), F32),
            jax.ShapeDtypeStruct((N_EXPERTS, LANES), I32),
        ],
        scratch_shapes=[pltpu.VMEM((N_EXPERTS, LANES), F32), pltpu.VMEM((tm, tm), BF16)],
        compiler_params=pltpu.CompilerParams(dimension_semantics=("arbitrary",), vmem_limit_bytes=VMEM_LIMIT),
        name="route",
    )(h2d, g, whi, wlo, b)


def _sc_mesh():
    return plsc.VectorSubcoreMesh(core_axis_name="c", subcore_axis_name="s")


def _sc_workers():
    info = plsc.get_sparse_core_info()
    return info.num_cores, info.num_cores * info.num_subcores


def _sc_dispatch(rows_packed, slots, n_slots):
    n_tok = rows_packed.shape[0]
    n_cores, n_workers = _sc_workers()
    chunks_per_worker = n_tok // SC_WINDOW // n_workers

    @functools.partial(pl.kernel, out_type=jax.ShapeDtypeStruct((n_slots, PACKED), I32), mesh=_sc_mesh(),
                       scratch_types=[pltpu.VMEM((TOP_K, SC_WINDOW), I32), pltpu.VMEM((SC_WINDOW, PACKED), I32)],
                       name="dispatch")
    def kern(x_hbm, i_hbm, o_hbm, idx_v, rows_v):
        wid = lax.axis_index("s") * n_cores + lax.axis_index("c")

        @pl.loop(0, chunks_per_worker)
        def _(j):
            chunk = wid * chunks_per_worker + j
            pltpu.sync_copy(i_hbm.at[chunk], idx_v)
            pltpu.sync_copy(x_hbm.at[pl.ds(chunk * SC_WINDOW, SC_WINDOW)], rows_v)
            for k in range(TOP_K):
                pltpu.sync_copy(rows_v, o_hbm.at[idx_v.at[k]])

    return kern(rows_packed, slots)


def _sc_combine(ys, slots):
    n_tok = slots.shape[0] * SC_WINDOW
    n_cores, n_workers = _sc_workers()
    chunks_per_worker = n_tok // SC_WINDOW // n_workers

    @functools.partial(pl.kernel, out_type=jax.ShapeDtypeStruct((TOP_K, n_tok, PACKED), I32), mesh=_sc_mesh(),
                       scratch_types=[pltpu.VMEM((TOP_K, SC_WINDOW), I32), pltpu.VMEM((SC_WINDOW, PACKED), I32)],
                       name="combine")
    def kern(y_hbm, i_hbm, o_hbm, idx_v, rows_v):
        wid = lax.axis_index("s") * n_cores + lax.axis_index("c")

        @pl.loop(0, chunks_per_worker)
        def _(j):
            chunk = wid * chunks_per_worker + j
            pltpu.sync_copy(i_hbm.at[chunk], idx_v)
            for k in range(TOP_K):
                pltpu.sync_copy(y_hbm.at[idx_v.at[k]], rows_v)
                pltpu.sync_copy(rows_v, o_hbm.at[k, pl.ds(chunk * SC_WINDOW, SC_WINDOW)])

    return kern(ys, slots)


def _expert_kernel(blk_e_ref, nused_ref, x_ref, wu_ref, bu_ref, wd_ref, bd_ref, o_ref, wu_bf, wd_bf):
    i = pl.program_id(0)
    used = i < nused_ref[0]

    @pl.when(used & ((i == 0) | (blk_e_ref[i] != blk_e_ref[jnp.maximum(i - 1, 0)])))
    def _():
        for c in range(2 * D_FF // CAST_CHUNK):
            cs = slice(c * CAST_CHUNK, (c + 1) * CAST_CHUNK)
            wu_bf[:, cs] = wu_ref[0, :, cs].astype(BF16)
        for c in range(D_MODEL // CAST_CHUNK):
            cs = slice(c * CAST_CHUNK, (c + 1) * CAST_CHUNK)
            wd_bf[:, cs] = wd_ref[0, :, cs].astype(BF16)

    @pl.when(used)
    def _():
        lo, hi = _unpack_bf16_pair(x_ref[...])
        x = jnp.concatenate([lo.astype(BF16), hi.astype(BF16)], axis=1)
        y = jnp.zeros((EXPERT_BLOCK, D_MODEL), F32)
        for c in range(D_FF // FF_CHUNK):
            gs = slice(c * FF_CHUNK, (c + 1) * FF_CHUNK)
            us = slice(D_FF + c * FF_CHUNK, D_FF + (c + 1) * FF_CHUNK)
            gate = jnp.dot(x, wu_bf[:, gs], preferred_element_type=F32) + bu_ref[0, :, gs]
            up = jnp.dot(x, wu_bf[:, us], preferred_element_type=F32) + bu_ref[0, :, us]
            gate = jnp.minimum(gate, SWIGLU_LIMIT)
            up = jnp.clip(up, -SWIGLU_LIMIT, SWIGLU_LIMIT)
            act = (up + 1.0) * (gate * jax.nn.sigmoid(SWIGLU_ALPHA * gate))
            y = y + jnp.dot(act.astype(BF16), wd_bf[gs, :], preferred_element_type=F32)
        o_ref[...] = _pack_bf16_pair(y + bd_ref[0])

    @pl.when(jnp.logical_not(used))
    def _():
        o_ref[...] = jnp.zeros_like(o_ref)


def _experts(xs, blk_e, nused, wu, bu, wd, bd):
    n_slots = xs.shape[0]
    tb = EXPERT_BLOCK
    grid_spec = pltpu.PrefetchScalarGridSpec(
        num_scalar_prefetch=2,
        grid=(n_slots // tb,),
        in_specs=[
            pl.BlockSpec((tb, PACKED), lambda i, be, nu: (i, 0)),
            pl.BlockSpec((1, D_MODEL, 2 * D_FF), lambda i, be, nu: (be[i], 0, 0)),
            pl.BlockSpec((1, 1, 2 * D_FF), lambda i, be, nu: (be[i], 0, 0)),
            pl.BlockSpec((1, D_FF, D_MODEL), lambda i, be, nu: (be[i], 0, 0)),
            pl.BlockSpec((1, 1, D_MODEL), lambda i, be, nu: (be[i], 0, 0)),
        ],
        out_specs=pl.BlockSpec((tb, PACKED), lambda i, be, nu: (i, 0)),
        scratch_shapes=[pltpu.VMEM((D_MODEL, 2 * D_FF), BF16), pltpu.VMEM((D_FF, D_MODEL), BF16)],
    )
    return pl.pallas_call(
        _expert_kernel,
        grid_spec=grid_spec,
        out_shape=jax.ShapeDtypeStruct((n_slots, PACKED), I32),
        compiler_params=pltpu.CompilerParams(dimension_semantics=("arbitrary",), vmem_limit_bytes=VMEM_LIMIT),
        name="experts",
    )(blk_e, nused, xs, wu, bu, wd, bd)


def _final_kernel(h_ref, yg_ref, rg_ref, g_ref, o_ref):
    rg = rg_ref[...]
    lo = jnp.zeros((FINAL_TILE, PACKED), F32)
    hi = jnp.zeros((FINAL_TILE, PACKED), F32)
    for k in range(TOP_K):
        yl, yh = _unpack_bf16_pair(yg_ref[k])
        w = rg[:, k:k + 1]
        lo = lo + w * yl
        hi = hi + w * yh
    x = h_ref[...] + jnp.concatenate([lo, hi], axis=1)
    o_ref[...] = _rms(x, g_ref[...])


def _final(h2d, yg, rg, g):
    rows = h2d.shape[0]
    tm = FINAL_TILE
    return pl.pallas_call(
        _final_kernel,
        grid=(rows // tm,),
        in_specs=[
            pl.BlockSpec((tm, D_MODEL), lambda i: (i, 0)),
            pl.BlockSpec((TOP_K, tm, PACKED), lambda i: (0, i, 0)),
            pl.BlockSpec((tm, LANES), lambda i: (i, 0)),
            pl.BlockSpec((1, D_MODEL), lambda i: (0, 0)),
        ],
        out_specs=pl.BlockSpec((tm, D_MODEL), lambda i: (i, 0)),
        out_shape=jax.ShapeDtypeStruct((rows, D_MODEL), F32),
        compiler_params=pltpu.CompilerParams(dimension_semantics=("parallel",), vmem_limit_bytes=VMEM_LIMIT),
        name="final",
    )(h2d, yg, rg, g)


def _plan(ri, counts, n_tok):
    tb = EXPERT_BLOCK
    n_slots = n_tok * TOP_K + N_EXPERTS * tb
    eid = ri[:TOP_K]
    rank = ri[TOP_K:]
    c = counts[:, 0]
    pc = (c + tb - 1) // tb * tb
    pend = jnp.cumsum(pc)
    pstart = pend - pc
    base = jnp.zeros_like(eid)
    for e in range(N_EXPERTS):
        base = jnp.where(eid == e, pstart[e], base)
    slots = (base + rank).astype(I32)
    slots = slots.reshape(TOP_K, n_tok // SC_WINDOW, SC_WINDOW).transpose(1, 0, 2)
    blk_start = jnp.arange(n_slots // tb, dtype=I32) * tb
    blk_e = jnp.minimum(jnp.sum(blk_start[:, None] >= pend[None, :], axis=1), N_EXPERTS - 1).astype(I32)
    nused = (pend[-1:] // tb).astype(I32)
    return slots, blk_e, nused, n_slots


def _trunk(x, mem, p):
    batch, seq, _ = x.shape
    n_tok = batch * seq
    x2d = x.reshape(n_tok, D_MODEL)
    kv = _memkv(mem.reshape(-1, D_MODEL), p["norm_mem_g"], p["w_mem_kv"])
    kv3 = kv.reshape(batch, mem.shape[1], 2 * CROSS_WIDTH)
    u, a0, a1, a2, qc, gate = _proj(x2d, p["norm_mix_g"], p["w_in"], _rope_tables(seq), seq)
    outs, lses = [], []
    for g, a_grp in enumerate((a0, a1, a2)):
        o, lse = _dilattn(a_grp, batch, seq, g)
        outs.append(o)
        lses.append(lse)
    h = _mix(x2d, u, qc, gate, outs, lses, kv3, p["w_pool"], p["pool_scale"], p["w_br_pool"], p["w_br_attn"],
             p["w_br_cross"], p["w_out"], batch, seq)
    hn, ri, rg, counts = _route(h, p["norm_ffn_g"], p["w_router_hi"], p["w_router_lo"], p["b_router"])
    slots, blk_e, nused, n_slots = _plan(ri, counts, n_tok)
    xs = _sc_dispatch(hn, slots, n_slots)
    ys = _experts(xs, blk_e, nused, p["w_up"], p["b_up"], p["w_down"], p["b_down"])
    yg = _sc_combine(ys, slots)
    out = _final(h, yg, rg, p["norm_final_g"])
    return out.reshape(batch, seq, D_MODEL)


def _prep_params(norm_mix_g, norm_mem_g, w_in, w_pool, pool_scale, w_mem_kv, w_br_pool, w_br_attn, w_br_cross,
                 w_out, norm_ffn_g, w_router, b_router, w_up, b_up, w_down, b_down, norm_final_g):
    wr = w_router[0]
    wr_hi = wr.astype(BF16)
    return dict(
        norm_mix_g=norm_mix_g[0].reshape(1, D_MODEL),
        norm_mem_g=norm_mem_g[0].reshape(1, D_MODEL),
        w_in=w_in[0].astype(BF16),
        w_pool=w_pool[0].astype(BF16),
        pool_scale=pool_scale[0].reshape(1, POOL_WIDTH),
        w_mem_kv=w_mem_kv[0].astype(BF16),
        w_br_pool=w_br_pool[0].astype(BF16),
        w_br_attn=w_br_attn[0].astype(BF16),
        w_br_cross=w_br_cross[0].astype(BF16),
        w_out=w_out[0].astype(BF16),
        norm_ffn_g=norm_ffn_g[0].reshape(1, D_MODEL),
        w_router_hi=jnp.pad(wr_hi, ((0, 0), (0, LANES - N_EXPERTS))),
        w_router_lo=jnp.pad((wr - wr_hi.astype(F32)).astype(BF16), ((0, 0), (0, LANES - N_EXPERTS))),
        b_router=jnp.pad(b_router[0].reshape(1, N_EXPERTS), ((0, 0), (0, LANES - N_EXPERTS))),
        w_up=w_up[0],
        b_up=b_up[0].reshape(N_EXPERTS, 1, 2 * D_FF),
        w_down=w_down[0],
        b_down=b_down[0].reshape(N_EXPERTS, 1, D_MODEL),
        norm_final_g=norm_final_g.reshape(1, D_MODEL),
    )


def kernel(x_prompt, x_sample, mem_prompt, mem_sample, norm_mix_g, norm_mem_g, w_in, w_pool, pool_scale, w_mem_kv,
           w_br_pool, w_br_attn, w_br_cross, w_out, norm_ffn_g, w_router, b_router, w_up, b_up, w_down, b_down,
           norm_final_g):
    p = _prep_params(norm_mix_g, norm_mem_g, w_in, w_pool, pool_scale, w_mem_kv, w_br_pool, w_br_attn, w_br_cross,
                     w_out, norm_ffn_g, w_router, b_router, w_up, b_up, w_down, b_down, norm_final_g)
    y_prompt = _trunk(x_prompt, mem_prompt, p)
    y_sample = _trunk(x_sample, mem_sample, p)
    return (y_prompt, y_sample)
```

```python
import functools
import math

import jax
import jax.numpy as jnp
from jax import lax
from jax.experimental import pallas as pl
from jax.experimental.pallas import tpu as pltpu
from jax.experimental.pallas import tpu_sc as plsc

F32 = jnp.float32
BF16 = jnp.bfloat16
I32 = jnp.int32
U32 = jnp.uint32

D_MODEL = 1024
POOL_GROUPS = 4
POOL_WIDTH = 512
POOL_GROUP_DIM = 128
POOL_WINDOWS = (2, 4, 8, 16)
POOL_HALO = 16
POOL_EDGE = 8
HEAD_DIM = 64
DIL_CONFIGS = ((128, 1), (512, 4), (2048, 16))
HEADS_PER_DIL = 4
ATTN_WIDTH = 768
ATTN_OUT_WIDTH = 256
BAND_BLOCK = 64
ROPE_DIM = 16
ROPE_THETA = 500000.0
CROSS_HEADS = 4
CROSS_HEAD_DIM = 128
CROSS_WIDTH = 512
N_BRANCHES = 3
QKV_WIDTH = 3 * ATTN_WIDTH
IN_WIDTH = POOL_WIDTH + QKV_WIDTH + CROSS_WIDTH + N_BRANCHES * D_MODEL
N_EXPERTS = 32
TOP_K = 4
D_FF = 1024
SWIGLU_LIMIT = 7.0
SWIGLU_ALPHA = 1.702
EPS = 1e-5
NEG_INF = -1e30

LANES = 128
PACKED = D_MODEL // 2
TOKEN_TILE = 512
FINAL_TILE = 1024
PROJ_CHUNK = 512
MIX_CHUNK = 256
FF_CHUNK = 512
CAST_CHUNK = 512
EXPERT_BLOCK = 512
SC_WINDOW = 128
VMEM_LIMIT = 52 * 1024 * 1024


def _rms(x, g):
    r = lax.rsqrt(jnp.mean(x * x, axis=-1, keepdims=True) + EPS)
    return x * r * g


def _pack_bf16_pair(x):
    bits = lax.bitcast_convert_type(x.astype(BF16).astype(F32), U32)
    packed = (bits[:, :PACKED] >> 16) | bits[:, PACKED:]
    return lax.bitcast_convert_type(packed, I32)


def _unpack_bf16_pair(w):
    u = lax.bitcast_convert_type(w, U32)
    lo = lax.bitcast_convert_type(u << 16, F32)
    hi = lax.bitcast_convert_type(u & jnp.uint32(0xFFFF0000), F32)
    return lo, hi


def _memkv_kernel(mem_ref, g_ref, w_ref, o_ref):
    xn = _rms(mem_ref[...], g_ref[...]).astype(BF16)
    o_ref[...] = jnp.dot(xn, w_ref[...], preferred_element_type=F32).astype(BF16)


def _memkv(mem2d, g, w_bf):
    rows = mem2d.shape[0]
    tm = 256
    return pl.pallas_call(
        _memkv_kernel,
        grid=(rows // tm,),
        in_specs=[
            pl.BlockSpec((tm, D_MODEL), lambda i: (i, 0)),
            pl.BlockSpec((1, D_MODEL), lambda i: (0, 0)),
            pl.BlockSpec((D_MODEL, 2 * CROSS_WIDTH), lambda i: (0, 0)),
        ],
        out_specs=pl.BlockSpec((tm, 2 * CROSS_WIDTH), lambda i: (i, 0)),
        out_shape=jax.ShapeDtypeStruct((rows, 2 * CROSS_WIDTH), BF16),
        compiler_params=pltpu.CompilerParams(dimension_semantics=("parallel",)),
        name="memkv",
    )(mem2d, g, w_bf)


_O_QKV = POOL_WIDTH
_O_QC = _O_QKV + QKV_WIDTH
_O_GATE = _O_QC + CROSS_WIDTH


def _proj_kernel(x_ref, g_ref, w_ref, tab_ref, u_ref, a0_ref, a1_ref, a2_ref, qc_ref, gate_ref, xcols):
    tm = TOKEN_TILE
    n_cols = D_MODEL // LANES
    g = g_ref[...]
    xn = _rms(x_ref[...], g).astype(BF16)
    for c in range(n_cols):
        xcols[c] = x_ref[:, c * LANES:(c + 1) * LANES]

    def mm(lhs, c0, width=PROJ_CHUNK):
        return jnp.dot(lhs, w_ref[:, c0:c0 + width], preferred_element_type=F32)

    for j in range(POOL_WIDTH // PROJ_CHUNK):
        u_ref[:, j * PROJ_CHUNK:(j + 1) * PROJ_CHUNK] = mm(xn, j * PROJ_CHUNK)

    for grp, a_ref in enumerate((a0_ref, a1_ref, a2_ref)):
        dil = DIL_CONFIGS[grp][1]
        res_rows = tm // dil
        if dil == 1:
            lhs = xn
        else:
            xp = jnp.concatenate(
                [jnp.concatenate([xcols[c, pl.ds(r, res_rows, stride=dil), :] for c in range(n_cols)], axis=1)
                 for r in range(dil)], axis=0)
            lhs = _rms(xp, g).astype(BF16)
        cos, s1, s2 = tab_ref[0, grp], tab_ref[1, grp], tab_ref[2, grp]
        for which in range(3):
            t = mm(lhs, _O_QKV + which * ATTN_WIDTH + grp * ATTN_OUT_WIDTH, ATTN_OUT_WIDTH)
            if which < 2:
                halves = []
                for hh in range(ATTN_OUT_WIDTH // LANES):
                    th = t[:, hh * LANES:(hh + 1) * LANES]
                    th = (th * cos + pltpu.roll(th, LANES - ROPE_DIM // 2, axis=1) * s1
                          + pltpu.roll(th, ROPE_DIM // 2, axis=1) * s2)
                    halves.append(th)
                t = jnp.concatenate(halves, axis=1)
                if which == 0:
                    t = t * (1.0 / math.sqrt(HEAD_DIM))
            tb = t.astype(BF16)
            for r in range(dil):
                c0 = (r * 3 + which) * ATTN_OUT_WIDTH
                a_ref[:, c0:c0 + ATTN_OUT_WIDTH] = tb[r * res_rows:(r + 1) * res_rows, :]

    for j in range(CROSS_WIDTH // PROJ_CHUNK):
        qc_ref[:, j * PROJ_CHUNK:(j + 1) * PROJ_CHUNK] = mm(xn, _O_QC + j * PROJ_CHUNK).astype(BF16)

    for j in range(N_BRANCHES * D_MODEL // PROJ_CHUNK):
        gate_ref[:, j * PROJ_CHUNK:(j + 1) * PROJ_CHUNK] = jax.nn.sigmoid(mm(xn, _O_GATE + j * PROJ_CHUNK)).astype(BF16)


def _proj(x2d, g, w_in_bf, tabs, seq):
    rows = x2d.shape[0]
    tm = TOKEN_TILE
    tiles_per_seq = seq // tm
    n_grp = len(DIL_CONFIGS)
    a_specs = [pl.BlockSpec((tm // d, d * ATTN_WIDTH), lambda i: (i, 0)) for _, d in DIL_CONFIGS]
    a_shapes = [jax.ShapeDtypeStruct((rows // d, d * ATTN_WIDTH), BF16) for _, d in DIL_CONFIGS]
    return pl.pallas_call(
        _proj_kernel,
        grid=(rows // tm,),
        in_specs=[
            pl.BlockSpec((tm, D_MODEL), lambda i: (i, 0)),
            pl.BlockSpec((1, D_MODEL), lambda i: (0, 0)),
            pl.BlockSpec((D_MODEL, IN_WIDTH), lambda i: (0, 0), pipeline_mode=pl.Buffered(1)),
            pl.BlockSpec((3, n_grp, tm, LANES), lambda i: (0, 0, i % tiles_per_seq, 0)),
        ],
        out_specs=[
            pl.BlockSpec((tm, POOL_WIDTH), lambda i: (i, 0)),
            *a_specs,
            pl.BlockSpec((tm, CROSS_WIDTH), lambda i: (i, 0)),
            pl.BlockSpec((tm, N_BRANCHES * D_MODEL), lambda i: (i, 0)),
        ],
        out_shape=[
            jax.ShapeDtypeStruct((rows, POOL_WIDTH), F32),
            *a_shapes,
            jax.ShapeDtypeStruct((rows, CROSS_WIDTH), BF16),
            jax.ShapeDtypeStruct((rows, N_BRANCHES * D_MODEL), BF16),
        ],
        scratch_shapes=[pltpu.VMEM((D_MODEL // LANES, tm, LANES), F32)],
        compiler_params=pltpu.CompilerParams(dimension_semantics=("parallel",), vmem_limit_bytes=VMEM_LIMIT),
        name="proj",
    )(x2d, g, w_in_bf, tabs)


def _rope_tables(seq):
    pos = jnp.arange(seq, dtype=F32)
    inv = jnp.power(jnp.float32(ROPE_THETA), -jnp.arange(0, ROPE_DIM, 2, dtype=F32) / ROPE_DIM)
    ang = pos[:, None] * inv[None, :]
    half = ROPE_DIM // 2
    j = jnp.arange(LANES) % HEAD_DIM
    cos_l = jnp.cos(ang)[:, j % half]
    sin_l = jnp.sin(ang)[:, j % half]
    kinds = jnp.stack([
        jnp.where(j[None, :] < ROPE_DIM, cos_l, 1.0),
        jnp.where(j[None, :] < half, -sin_l, 0.0),
        jnp.where((j[None, :] >= half) & (j[None, :] < ROPE_DIM), sin_l, 0.0),
    ]).astype(F32)
    tm = TOKEN_TILE
    per_group = []
    for _, d in DIL_CONFIGS:
        t = kinds.reshape(3, seq // tm, tm // d, d, LANES).transpose(0, 1, 3, 2, 4).reshape(3, seq, LANES)
        per_group.append(t)
    return jnp.stack(per_group, axis=1)


def _dilattn_kernel(a_ref, bias_ref, o_ref, lse_ref, *, res_len, tq, win, n_res):
    t = pl.program_id(2)
    nblk = tq // BAND_BLOCK
    stacked = HEADS_PER_DIL * BAND_BLOCK
    lane_head = lax.broadcasted_iota(I32, (BAND_BLOCK, ATTN_OUT_WIDTH), 1) // HEAD_DIM
    head_sel = [lane_head == h for h in range(HEADS_PER_DIL)]
    head_mask = [jnp.where(sel, 1.0, 0.0).astype(BF16) for sel in head_sel]
    ones = jnp.ones((win, LANES), BF16)
    blocks = [(rr, n) for rr in range(n_res) for n in range(nblk)]

    s_parts, v_wins = [], []
    for rr, n in blocks:
        c0 = rr * ATTN_WIDTH
        qpos0 = pl.multiple_of(t * tq + n * BAND_BLOCK, BAND_BLOCK)
        start = pl.multiple_of(jnp.clip(qpos0 - BAND_BLOCK, 0, res_len - win), BAND_BLOCK)
        qb = a_ref[0, pl.ds(qpos0, BAND_BLOCK), c0:c0 + ATTN_OUT_WIDTH]
        kw = a_ref[0, pl.ds(start, win), c0 + ATTN_OUT_WIDTH:c0 + 2 * ATTN_OUT_WIDTH]
        v_wins.append(a_ref[0, pl.ds(start, win), c0 + 2 * ATTN_OUT_WIDTH:c0 + 3 * ATTN_OUT_WIDTH])
        qs = jnp.concatenate([qb * hm for hm in head_mask], axis=0)
        s = lax.dot_general(qs, kw, (((1,), (1,)), ((), ())), preferred_element_type=F32)
        s_parts.append(s + bias_ref[(qpos0 - start) // BAND_BLOCK])
    s_all = jnp.concatenate(s_parts, axis=0)
    m_all = jnp.max(s_all, axis=-1, keepdims=True)
    p_all = jnp.exp(s_all - m_all).astype(BF16)

    for i, (rr, n) in enumerate(blocks):
        p = p_all[i * stacked:(i + 1) * stacked]
        o_st = jnp.dot(p, v_wins[i], preferred_element_type=F32)
        l_st = jnp.dot(p, ones, preferred_element_type=F32)
        m_st = m_all[i * stacked:(i + 1) * stacked]
        o = jnp.zeros((BAND_BLOCK, ATTN_OUT_WIDTH), F32)
        l = jnp.ones((BAND_BLOCK, ATTN_OUT_WIDTH), F32)
        m = jnp.zeros((BAND_BLOCK, ATTN_OUT_WIDTH), F32)
        for h in range(HEADS_PER_DIL):
            rs = slice(h * BAND_BLOCK, (h + 1) * BAND_BLOCK)
            o = jnp.where(head_sel[h], o_st[rs], o)
            l = jnp.where(head_sel[h], jnp.concatenate([l_st[rs], l_st[rs]], axis=1), l)
            m = jnp.where(head_sel[h], m_st[rs], m)
        rows = slice(n * BAND_BLOCK, (n + 1) * BAND_BLOCK)
        cols = slice(rr * ATTN_OUT_WIDTH, (rr + 1) * ATTN_OUT_WIDTH)
        o_ref[0, rows, cols] = (o / l).astype(BF16)
        lse_ref[0, rows, cols] = m + jnp.log(l)


ATTN_BLOCKS_PER_STEP = 16


def _dilattn(a_grp, batch, seq, group):
    window, dil = DIL_CONFIGS[group]
    half = window // (2 * dil)
    res_len = seq // dil
    win = min(3 * BAND_BLOCK, res_len)
    n_res = min(dil, 4)
    tq = min(res_len, ATTN_BLOCKS_PER_STEP // n_res * BAND_BLOCK)
    qkv3 = a_grp.reshape(batch, res_len, dil * ATTN_WIDTH)
    kern = functools.partial(_dilattn_kernel, res_len=res_len, tq=tq, win=win, n_res=n_res)
    out_spec = pl.BlockSpec((1, tq, n_res * ATTN_OUT_WIDTH), lambda b, r, t: (b, t, r))
    n_off = win // BAND_BLOCK
    stacked = HEADS_PER_DIL * BAND_BLOCK
    key_minus_query = (jnp.arange(win)[None, None, :] - (jnp.arange(stacked) % BAND_BLOCK)[None, :, None]
                       - BAND_BLOCK * jnp.arange(n_off)[:, None, None])
    bias = jnp.where(jnp.abs(key_minus_query) <= half, 0.0, NEG_INF).astype(F32)
    o, lse = pl.pallas_call(
        kern,
        grid=(batch, dil // n_res, res_len // tq),
        in_specs=[pl.BlockSpec((1, res_len, n_res * ATTN_WIDTH), lambda b, r, t: (b, 0, r)),
                  pl.BlockSpec((n_off, stacked, win), lambda b, r, t: (0, 0, 0))],
        out_specs=[out_spec, out_spec],
        out_shape=[
            jax.ShapeDtypeStruct((batch, res_len, dil * ATTN_OUT_WIDTH), BF16),
            jax.ShapeDtypeStruct((batch, res_len, dil * ATTN_OUT_WIDTH), F32),
        ],
        compiler_params=pltpu.CompilerParams(dimension_semantics=("parallel", "parallel", "parallel"),
                                             vmem_limit_bytes=VMEM_LIMIT),
        name=f"dilattn{group}",
    )(qkv3, bias)
    return (o.reshape(batch * res_len, dil * ATTN_OUT_WIDTH), lse.reshape(batch * res_len, dil * ATTN_OUT_WIDTH))


def _mix_kernel(x_ref, u_ref, up_ref, un_ref, qc_ref, gate_ref, o0_ref, o1_ref, o2_ref, l0_ref, l1_ref, l2_ref,
                kv_ref, wpool_ref, pscale_ref, wbp_ref, wba_ref, wbc_ref, wout_ref, h_ref, ubuf, merged, relay, runs,
                *, seq):
    tm = TOKEN_TILE
    i = pl.program_id(1)
    nt = pl.num_programs(1)

    ubuf[0:POOL_HALO, :] = jnp.where(i > 0, up_ref[...], 0.0)
    ubuf[POOL_HALO:POOL_HALO + tm, :] = u_ref[...]
    ubuf[POOL_HALO + tm:, :] = jnp.where(i < nt - 1, un_ref[...], 0.0)
    edge_row = lax.broadcasted_iota(I32, (POOL_EDGE, POOL_GROUP_DIM), 0)
    pos_top = i * tm + edge_row
    pos_bot = i * tm + (tm - POOL_EDGE) + edge_row
    pool_parts = []
    for g, w in enumerate(POOL_WINDOWS):
        cs = slice(g * POOL_GROUP_DIM, (g + 1) * POOL_GROUP_DIM)
        half_w = w // 2
        need = {half_w: tm + POOL_HALO}
        m = half_w
        while m > 1:
            need[m // 2] = need[m] + m // 2
            m //= 2
        src, m = None, 1
        while m < half_w:
            n = need[2 * m]
            dst = runs.at[(m.bit_length() - 1) % 2]
            if src is None:
                dst[0:n, :] = ubuf[0:n, cs] + ubuf[m:m + n, cs]
            else:
                dst[0:n, :] = src[0:n, :] + src[m:m + n, :]
            src, m = dst, 2 * m
        lo = POOL_HALO - half_w
        if src is None:
            acc = ubuf[lo:lo + tm, cs] + ubuf[POOL_HALO:POOL_HALO + tm, cs]
        else:
            acc = src[lo:lo + tm, :] + src[POOL_HALO:POOL_HALO + tm, :]
        cnt_top = (jnp.minimum(pos_top + w // 2, seq) - jnp.maximum(pos_top - w // 2, 0)).astype(F32)
        cnt_bot = (jnp.minimum(pos_bot + w // 2, seq) - jnp.maximum(pos_bot - w // 2, 0)).astype(F32)
        mean = jnp.concatenate([acc[:POOL_EDGE] / cnt_top,
                                acc[POOL_EDGE:tm - POOL_EDGE] * (1.0 / w),
                                acc[tm - POOL_EDGE:] / cnt_bot], axis=0)
        z = mean - ubuf[POOL_HALO:POOL_HALO + tm, cs]
        zp = jnp.dot(z.astype(BF16), wpool_ref[g], preferred_element_type=F32)
        pool_parts.append(zp * pscale_ref[:, cs])
    pool_bf = jnp.concatenate(pool_parts, axis=1).astype(BF16)

    cross_parts = []
    for h in range(CROSS_HEADS):
        cs = slice(h * CROSS_HEAD_DIM, (h + 1) * CROSS_HEAD_DIM)
        kh = kv_ref[0, :, cs]
        vh = kv_ref[0, :, CROSS_WIDTH + h * CROSS_HEAD_DIM:CROSS_WIDTH + (h + 1) * CROSS_HEAD_DIM]
        s = lax.dot_general(qc_ref[:, cs], kh, (((1,), (1,)), ((), ())), preferred_element_type=F32)
        s = s * (1.0 / math.sqrt(CROSS_HEAD_DIM))
        m = jnp.max(s, axis=-1, keepdims=True)
        p = jnp.exp(s - m)
        l = jnp.sum(p, axis=-1, keepdims=True)
        cross_parts.append(jnp.dot(p.astype(BF16), vh, preferred_element_type=F32) / l)
    cross_bf = jnp.concatenate(cross_parts, axis=1).astype(BF16)

    def token_major(ref, slot, dil):
        if dil == 1:
            return ref[...].astype(F32)
        res_rows = tm // dil
        for r in range(dil):
            for hh in range(ATTN_OUT_WIDTH // LANES):
                c0 = r * ATTN_OUT_WIDTH + hh * LANES
                relay[slot, hh, pl.ds(r, res_rows, stride=dil), :] = ref[:, c0:c0 + LANES].astype(F32)
        return jnp.concatenate([relay[slot, hh] for hh in range(ATTN_OUT_WIDTH // LANES)], axis=1)

    dils = [d for _, d in DIL_CONFIGS]
    o0, o1, o2 = [token_major(r, s, d) for r, s, d in zip((o0_ref, o1_ref, o2_ref), (0, 1, 2), dils)]
    l0, l1, l2 = [token_major(r, s, d) for r, s, d in zip((l0_ref, l1_ref, l2_ref), (3, 4, 5), dils)]
    mx = jnp.maximum(jnp.maximum(l0, l1), l2)
    e0, e1, e2 = jnp.exp(l0 - mx), jnp.exp(l1 - mx), jnp.exp(l2 - mx)
    attn = (e0 * o0 + e1 * o1 + e2 * o2) / (e0 + e1 + e2)
    attn_bf = attn.astype(BF16)

    for c in range(D_MODEL // MIX_CHUNK):
        cs = slice(c * MIX_CHUNK, (c + 1) * MIX_CHUNK)
        mrg = gate_ref[:, cs].astype(F32) * jnp.dot(pool_bf, wbp_ref[:, cs], preferred_element_type=F32)
        mrg = mrg + gate_ref[:, D_MODEL + c * MIX_CHUNK:D_MODEL + (c + 1) * MIX_CHUNK].astype(F32) * jnp.dot(
            attn_bf, wba_ref[:, cs], preferred_element_type=F32)
        mrg = mrg + gate_ref[:, 2 * D_MODEL + c * MIX_CHUNK:2 * D_MODEL + (c + 1) * MIX_CHUNK].astype(F32) * jnp.dot(
            cross_bf, wbc_ref[:, cs], preferred_element_type=F32)
        merged[:, cs] = mrg.astype(BF16)
    h_ref[...] = x_ref[...] + jnp.dot(merged[...], wout_ref[...], preferred_element_type=F32)


def _mix(x2d, u, qc, gate, outs, lses, kv3, wpool_bf, pscale, wbp, wba, wbc, wout, batch, seq):
    tm = TOKEN_TILE
    ts = seq // tm
    rows = batch * seq
    hb = tm // POOL_HALO
    n_halo = rows // POOL_HALO

    def row(b, i):
        return (b * ts + i, 0)

    def const(b, i):
        return (0, 0)

    tok = lambda w: pl.BlockSpec((tm, w), row)
    in_specs = [
        tok(D_MODEL),
        tok(POOL_WIDTH),
        pl.BlockSpec((POOL_HALO, POOL_WIDTH), lambda b, i: (jnp.maximum((b * ts + i) * hb - 1, 0), 0)),
        pl.BlockSpec((POOL_HALO, POOL_WIDTH), lambda b, i: (jnp.minimum((b * ts + i + 1) * hb, n_halo - 1), 0)),
        tok(CROSS_WIDTH),
        tok(N_BRANCHES * D_MODEL),
        *[pl.BlockSpec((tm // d, d * ATTN_OUT_WIDTH), row) for _, d in DIL_CONFIGS],
        *[pl.BlockSpec((tm // d, d * ATTN_OUT_WIDTH), row) for _, d in DIL_CONFIGS],
        pl.BlockSpec((1, kv3.shape[1], 2 * CROSS_WIDTH), lambda b, i: (b, 0, 0)),
        pl.BlockSpec((POOL_GROUPS, POOL_GROUP_DIM, POOL_GROUP_DIM), lambda b, i: (0, 0, 0)),
        pl.BlockSpec((1, POOL_WIDTH), const),
        pl.BlockSpec((POOL_WIDTH, D_MODEL), const),
        pl.BlockSpec((ATTN_OUT_WIDTH, D_MODEL), const),
        pl.BlockSpec((CROSS_WIDTH, D_MODEL), const),
        pl.BlockSpec((D_MODEL, D_MODEL), const),
    ]
    return pl.pallas_call(
        functools.partial(_mix_kernel, seq=seq),
        grid=(batch, ts),
        in_specs=in_specs,
        out_specs=pl.BlockSpec((tm, D_MODEL), row),
        out_shape=jax.ShapeDtypeStruct((rows, D_MODEL), F32),
        scratch_shapes=[pltpu.VMEM((tm + 2 * POOL_HALO, POOL_WIDTH), F32), pltpu.VMEM((tm, D_MODEL), BF16),
                        pltpu.VMEM((2 * len(DIL_CONFIGS), ATTN_OUT_WIDTH // LANES, tm, LANES), F32),
                        pltpu.VMEM((2, tm + 2 * POOL_HALO, POOL_GROUP_DIM), F32)],
        compiler_params=pltpu.CompilerParams(dimension_semantics=("parallel", "parallel"), vmem_limit_bytes=VMEM_LIMIT),
        name="mix",
    )(x2d, u, u, u, qc, gate, *outs, *lses, kv3, wpool_bf, pscale, wbp, wba, wbc, wout)


def _route_kernel(h_ref, g_ref, whi_ref, wlo_ref, b_ref, hn_ref, ri_ref, rg_ref, cnt_ref, carry, tri):
    tm = TOKEN_TILE
    step = pl.program_id(0)

    @pl.when(step == 0)
    def _():
        carry[...] = jnp.zeros_like(carry)
        r = lax.broadcasted_iota(I32, (tm, tm), 0)
        c = lax.broadcasted_iota(I32, (tm, tm), 1)
        tri[...] = jnp.where(r < c, 1.0, 0.0).astype(BF16)

    hn = _rms(h_ref[...], g_ref[...])
    hn_ref[...] = _pack_bf16_pair(hn)
    hi = hn.astype(BF16)
    lo = (hn - hi.astype(F32)).astype(BF16)
    logits = (jnp.dot(hi, whi_ref[...], preferred_element_type=F32)
              + jnp.dot(lo, whi_ref[...], preferred_element_type=F32)
              + jnp.dot(hi, wlo_ref[...], preferred_element_type=F32)) + b_ref[...]
    work = logits.T[:N_EXPERTS]
    row = lax.broadcasted_iota(I32, (N_EXPERTS, tm), 0).astype(F32)
    idxs, vals = [], []
    onehot = jnp.zeros((N_EXPERTS, tm), F32)
    for _ in range(TOP_K):
        m = jnp.max(work, axis=0, keepdims=True)
        idx = jnp.min(jnp.where(work == m, row, float(N_EXPERTS)), axis=0, keepdims=True)
        sel = row == idx
        onehot = jnp.where(sel, 1.0, onehot)
        work = jnp.where(sel, -jnp.inf, work)
        idxs.append(idx)
        vals.append(m)
    exps = [jnp.exp(v - vals[0]) for v in vals]
    den = exps[0] + exps[1] + exps[2] + exps[3]
    gates = [e / den for e in exps]

    prefix = jnp.dot(onehot.astype(BF16), tri[...], preferred_element_type=F32) + carry[:, 0:1]
    ranks = [jnp.sum(jnp.where(row == idx, prefix, 0.0), axis=0, keepdims=True) for idx in idxs]
    carry[...] = carry[...] + jnp.sum(onehot, axis=1, keepdims=True)

    row8 = lax.broadcasted_iota(I32, (2 * TOP_K, tm), 0)
    ri = jnp.zeros((2 * TOP_K, tm), F32)
    row128 = lax.broadcasted_iota(I32, (LANES, tm), 0)
    rg = jnp.zeros((LANES, tm), F32)
    for k in range(TOP_K):
        ri = jnp.where(row8 == k, idxs[k], ri)
        ri = jnp.where(row8 == TOP_K + k, ranks[k], ri)
        rg = jnp.where(row128 == k, gates[k], rg)
    ri_ref[...] = ri.astype(I32)
    rg_ref[...] = rg.T
    cnt_ref[...] = carry[...].astype(I32)


def _route(h2d, g, whi, wlo, b):
    rows = h2d.shape[0]
    tm = TOKEN_TILE
    const = lambda i: (0, 0)
    return pl.pallas_call(
        _route_kernel,
        grid=(rows // tm,),
        in_specs=[
            pl.BlockSpec((tm, D_MODEL), lambda i: (i, 0)),
            pl.BlockSpec((1, D_MODEL), const),
            pl.BlockSpec((D_MODEL, LANES), const),
            pl.BlockSpec((D_MODEL, LANES), const),
            pl.BlockSpec((1, LANES), const),
        ],
        out_specs=[
            pl.BlockSpec((tm, PACKED), lambda i: (i, 0)),
            pl.BlockSpec((2 * TOP_K, tm), lambda i: (0, i)),
            pl.BlockSpec((tm, LANES), lambda i: (i, 0)),
            pl.BlockSpec((N_EXPERTS, LANES), const),
        ],
        out_shape=[
            jax.ShapeDtypeStruct((rows, PACKED), I32),
            jax.ShapeDtypeStruct((2 * TOP_K, rows), I32),
            jax.ShapeDtypeStruct((rows, LANES), F32),
            jax.ShapeDtypeStruct((N_EXPERTS, LANES), I32),
        ],
        scratch_shapes=[pltpu.VMEM((N_EXPERTS, LANES), F32), pltpu.VMEM((tm, tm), BF16)],
        compiler_params=pltpu.CompilerParams(dimension_semantics=("arbitrary",), vmem_limit_bytes=VMEM_LIMIT),
        name="route",
    )(h2d, g, whi, wlo, b)


def _sc_mesh():
    return plsc.VectorSubcoreMesh(core_axis_name="c", subcore_axis_name="s")


def _sc_workers():
    info = plsc.get_sparse_core_info()
    return info.num_cores, info.num_cores * info.num_subcores


def _sc_dispatch(rows_packed, slots, n_slots):
    n_tok = rows_packed.shape[0]
    n_cores, n_workers = _sc_workers()
    chunks_per_worker = n_tok // SC_WINDOW // n_workers

    @functools.partial(pl.kernel, out_type=jax.ShapeDtypeStruct((n_slots, PACKED), I32), mesh=_sc_mesh(),
                       scratch_types=[pltpu.VMEM((TOP_K, SC_WINDOW), I32), pltpu.VMEM((SC_WINDOW, PACKED), I32)],
                       name="dispatch")
    def kern(x_hbm, i_hbm, o_hbm, idx_v, rows_v):
        wid = lax.axis_index("s") * n_cores + lax.axis_index("c")

        @pl.loop(0, chunks_per_worker)
        def _(j):
            chunk = wid * chunks_per_worker + j
            pltpu.sync_copy(i_hbm.at[chunk], idx_v)
            pltpu.sync_copy(x_hbm.at[pl.ds(chunk * SC_WINDOW, SC_WINDOW)], rows_v)
            for k in range(TOP_K):
                pltpu.sync_copy(rows_v, o_hbm.at[idx_v.at[k]])

    return kern(rows_packed, slots)


def _sc_combine(ys, slots):
    n_tok = slots.shape[0] * SC_WINDOW
    n_cores, n_workers = _sc_workers()
    chunks_per_worker = n_tok // SC_WINDOW // n_workers

    @functools.partial(pl.kernel, out_type=jax.ShapeDtypeStruct((TOP_K, n_tok, PACKED), I32), mesh=_sc_mesh(),
                       scratch_types=[pltpu.VMEM((TOP_K, SC_WINDOW), I32), pltpu.VMEM((SC_WINDOW, PACKED), I32)],
                       name="combine")
    def kern(y_hbm, i_hbm, o_hbm, idx_v, rows_v):
        wid = lax.axis_index("s") * n_cores + lax.axis_index("c")

        @pl.loop(0, chunks_per_worker)
        def _(j):
            chunk = wid * chunks_per_worker + j
            pltpu.sync_copy(i_hbm.at[chunk], idx_v)
            for k in range(TOP_K):
                pltpu.sync_copy(y_hbm.at[idx_v.at[k]], rows_v)
                pltpu.sync_copy(rows_v, o_hbm.at[k, pl.ds(chunk * SC_WINDOW, SC_WINDOW)])

    return kern(ys, slots)


def _expert_kernel(blk_e_ref, nused_ref, x_ref, wu_ref, bu_ref, wd_ref, bd_ref, o_ref, wu_bf, wd_bf):
    i = pl.program_id(0)
    used = i < nused_ref[0]

    @pl.when(used & ((i == 0) | (blk_e_ref[i] != blk_e_ref[jnp.maximum(i - 1, 0)])))
    def _():
        for c in range(2 * D_FF // CAST_CHUNK):
            cs = slice(c * CAST_CHUNK, (c + 1) * CAST_CHUNK)
            wu_bf[:, cs] = wu_ref[0, :, cs].astype(BF16)
        for c in range(D_MODEL // CAST_CHUNK):
            cs = slice(c * CAST_CHUNK, (c + 1) * CAST_CHUNK)
            wd_bf[:, cs] = wd_ref[0, :, cs].astype(BF16)

    @pl.when(used)
    def _():
        lo, hi = _unpack_bf16_pair(x_ref[...])
        x = jnp.concatenate([lo.astype(BF16), hi.astype(BF16)], axis=1)
        y = jnp.zeros((EXPERT_BLOCK, D_MODEL), F32)
        for c in range(D_FF // FF_CHUNK):
            gs = slice(c * FF_CHUNK, (c + 1) * FF_CHUNK)
            us = slice(D_FF + c * FF_CHUNK, D_FF + (c + 1) * FF_CHUNK)
            gate = jnp.dot(x, wu_bf[:, gs], preferred_element_type=F32) + bu_ref[0, :, gs]
            up = jnp.dot(x, wu_bf[:, us], preferred_element_type=F32) + bu_ref[0, :, us]
            gate = jnp.minimum(gate, SWIGLU_LIMIT)
            up = jnp.clip(up, -SWIGLU_LIMIT, SWIGLU_LIMIT)
            act = (up + 1.0) * (gate * jax.nn.sigmoid(SWIGLU_ALPHA * gate))
            y = y + jnp.dot(act.astype(BF16), wd_bf[gs, :], preferred_element_type=F32)
        o_ref[...] = _pack_bf16_pair(y + bd_ref[0])

    @pl.when(jnp.logical_not(used))
    def _():
        o_ref[...] = jnp.zeros_like(o_ref)


def _experts(xs, blk_e, nused, wu, bu, wd, bd):
    n_slots = xs.shape[0]
    tb = EXPERT_BLOCK
    grid_spec = pltpu.PrefetchScalarGridSpec(
        num_scalar_prefetch=2,
        grid=(n_slots // tb,),
        in_specs=[
            pl.BlockSpec((tb, PACKED), lambda i, be, nu: (i, 0)),
            pl.BlockSpec((1, D_MODEL, 2 * D_FF), lambda i, be, nu: (be[i], 0, 0)),
            pl.BlockSpec((1, 1, 2 * D_FF), lambda i, be, nu: (be[i], 0, 0)),
            pl.BlockSpec((1, D_FF, D_MODEL), lambda i, be, nu: (be[i], 0, 0)),
            pl.BlockSpec((1, 1, D_MODEL), lambda i, be, nu: (be[i], 0, 0)),
        ],
        out_specs=pl.BlockSpec((tb, PACKED), lambda i, be, nu: (i, 0)),
        scratch_shapes=[pltpu.VMEM((D_MODEL, 2 * D_FF), BF16), pltpu.VMEM((D_FF, D_MODEL), BF16)],
    )
    return pl.pallas_call(
        _expert_kernel,
        grid_spec=grid_spec,
        out_shape=jax.ShapeDtypeStruct((n_slots, PACKED), I32),
        compiler_params=pltpu.CompilerParams(dimension_semantics=("arbitrary",), vmem_limit_bytes=VMEM_LIMIT),
        name="experts",
    )(blk_e, nused, xs, wu, bu, wd, bd)


def _final_kernel(h_ref, yg_ref, rg_ref, g_ref, o_ref):
    rg = rg_ref[...]
    lo = jnp.zeros((FINAL_TILE, PACKED), F32)
    hi = jnp.zeros((FINAL_TILE, PACKED), F32)
    for k in range(TOP_K):
        yl, yh = _unpack_bf16_pair(yg_ref[k])
        w = rg[:, k:k + 1]
        lo = lo + w * yl
        hi = hi + w * yh
    x = h_ref[...] + jnp.concatenate([lo, hi], axis=1)
    o_ref[...] = _rms(x, g_ref[...])


def _final(h2d, yg, rg, g):
    rows = h2d.shape[0]
    tm = FINAL_TILE
    return pl.pallas_call(
        _final_kernel,
        grid=(rows // tm,),
        in_specs=[
            pl.BlockSpec((tm, D_MODEL), lambda i: (i, 0)),
            pl.BlockSpec((TOP_K, tm, PACKED), lambda i: (0, i, 0)),
            pl.BlockSpec((tm, LANES), lambda i: (i, 0)),
            pl.BlockSpec((1, D_MODEL), lambda i: (0, 0)),
        ],
        out_specs=pl.BlockSpec((tm, D_MODEL), lambda i: (i, 0)),
        out_shape=jax.ShapeDtypeStruct((rows, D_MODEL), F32),
        compiler_params=pltpu.CompilerParams(dimension_semantics=("parallel",), vmem_limit_bytes=VMEM_LIMIT),
        name="final",
    )(h2d, yg, rg, g)


def _plan(ri, counts, n_tok):
    tb = EXPERT_BLOCK
    n_slots = n_tok * TOP_K + N_EXPERTS * tb
    eid = ri[:TOP_K]
    rank = ri[TOP_K:]
    c = counts[:, 0]
    pc = (c + tb - 1) // tb * tb
    pend = jnp.cumsum(pc)
    pstart = pend - pc
    base = jnp.zeros_like(eid)
    for e in range(N_EXPERTS):
        base = jnp.where(eid == e, pstart[e], base)
    slots = (base + rank).astype(I32)
    slots = slots.reshape(TOP_K, n_tok // SC_WINDOW, SC_WINDOW).transpose(1, 0, 2)
    blk_start = jnp.arange(n_slots // tb, dtype=I32) * tb
    blk_e = jnp.minimum(jnp.sum(blk_start[:, None] >= pend[None, :], axis=1), N_EXPERTS - 1).astype(I32)
    nused = (pend[-1:] // tb).astype(I32)
    return slots, blk_e, nused, n_slots


def _trunk(x, mem, p):
    batch, seq, _ = x.shape
    n_tok = batch * seq
    x2d = x.reshape(n_tok, D_MODEL)
    kv = _memkv(mem.reshape(-1, D_MODEL), p["norm_mem_g"], p["w_mem_kv"])
    kv3 = kv.reshape(batch, mem.shape[1], 2 * CROSS_WIDTH)
    u, a0, a1, a2, qc, gate = _proj(x2d, p["norm_mix_g"], p["w_in"], _rope_tables(seq), seq)
    outs, lses = [], []
    for g, a_grp in enumerate((a0, a1, a2)):
        o, lse = _dilattn(a_grp, batch, seq, g)
        outs.append(o)
        lses.append(lse)
    h = _mix(x2d, u, qc, gate, outs, lses, kv3, p["w_pool"], p["pool_scale"], p["w_br_pool"], p["w_br_attn"],
             p["w_br_cross"], p["w_out"], batch, seq)
    hn, ri, rg, counts = _route(h, p["norm_ffn_g"], p["w_router_hi"], p["w_router_lo"], p["b_router"])
    slots, blk_e, nused, n_slots = _plan(ri, counts, n_tok)
    xs = _sc_dispatch(hn, slots, n_slots)
    ys = _experts(xs, blk_e, nused, p["w_up"], p["b_up"], p["w_down"], p["b_down"])
    yg = _sc_combine(ys, slots)
    out = _final(h, yg, rg, p["norm_final_g"])
    return out.reshape(batch, seq, D_MODEL)


def _prep_params(norm_mix_g, norm_mem_g, w_in, w_pool, pool_scale, w_mem_kv, w_br_pool, w_br_attn, w_br_cross,
                 w_out, norm_ffn_g, w_router, b_router, w_up, b_up, w_down, b_down, norm_final_g):
    wr = w_router[0]
    wr_hi = wr.astype(BF16)
    return dict(
        norm_mix_g=norm_mix_g[0].reshape(1, D_MODEL),
        norm_mem_g=norm_mem_g[0].reshape(1, D_MODEL),
        w_in=w_in[0].astype(BF16),
        w_pool=w_pool[0].astype(BF16),
        pool_scale=pool_scale[0].reshape(1, POOL_WIDTH),
        w_mem_kv=w_mem_kv[0].astype(BF16),
        w_br_pool=w_br_pool[0].astype(BF16),
        w_br_attn=w_br_attn[0].astype(BF16),
        w_br_cross=w_br_cross[0].astype(BF16),
        w_out=w_out[0].astype(BF16),
        norm_ffn_g=norm_ffn_g[0].reshape(1, D_MODEL),
        w_router_hi=jnp.pad(wr_hi, ((0, 0), (0, LANES - N_EXPERTS))),
        w_router_lo=jnp.pad((wr - wr_hi.astype(F32)).astype(BF16), ((0, 0), (0, LANES - N_EXPERTS))),
        b_router=jnp.pad(b_router[0].reshape(1, N_EXPERTS), ((0, 0), (0, LANES - N_EXPERTS))),
        w_up=w_up[0],
        b_up=b_up[0].reshape(N_EXPERTS, 1, 2 * D_FF),
        w_down=w_down[0],
        b_down=b_down[0].reshape(N_EXPERTS, 1, D_MODEL),
        norm_final_g=norm_final_g.reshape(1, D_MODEL),
    )


def kernel(x_prompt, x_sample, mem_prompt, mem_sample, norm_mix_g, norm_mem_g, w_in, w_pool, pool_scale, w_mem_kv,
           w_br_pool, w_br_attn, w_br_cross, w_out, norm_ffn_g, w_router, b_router, w_up, b_up, w_down, b_down,
           norm_final_g):
    p = _prep_params(norm_mix_g, norm_mem_g, w_in, w_pool, pool_scale, w_mem_kv, w_br_pool, w_br_attn, w_br_cross,
                     w_out, norm_ffn_g, w_router, b_router, w_up, b_up, w_down, b_down, norm_final_g)
    y_prompt = _trunk(x_prompt, mem_prompt, p)
    y_sample = _trunk(x_sample, mem_sample, p)
    return (y_prompt, y_sample)
```

```python
import functools
import math

import jax
import jax.numpy as jnp
from jax import lax
from jax.experimental import pallas as pl
from jax.experimental.pallas import tpu as pltpu
from jax.experimental.pallas import tpu_sc as plsc

F32 = jnp.float32
BF16 = jnp.bfloat16
I32 = jnp.int32
U32 = jnp.uint32

D_MODEL = 1024
POOL_GROUPS = 4
POOL_WIDTH = 512
POOL_GROUP_DIM = 128
POOL_WINDOWS = (2, 4, 8, 16)
POOL_HALO = 16
POOL_EDGE = 8
HEAD_DIM = 64
DIL_CONFIGS = ((128, 1), (512, 4), (2048, 16))
HEADS_PER_DIL = 4
ATTN_WIDTH = 768
ATTN_OUT_WIDTH = 256
BAND_BLOCK = 64
ROPE_DIM = 16
ROPE_THETA = 500000.0
CROSS_HEADS = 4
CROSS_HEAD_DIM = 128
CROSS_WIDTH = 512
N_BRANCHES = 3
QKV_WIDTH = 3 * ATTN_WIDTH
IN_WIDTH = POOL_WIDTH + QKV_WIDTH + CROSS_WIDTH + N_BRANCHES * D_MODEL
N_EXPERTS = 32
TOP_K = 4
D_FF = 1024
SWIGLU_LIMIT = 7.0
SWIGLU_ALPHA = 1.702
EPS = 1e-5
NEG_INF = -1e30

LANES = 128
PACKED = D_MODEL // 2
TOKEN_TILE = 512
FINAL_TILE = 1024
ROUTE_TILE = 1024
PROJ_CHUNK = 512
MIX_CHUNK = 256
FF_CHUNK = 512
CAST_CHUNK = 512
EXPERT_BLOCK = 512
SC_WINDOW = 128
VMEM_LIMIT = 52 * 1024 * 1024


def _rms(x, g):
    r = lax.rsqrt(jnp.mean(x * x, axis=-1, keepdims=True) + EPS)
    return x * r * g


def _pack_bf16_pair(x):
    bits = lax.bitcast_convert_type(x.astype(BF16).astype(F32), U32)
    packed = (bits[:, :PACKED] >> 16) | bits[:, PACKED:]
    return lax.bitcast_convert_type(packed, I32)


def _unpack_bf16_pair(w):
    u = lax.bitcast_convert_type(w, U32)
    lo = lax.bitcast_convert_type(u << 16, F32)
    hi = lax.bitcast_convert_type(u & jnp.uint32(0xFFFF0000), F32)
    return lo, hi


def _memkv_kernel(mem_ref, g_ref, w_ref, o_ref):
    xn = _rms(mem_ref[...], g_ref[...]).astype(BF16)
    o_ref[...] = jnp.dot(xn, w_ref[...], preferred_element_type=F32).astype(BF16)


def _memkv(mem2d, g, w_bf):
    rows = mem2d.shape[0]
    tm = 256
    return pl.pallas_call(
        _memkv_kernel,
        grid=(rows // tm,),
        in_specs=[
            pl.BlockSpec((tm, D_MODEL), lambda i: (i, 0)),
            pl.BlockSpec((1, D_MODEL), lambda i: (0, 0)),
            pl.BlockSpec((D_MODEL, 2 * CROSS_WIDTH), lambda i: (0, 0)),
        ],
        out_specs=pl.BlockSpec((tm, 2 * CROSS_WIDTH), lambda i: (i, 0)),
        out_shape=jax.ShapeDtypeStruct((rows, 2 * CROSS_WIDTH), BF16),
        compiler_params=pltpu.CompilerParams(dimension_semantics=("parallel",)),
        name="memkv",
    )(mem2d, g, w_bf)


_O_QKV = POOL_WIDTH
_O_QC = _O_QKV + QKV_WIDTH
_O_GATE = _O_QC + CROSS_WIDTH


def _proj_kernel(x_ref, g_ref, w_ref, tab_ref, u_ref, a0_ref, a1_ref, a2_ref, qc_ref, gate_ref, xcols):
    tm = TOKEN_TILE
    n_cols = D_MODEL // LANES
    g = g_ref[...]
    xn = _rms(x_ref[...], g).astype(BF16)
    for c in range(n_cols):
        xcols[c] = x_ref[:, c * LANES:(c + 1) * LANES]

    def mm(lhs, c0, width=PROJ_CHUNK):
        return jnp.dot(lhs, w_ref[:, c0:c0 + width], preferred_element_type=F32)

    for j in range(POOL_WIDTH // PROJ_CHUNK):
        u_ref[:, j * PROJ_CHUNK:(j + 1) * PROJ_CHUNK] = mm(xn, j * PROJ_CHUNK)

    for grp, a_ref in enumerate((a0_ref, a1_ref, a2_ref)):
        dil = DIL_CONFIGS[grp][1]
        res_rows = tm // dil
        if dil == 1:
            lhs = xn
        else:
            xp = jnp.concatenate(
                [jnp.concatenate([xcols[c, pl.ds(r, res_rows, stride=dil), :] for c in range(n_cols)], axis=1)
                 for r in range(dil)], axis=0)
            lhs = _rms(xp, g).astype(BF16)
        cos, s1, s2 = tab_ref[0, grp], tab_ref[1, grp], tab_ref[2, grp]
        for which in range(3):
            t = mm(lhs, _O_QKV + which * ATTN_WIDTH + grp * ATTN_OUT_WIDTH, ATTN_OUT_WIDTH)
            if which < 2:
                halves = []
                for hh in range(ATTN_OUT_WIDTH // LANES):
                    th = t[:, hh * LANES:(hh + 1) * LANES]
                    th = (th * cos + pltpu.roll(th, LANES - ROPE_DIM // 2, axis=1) * s1
                          + pltpu.roll(th, ROPE_DIM // 2, axis=1) * s2)
                    halves.append(th)
                t = jnp.concatenate(halves, axis=1)
                if which == 0:
                    t = t * (1.0 / math.sqrt(HEAD_DIM))
            tb = t.astype(BF16)
            for r in range(dil):
                c0 = (r * 3 + which) * ATTN_OUT_WIDTH
                a_ref[:, c0:c0 + ATTN_OUT_WIDTH] = tb[r * res_rows:(r + 1) * res_rows, :]

    for j in range(CROSS_WIDTH // PROJ_CHUNK):
        qc_ref[:, j * PROJ_CHUNK:(j + 1) * PROJ_CHUNK] = mm(xn, _O_QC + j * PROJ_CHUNK).astype(BF16)

    for j in range(N_BRANCHES * D_MODEL // PROJ_CHUNK):
        gate_ref[:, j * PROJ_CHUNK:(j + 1) * PROJ_CHUNK] = jax.nn.sigmoid(mm(xn, _O_GATE + j * PROJ_CHUNK)).astype(BF16)


def _proj(x2d, g, w_in_bf, tabs, seq):
    rows = x2d.shape[0]
    tm = TOKEN_TILE
    tiles_per_seq = seq // tm
    n_grp = len(DIL_CONFIGS)
    a_specs = [pl.BlockSpec((tm // d, d * ATTN_WIDTH), lambda i: (i, 0)) for _, d in DIL_CONFIGS]
    a_shapes = [jax.ShapeDtypeStruct((rows // d, d * ATTN_WIDTH), BF16) for _, d in DIL_CONFIGS]
    return pl.pallas_call(
        _proj_kernel,
        grid=(rows // tm,),
        in_specs=[
            pl.BlockSpec((tm, D_MODEL), lambda i: (i, 0)),
            pl.BlockSpec((1, D_MODEL), lambda i: (0, 0)),
            pl.BlockSpec((D_MODEL, IN_WIDTH), lambda i: (0, 0), pipeline_mode=pl.Buffered(1)),
            pl.BlockSpec((3, n_grp, tm, LANES), lambda i: (0, 0, i % tiles_per_seq, 0)),
        ],
        out_specs=[
            pl.BlockSpec((tm, POOL_WIDTH), lambda i: (i, 0)),
            *a_specs,
            pl.BlockSpec((tm, CROSS_WIDTH), lambda i: (i, 0)),
            pl.BlockSpec((tm, N_BRANCHES * D_MODEL), lambda i: (i, 0)),
        ],
        out_shape=[
            jax.ShapeDtypeStruct((rows, POOL_WIDTH), F32),
            *a_shapes,
            jax.ShapeDtypeStruct((rows, CROSS_WIDTH), BF16),
            jax.ShapeDtypeStruct((rows, N_BRANCHES * D_MODEL), BF16),
        ],
        scratch_shapes=[pltpu.VMEM((D_MODEL // LANES, tm, LANES), F32)],
        compiler_params=pltpu.CompilerParams(dimension_semantics=("parallel",), vmem_limit_bytes=VMEM_LIMIT),
        name="proj",
    )(x2d, g, w_in_bf, tabs)


def _rope_tables(seq):
    pos = jnp.arange(seq, dtype=F32)
    inv = jnp.power(jnp.float32(ROPE_THETA), -jnp.arange(0, ROPE_DIM, 2, dtype=F32) / ROPE_DIM)
    ang = pos[:, None] * inv[None, :]
    half = ROPE_DIM // 2
    j = jnp.arange(LANES) % HEAD_DIM
    cos_l = jnp.cos(ang)[:, j % half]
    sin_l = jnp.sin(ang)[:, j % half]
    kinds = jnp.stack([
        jnp.where(j[None, :] < ROPE_DIM, cos_l, 1.0),
        jnp.where(j[None, :] < half, -sin_l, 0.0),
        jnp.where((j[None, :] >= half) & (j[None, :] < ROPE_DIM), sin_l, 0.0),
    ]).astype(F32)
    tm = TOKEN_TILE
    per_group = []
    for _, d in DIL_CONFIGS:
        t = kinds.reshape(3, seq // tm, tm // d, d, LANES).transpose(0, 1, 3, 2, 4).reshape(3, seq, LANES)
        per_group.append(t)
    return jnp.stack(per_group, axis=1)


def _dilattn_kernel(a_ref, bias_ref, o_ref, lse_ref, *, res_len, tq, win, n_res):
    t = pl.program_id(2)
    nblk = tq // BAND_BLOCK
    stacked = HEADS_PER_DIL * BAND_BLOCK
    lane_head = lax.broadcasted_iota(I32, (BAND_BLOCK, ATTN_OUT_WIDTH), 1) // HEAD_DIM
    head_sel = [lane_head == h for h in range(HEADS_PER_DIL)]
    head_mask = [jnp.where(sel, 1.0, 0.0).astype(BF16) for sel in head_sel]
    ones = jnp.ones((win, LANES), BF16)
    blocks = [(rr, n) for rr in range(n_res) for n in range(nblk)]

    s_parts, v_wins = [], []
    for rr, n in blocks:
        c0 = rr * ATTN_WIDTH
        qpos0 = pl.multiple_of(t * tq + n * BAND_BLOCK, BAND_BLOCK)
        start = pl.multiple_of(jnp.clip(qpos0 - BAND_BLOCK, 0, res_len - win), BAND_BLOCK)
        qb = a_ref[0, pl.ds(qpos0, BAND_BLOCK), c0:c0 + ATTN_OUT_WIDTH]
        kw = a_ref[0, pl.ds(start, win), c0 + ATTN_OUT_WIDTH:c0 + 2 * ATTN_OUT_WIDTH]
        v_wins.append(a_ref[0, pl.ds(start, win), c0 + 2 * ATTN_OUT_WIDTH:c0 + 3 * ATTN_OUT_WIDTH])
        qs = jnp.concatenate([qb * hm for hm in head_mask], axis=0)
        s = lax.dot_general(qs, kw, (((1,), (1,)), ((), ())), preferred_element_type=F32)
        s_parts.append(s + bias_ref[(qpos0 - start) // BAND_BLOCK])
    s_all = jnp.concatenate(s_parts, axis=0)
    m_all = jnp.max(s_all, axis=-1, keepdims=True)
    p_all = jnp.exp(s_all - m_all).astype(BF16)

    for i, (rr, n) in enumerate(blocks):
        p = p_all[i * stacked:(i + 1) * stacked]
        o_st = jnp.dot(p, v_wins[i], preferred_element_type=F32)
        l_st = jnp.dot(p, ones, preferred_element_type=F32)
        m_st = m_all[i * stacked:(i + 1) * stacked]
        o = jnp.zeros((BAND_BLOCK, ATTN_OUT_WIDTH), F32)
        l = jnp.ones((BAND_BLOCK, ATTN_OUT_WIDTH), F32)
        m = jnp.zeros((BAND_BLOCK, ATTN_OUT_WIDTH), F32)
        for h in range(HEADS_PER_DIL):
            rs = slice(h * BAND_BLOCK, (h + 1) * BAND_BLOCK)
            o = jnp.where(head_sel[h], o_st[rs], o)
            l = jnp.where(head_sel[h], jnp.concatenate([l_st[rs], l_st[rs]], axis=1), l)
            m = jnp.where(head_sel[h], m_st[rs], m)
        rows = slice(n * BAND_BLOCK, (n + 1) * BAND_BLOCK)
        cols = slice(rr * ATTN_OUT_WIDTH, (rr + 1) * ATTN_OUT_WIDTH)
        o_ref[0, rows, cols] = (o / l).astype(BF16)
        lse_ref[0, rows, cols] = m + jnp.log(l)


ATTN_BLOCKS_PER_STEP = 16


def _dilattn(a_grp, batch, seq, group):
    window, dil = DIL_CONFIGS[group]
    half = window // (2 * dil)
    res_len = seq // dil
    win = min(3 * BAND_BLOCK, res_len)
    n_res = min(dil, 4)
    tq = min(res_len, ATTN_BLOCKS_PER_STEP // n_res * BAND_BLOCK)
    qkv3 = a_grp.reshape(batch, res_len, dil * ATTN_WIDTH)
    kern = functools.partial(_dilattn_kernel, res_len=res_len, tq=tq, win=win, n_res=n_res)
    out_spec = pl.BlockSpec((1, tq, n_res * ATTN_OUT_WIDTH), lambda b, r, t: (b, t, r))
    n_off = win // BAND_BLOCK
    stacked = HEADS_PER_DIL * BAND_BLOCK
    key_minus_query = (jnp.arange(win)[None, None, :] - (jnp.arange(stacked) % BAND_BLOCK)[None, :, None]
                       - BAND_BLOCK * jnp.arange(n_off)[:, None, None])
    bias = jnp.where(jnp.abs(key_minus_query) <= half, 0.0, NEG_INF).astype(F32)
    o, lse = pl.pallas_call(
        kern,
        grid=(batch, dil // n_res, res_len // tq),
        in_specs=[pl.BlockSpec((1, res_len, n_res * ATTN_WIDTH), lambda b, r, t: (b, 0, r)),
                  pl.BlockSpec((n_off, stacked, win), lambda b, r, t: (0, 0, 0))],
        out_specs=[out_spec, out_spec],
        out_shape=[
            jax.ShapeDtypeStruct((batch, res_len, dil * ATTN_OUT_WIDTH), BF16),
            jax.ShapeDtypeStruct((batch, res_len, dil * ATTN_OUT_WIDTH), F32),
        ],
        compiler_params=pltpu.CompilerParams(dimension_semantics=("parallel", "parallel", "parallel"),
                                             vmem_limit_bytes=VMEM_LIMIT),
        name=f"dilattn{group}",
    )(qkv3, bias)
    return (o.reshape(batch * res_len, dil * ATTN_OUT_WIDTH), lse.reshape(batch * res_len, dil * ATTN_OUT_WIDTH))


def _mix_kernel(x_ref, u_ref, up_ref, un_ref, qc_ref, gate_ref, o0_ref, o1_ref, o2_ref, l0_ref, l1_ref, l2_ref,
                kv_ref, wpool_ref, pscale_ref, wbp_ref, wba_ref, wbc_ref, wout_ref, h_ref, ubuf, merged, relay, runs,
                *, seq):
    tm = TOKEN_TILE
    i = pl.program_id(1)
    nt = pl.num_programs(1)

    ubuf[0:POOL_HALO, :] = jnp.where(i > 0, up_ref[...], 0.0)
    ubuf[POOL_HALO:POOL_HALO + tm, :] = u_ref[...]
    ubuf[POOL_HALO + tm:, :] = jnp.where(i < nt - 1, un_ref[...], 0.0)
    edge_row = lax.broadcasted_iota(I32, (POOL_EDGE, POOL_GROUP_DIM), 0)
    pos_top = i * tm + edge_row
    pos_bot = i * tm + (tm - POOL_EDGE) + edge_row
    pool_parts = []
    for g, w in enumerate(POOL_WINDOWS):
        cs = slice(g * POOL_GROUP_DIM, (g + 1) * POOL_GROUP_DIM)
        half_w = w // 2
        need = {half_w: tm + POOL_HALO}
        m = half_w
        while m > 1:
            need[m // 2] = need[m] + m // 2
            m //= 2
        src, m = None, 1
        while m < half_w:
            n = need[2 * m]
            dst = runs.at[(m.bit_length() - 1) % 2]
            if src is None:
                dst[0:n, :] = ubuf[0:n, cs] + ubuf[m:m + n, cs]
            else:
                dst[0:n, :] = src[0:n, :] + src[m:m + n, :]
            src, m = dst, 2 * m
        lo = POOL_HALO - half_w
        if src is None:
            acc = ubuf[lo:lo + tm, cs] + ubuf[POOL_HALO:POOL_HALO + tm, cs]
        else:
            acc = src[lo:lo + tm, :] + src[POOL_HALO:POOL_HALO + tm, :]
        cnt_top = (jnp.minimum(pos_top + w // 2, seq) - jnp.maximum(pos_top - w // 2, 0)).astype(F32)
        cnt_bot = (jnp.minimum(pos_bot + w // 2, seq) - jnp.maximum(pos_bot - w // 2, 0)).astype(F32)
        mean = jnp.concatenate([acc[:POOL_EDGE] / cnt_top,
                                acc[POOL_EDGE:tm - POOL_EDGE] * (1.0 / w),
                                acc[tm - POOL_EDGE:] / cnt_bot], axis=0)
        z = mean - ubuf[POOL_HALO:POOL_HALO + tm, cs]
        zp = jnp.dot(z.astype(BF16), wpool_ref[g], preferred_element_type=F32)
        pool_parts.append(zp * pscale_ref[:, cs])
    pool_bf = jnp.concatenate(pool_parts, axis=1).astype(BF16)

    cross_parts = []
    for h in range(CROSS_HEADS):
        cs = slice(h * CROSS_HEAD_DIM, (h + 1) * CROSS_HEAD_DIM)
        kh = kv_ref[0, :, cs]
        vh = kv_ref[0, :, CROSS_WIDTH + h * CROSS_HEAD_DIM:CROSS_WIDTH + (h + 1) * CROSS_HEAD_DIM]
        s = lax.dot_general(qc_ref[:, cs], kh, (((1,), (1,)), ((), ())), preferred_element_type=F32)
        s = s * (1.0 / math.sqrt(CROSS_HEAD_DIM))
        m = jnp.max(s, axis=-1, keepdims=True)
        p = jnp.exp(s - m)
        l = jnp.sum(p, axis=-1, keepdims=True)
        cross_parts.append(jnp.dot(p.astype(BF16), vh, preferred_element_type=F32) / l)
    cross_bf = jnp.concatenate(cross_parts, axis=1).astype(BF16)

    def token_major(ref, slot, dil):
        if dil == 1:
            return ref[...].astype(F32)
        res_rows = tm // dil
        for r in range(dil):
            for hh in range(ATTN_OUT_WIDTH // LANES):
                c0 = r * ATTN_OUT_WIDTH + hh * LANES
                relay[slot, hh, pl.ds(r, res_rows, stride=dil), :] = ref[:, c0:c0 + LANES].astype(F32)
        return jnp.concatenate([relay[slot, hh] for hh in range(ATTN_OUT_WIDTH // LANES)], axis=1)

    dils = [d for _, d in DIL_CONFIGS]
    o0, o1, o2 = [token_major(r, s, d) for r, s, d in zip((o0_ref, o1_ref, o2_ref), (0, 1, 2), dils)]
    l0, l1, l2 = [token_major(r, s, d) for r, s, d in zip((l0_ref, l1_ref, l2_ref), (3, 4, 5), dils)]
    mx = jnp.maximum(jnp.maximum(l0, l1), l2)
    e0, e1, e2 = jnp.exp(l0 - mx), jnp.exp(l1 - mx), jnp.exp(l2 - mx)
    attn = (e0 * o0 + e1 * o1 + e2 * o2) / (e0 + e1 + e2)
    attn_bf = attn.astype(BF16)

    for c in range(D_MODEL // MIX_CHUNK):
        cs = slice(c * MIX_CHUNK, (c + 1) * MIX_CHUNK)
        mrg = gate_ref[:, cs].astype(F32) * jnp.dot(pool_bf, wbp_ref[:, cs], preferred_element_type=F32)
        mrg = mrg + gate_ref[:, D_MODEL + c * MIX_CHUNK:D_MODEL + (c + 1) * MIX_CHUNK].astype(F32) * jnp.dot(
            attn_bf, wba_ref[:, cs], preferred_element_type=F32)
        mrg = mrg + gate_ref[:, 2 * D_MODEL + c * MIX_CHUNK:2 * D_MODEL + (c + 1) * MIX_CHUNK].astype(F32) * jnp.dot(
            cross_bf, wbc_ref[:, cs], preferred_element_type=F32)
        merged[:, cs] = mrg.astype(BF16)
    h_ref[...] = x_ref[...] + jnp.dot(merged[...], wout_ref[...], preferred_element_type=F32)


def _mix(x2d, u, qc, gate, outs, lses, kv3, wpool_bf, pscale, wbp, wba, wbc, wout, batch, seq):
    tm = TOKEN_TILE
    ts = seq // tm
    rows = batch * seq
    hb = tm // POOL_HALO
    n_halo = rows // POOL_HALO

    def row(b, i):
        return (b * ts + i, 0)

    def const(b, i):
        return (0, 0)

    tok = lambda w: pl.BlockSpec((tm, w), row)
    in_specs = [
        tok(D_MODEL),
        tok(POOL_WIDTH),
        pl.BlockSpec((POOL_HALO, POOL_WIDTH), lambda b, i: (jnp.maximum((b * ts + i) * hb - 1, 0), 0)),
        pl.BlockSpec((POOL_HALO, POOL_WIDTH), lambda b, i: (jnp.minimum((b * ts + i + 1) * hb, n_halo - 1), 0)),
        tok(CROSS_WIDTH),
        tok(N_BRANCHES * D_MODEL),
        *[pl.BlockSpec((tm // d, d * ATTN_OUT_WIDTH), row) for _, d in DIL_CONFIGS],
        *[pl.BlockSpec((tm // d, d * ATTN_OUT_WIDTH), row) for _, d in DIL_CONFIGS],
        pl.BlockSpec((1, kv3.shape[1], 2 * CROSS_WIDTH), lambda b, i: (b, 0, 0)),
        pl.BlockSpec((POOL_GROUPS, POOL_GROUP_DIM, POOL_GROUP_DIM), lambda b, i: (0, 0, 0)),
        pl.BlockSpec((1, POOL_WIDTH), const),
        pl.BlockSpec((POOL_WIDTH, D_MODEL), const),
        pl.BlockSpec((ATTN_OUT_WIDTH, D_MODEL), const),
        pl.BlockSpec((CROSS_WIDTH, D_MODEL), const),
        pl.BlockSpec((D_MODEL, D_MODEL), const),
    ]
    return pl.pallas_call(
        functools.partial(_mix_kernel, seq=seq),
        grid=(batch, ts),
        in_specs=in_specs,
        out_specs=pl.BlockSpec((tm, D_MODEL), row),
        out_shape=jax.ShapeDtypeStruct((rows, D_MODEL), F32),
        scratch_shapes=[pltpu.VMEM((tm + 2 * POOL_HALO, POOL_WIDTH), F32), pltpu.VMEM((tm, D_MODEL), BF16),
                        pltpu.VMEM((2 * len(DIL_CONFIGS), ATTN_OUT_WIDTH // LANES, tm, LANES), F32),
                        pltpu.VMEM((2, tm + 2 * POOL_HALO, POOL_GROUP_DIM), F32)],
        compiler_params=pltpu.CompilerParams(dimension_semantics=("parallel", "parallel"), vmem_limit_bytes=VMEM_LIMIT),
        name="mix",
    )(x2d, u, u, u, qc, gate, *outs, *lses, kv3, wpool_bf, pscale, wbp, wba, wbc, wout)


def _route_kernel(h_ref, g_ref, whi_ref, wlo_ref, b_ref, hn_ref, ri_ref, rg_ref, cnt_ref, carry, tri):
    tm = ROUTE_TILE
    step = pl.program_id(0)

    @pl.when(step == 0)
    def _():
        carry[...] = jnp.zeros_like(carry)
        r = lax.broadcasted_iota(I32, (tm, tm), 0)
        c = lax.broadcasted_iota(I32, (tm, tm), 1)
        tri[...] = jnp.where(r < c, 1.0, 0.0).astype(BF16)

    hn = _rms(h_ref[...], g_ref[...])
    hn_ref[...] = _pack_bf16_pair(hn)
    hi = hn.astype(BF16)
    lo = (hn - hi.astype(F32)).astype(BF16)
    logits = (jnp.dot(hi, whi_ref[...], preferred_element_type=F32)
              + jnp.dot(lo, whi_ref[...], preferred_element_type=F32)
              + jnp.dot(hi, wlo_ref[...], preferred_element_type=F32)) + b_ref[...]
    work = logits.T[:N_EXPERTS]
    row = lax.broadcasted_iota(I32, (N_EXPERTS, tm), 0).astype(F32)
    idxs, vals = [], []
    onehot = jnp.zeros((N_EXPERTS, tm), F32)
    for _ in range(TOP_K):
        m = jnp.max(work, axis=0, keepdims=True)
        idx = jnp.min(jnp.where(work == m, row, float(N_EXPERTS)), axis=0, keepdims=True)
        sel = row == idx
        onehot = jnp.where(sel, 1.0, onehot)
        work = jnp.where(sel, -jnp.inf, work)
        idxs.append(idx)
        vals.append(m)
    exps = [jnp.exp(v - vals[0]) for v in vals]
    den = exps[0] + exps[1] + exps[2] + exps[3]
    gates = [e / den for e in exps]

    prefix = jnp.dot(onehot.astype(BF16), tri[...], preferred_element_type=F32) + carry[:, 0:1]
    ranks = [jnp.sum(jnp.where(row == idx, prefix, 0.0), axis=0, keepdims=True) for idx in idxs]
    carry[...] = carry[...] + jnp.sum(onehot, axis=1, keepdims=True)

    row8 = lax.broadcasted_iota(I32, (2 * TOP_K, tm), 0)
    ri = jnp.zeros((2 * TOP_K, tm), F32)
    row128 = lax.broadcasted_iota(I32, (LANES, tm), 0)
    rg = jnp.zeros((LANES, tm), F32)
    for k in range(TOP_K):
        ri = jnp.where(row8 == k, idxs[k], ri)
        ri = jnp.where(row8 == TOP_K + k, ranks[k], ri)
        rg = jnp.where(row128 == k, gates[k], rg)
    ri_ref[...] = ri.astype(I32)
    rg_ref[...] = rg.T
    cnt_ref[...] = carry[...].astype(I32)


def _route(h2d, g, whi, wlo, b):
    rows = h2d.shape[0]
    tm = ROUTE_TILE
    const = lambda i: (0, 0)
    return pl.pallas_call(
        _route_kernel,
        grid=(rows // tm,),
        in_specs=[
            pl.BlockSpec((tm, D_MODEL), lambda i: (i, 0)),
            pl.BlockSpec((1, D_MODEL), const),
            pl.BlockSpec((D_MODEL, LANES), const),
            pl.BlockSpec((D_MODEL, LANES), const),
            pl.BlockSpec((1, LANES), const),
        ],
        out_specs=[
            pl.BlockSpec((tm, PACKED), lambda i: (i, 0)),
            pl.BlockSpec((2 * TOP_K, tm), lambda i: (0, i)),
            pl.BlockSpec((tm, LANES), lambda i: (i, 0)),
            pl.BlockSpec((N_EXPERTS, LANES), const),
        ],
        out_shape=[
            jax.ShapeDtypeStruct((rows, PACKED), I32),
            jax.ShapeDtypeStruct((2 * TOP_K, rows), I32),
            jax.ShapeDtypeStruct((rows, LANES), F32),
            jax.ShapeDtypeStruct((N_EXPERTS, LANES), I32),
        ],
        scratch_shapes=[pltpu.VMEM((N_EXPERTS, LANES), F32), pltpu.VMEM((tm, tm), BF16)],
        compiler_params=pltpu.CompilerParams(dimension_semantics=("arbitrary",), vmem_limit_bytes=VMEM_LIMIT),
        name="route",
    )(h2d, g, whi, wlo, b)


def _sc_mesh():
    return plsc.VectorSubcoreMesh(core_axis_name="c", subcore_axis_name="s")


def _sc_workers():
    info = plsc.get_sparse_core_info()
    return info.num_cores, info.num_cores * info.num_subcores


def _sc_dispatch(rows_packed, slots, n_slots):
    n_tok = rows_packed.shape[0]
    n_cores, n_workers = _sc_workers()
    chunks_per_worker = n_tok // SC_WINDOW // n_workers

    @functools.partial(pl.kernel, out_type=jax.ShapeDtypeStruct((n_slots, PACKED), I32), mesh=_sc_mesh(),
                       scratch_types=[pltpu.VMEM((TOP_K, SC_WINDOW), I32), pltpu.VMEM((SC_WINDOW, PACKED), I32)],
                       name="dispatch")
    def kern(x_hbm, i_hbm, o_hbm, idx_v, rows_v):
        wid = lax.axis_index("s") * n_cores + lax.axis_index("c")

        @pl.loop(0, chunks_per_worker)
        def _(j):
            chunk = wid * chunks_per_worker + j
            pltpu.sync_copy(i_hbm.at[chunk], idx_v)
            pltpu.sync_copy(x_hbm.at[pl.ds(chunk * SC_WINDOW, SC_WINDOW)], rows_v)
            for k in range(TOP_K):
                pltpu.sync_copy(rows_v, o_hbm.at[idx_v.at[k]])

    return kern(rows_packed, slots)


def _sc_combine(ys, slots):
    n_tok = slots.shape[0] * SC_WINDOW
    n_cores, n_workers = _sc_workers()
    chunks_per_worker = n_tok // SC_WINDOW // n_workers

    @functools.partial(pl.kernel, out_type=jax.ShapeDtypeStruct((TOP_K, n_tok, PACKED), I32), mesh=_sc_mesh(),
                       scratch_types=[pltpu.VMEM((TOP_K, SC_WINDOW), I32), pltpu.VMEM((SC_WINDOW, PACKED), I32)],
                       name="combine")
    def kern(y_hbm, i_hbm, o_hbm, idx_v, rows_v):
        wid = lax.axis_index("s") * n_cores + lax.axis_index("c")

        @pl.loop(0, chunks_per_worker)
        def _(j):
            chunk = wid * chunks_per_worker + j
            pltpu.sync_copy(i_hbm.at[chunk], idx_v)
            for k in range(TOP_K):
                pltpu.sync_copy(y_hbm.at[idx_v.at[k]], rows_v)
                pltpu.sync_copy(rows_v, o_hbm.at[k, pl.ds(chunk * SC_WINDOW, SC_WINDOW)])

    return kern(ys, slots)


def _expert_kernel(blk_e_ref, nused_ref, x_ref, wu_ref, bu_ref, wd_ref, bd_ref, o_ref, wu_bf, wd_bf):
    i = pl.program_id(0)
    used = i < nused_ref[0]

    @pl.when(used & ((i == 0) | (blk_e_ref[i] != blk_e_ref[jnp.maximum(i - 1, 0)])))
    def _():
        for c in range(2 * D_FF // CAST_CHUNK):
            cs = slice(c * CAST_CHUNK, (c + 1) * CAST_CHUNK)
            wu_bf[:, cs] = wu_ref[0, :, cs].astype(BF16)
        for c in range(D_MODEL // CAST_CHUNK):
            cs = slice(c * CAST_CHUNK, (c + 1) * CAST_CHUNK)
            wd_bf[:, cs] = wd_ref[0, :, cs].astype(BF16)

    @pl.when(used)
    def _():
        lo, hi = _unpack_bf16_pair(x_ref[...])
        x = jnp.concatenate([lo.astype(BF16), hi.astype(BF16)], axis=1)
        y = jnp.zeros((EXPERT_BLOCK, D_MODEL), F32)
        for c in range(D_FF // FF_CHUNK):
            gs = slice(c * FF_CHUNK, (c + 1) * FF_CHUNK)
            us = slice(D_FF + c * FF_CHUNK, D_FF + (c + 1) * FF_CHUNK)
            gate = jnp.dot(x, wu_bf[:, gs], preferred_element_type=F32) + bu_ref[0, :, gs]
            up = jnp.dot(x, wu_bf[:, us], preferred_element_type=F32) + bu_ref[0, :, us]
            gate = jnp.minimum(gate, SWIGLU_LIMIT)
            up = jnp.clip(up, -SWIGLU_LIMIT, SWIGLU_LIMIT)
            act = (up + 1.0) * (gate * jax.nn.sigmoid(SWIGLU_ALPHA * gate))
            y = y + jnp.dot(act.astype(BF16), wd_bf[gs, :], preferred_element_type=F32)
        o_ref[...] = _pack_bf16_pair(y + bd_ref[0])

    @pl.when(jnp.logical_not(used))
    def _():
        o_ref[...] = jnp.zeros_like(o_ref)


def _experts(xs, blk_e, nused, wu, bu, wd, bd):
    n_slots = xs.shape[0]
    tb = EXPERT_BLOCK
    grid_spec = pltpu.PrefetchScalarGridSpec(
        num_scalar_prefetch=2,
        grid=(n_slots // tb,),
        in_specs=[
            pl.BlockSpec((tb, PACKED), lambda i, be, nu: (i, 0)),
            pl.BlockSpec((1, D_MODEL, 2 * D_FF), lambda i, be, nu: (be[i], 0, 0)),
            pl.BlockSpec((1, 1, 2 * D_FF), lambda i, be, nu: (be[i], 0, 0)),
            pl.BlockSpec((1, D_FF, D_MODEL), lambda i, be, nu: (be[i], 0, 0)),
            pl.BlockSpec((1, 1, D_MODEL), lambda i, be, nu: (be[i], 0, 0)),
        ],
        out_specs=pl.BlockSpec((tb, PACKED), lambda i, be, nu: (i, 0)),
        scratch_shapes=[pltpu.VMEM((D_MODEL, 2 * D_FF), BF16), pltpu.VMEM((D_FF, D_MODEL), BF16)],
    )
    return pl.pallas_call(
        _expert_kernel,
        grid_spec=grid_spec,
        out_shape=jax.ShapeDtypeStruct((n_slots, PACKED), I32),
        compiler_params=pltpu.CompilerParams(dimension_semantics=("arbitrary",), vmem_limit_bytes=VMEM_LIMIT),
        name="experts",
    )(blk_e, nused, xs, wu, bu, wd, bd)


def _final_kernel(h_ref, yg_ref, rg_ref, g_ref, o_ref):
    rg = rg_ref[...]
    lo = jnp.zeros((FINAL_TILE, PACKED), F32)
    hi = jnp.zeros((FINAL_TILE, PACKED), F32)
    for k in range(TOP_K):
        yl, yh = _unpack_bf16_pair(yg_ref[k])
        w = rg[:, k:k + 1]
        lo = lo + w * yl
        hi = hi + w * yh
    x = h_ref[...] + jnp.concatenate([lo, hi], axis=1)
    o_ref[...] = _rms(x, g_ref[...])


def _final(h2d, yg, rg, g):
    rows = h2d.shape[0]
    tm = FINAL_TILE
    return pl.pallas_call(
        _final_kernel,
        grid=(rows // tm,),
        in_specs=[
            pl.BlockSpec((tm, D_MODEL), lambda i: (i, 0)),
            pl.BlockSpec((TOP_K, tm, PACKED), lambda i: (0, i, 0)),
            pl.BlockSpec((tm, LANES), lambda i: (i, 0)),
            pl.BlockSpec((1, D_MODEL), lambda i: (0, 0)),
        ],
        out_specs=pl.BlockSpec((tm, D_MODEL), lambda i: (i, 0)),
        out_shape=jax.ShapeDtypeStruct((rows, D_MODEL), F32),
        compiler_params=pltpu.CompilerParams(dimension_semantics=("parallel",), vmem_limit_bytes=VMEM_LIMIT),
        name="final",
    )(h2d, yg, rg, g)


def _plan(ri, counts, n_tok):
    tb = EXPERT_BLOCK
    n_slots = n_tok * TOP_K + N_EXPERTS * tb
    eid = ri[:TOP_K]
    rank = ri[TOP_K:]
    c = counts[:, 0]
    pc = (c + tb - 1) // tb * tb
    pend = jnp.cumsum(pc)
    pstart = pend - pc
    base = jnp.zeros_like(eid)
    for e in range(N_EXPERTS):
        base = jnp.where(eid == e, pstart[e], base)
    slots = (base + rank).astype(I32)
    slots = slots.reshape(TOP_K, n_tok // SC_WINDOW, SC_WINDOW).transpose(1, 0, 2)
    blk_start = jnp.arange(n_slots // tb, dtype=I32) * tb
    blk_e = jnp.minimum(jnp.sum(blk_start[:, None] >= pend[None, :], axis=1), N_EXPERTS - 1).astype(I32)
    nused = (pend[-1:] // tb).astype(I32)
    return slots, blk_e, nused, n_slots


def _trunk(x, mem, p):
    batch, seq, _ = x.shape
    n_tok = batch * seq
    x2d = x.reshape(n_tok, D_MODEL)
    kv = _memkv(mem.reshape(-1, D_MODEL), p["norm_mem_g"], p["w_mem_kv"])
    kv3 = kv.reshape(batch, mem.shape[1], 2 * CROSS_WIDTH)
    u, a0, a1, a2, qc, gate = _proj(x2d, p["norm_mix_g"], p["w_in"], _rope_tables(seq), seq)
    outs, lses = [], []
    for g, a_grp in enumerate((a0, a1, a2)):
        o, lse = _dilattn(a_grp, batch, seq, g)
        outs.append(o)
        lses.append(lse)
    h = _mix(x2d, u, qc, gate, outs, lses, kv3, p["w_pool"], p["pool_scale"], p["w_br_pool"], p["w_br_attn"],
             p["w_br_cross"], p["w_out"], batch, seq)
    hn, ri, rg, counts = _route(h, p["norm_ffn_g"], p["w_router_hi"], p["w_router_lo"], p["b_router"])
    slots, blk_e, nused, n_slots = _plan(ri, counts, n_tok)
    xs = _sc_dispatch(hn, slots, n_slots)
    ys = _experts(xs, blk_e, nused, p["w_up"], p["b_up"], p["w_down"], p["b_down"])
    yg = _sc_combine(ys, slots)
    out = _final(h, yg, rg, p["norm_final_g"])
    return out.reshape(batch, seq, D_MODEL)


def _prep_params(norm_mix_g, norm_mem_g, w_in, w_pool, pool_scale, w_mem_kv, w_br_pool, w_br_attn, w_br_cross,
                 w_out, norm_ffn_g, w_router, b_router, w_up, b_up, w_down, b_down, norm_final_g):
    wr = w_router[0]
    wr_hi = wr.astype(BF16)
    return dict(
        norm_mix_g=norm_mix_g[0].reshape(1, D_MODEL),
        norm_mem_g=norm_mem_g[0].reshape(1, D_MODEL),
        w_in=w_in[0].astype(BF16),
        w_pool=w_pool[0].astype(BF16),
        pool_scale=pool_scale[0].reshape(1, POOL_WIDTH),
        w_mem_kv=w_mem_kv[0].astype(BF16),
        w_br_pool=w_br_pool[0].astype(BF16),
        w_br_attn=w_br_attn[0].astype(BF16),
        w_br_cross=w_br_cross[0].astype(BF16),
        w_out=w_out[0].astype(BF16),
        norm_ffn_g=norm_ffn_g[0].reshape(1, D_MODEL),
        w_router_hi=jnp.pad(wr_hi, ((0, 0), (0, LANES - N_EXPERTS))),
        w_router_lo=jnp.pad((wr - wr_hi.astype(F32)).astype(BF16), ((0, 0), (0, LANES - N_EXPERTS))),
        b_router=jnp.pad(b_router[0].reshape(1, N_EXPERTS), ((0, 0), (0, LANES - N_EXPERTS))),
        w_up=w_up[0],
        b_up=b_up[0].reshape(N_EXPERTS, 1, 2 * D_FF),
        w_down=w_down[0],
        b_down=b_down[0].reshape(N_EXPERTS, 1, D_MODEL),
        norm_final_g=norm_final_g.reshape(1, D_MODEL),
    )


def kernel(x_prompt, x_sample, mem_prompt, mem_sample, norm_mix_g, norm_mem_g, w_in, w_pool, pool_scale, w_mem_kv,
           w_br_pool, w_br_attn, w_br_cross, w_out, norm_ffn_g, w_router, b_router, w_up, b_up, w_down, b_down,
           norm_final_g):
    p = _prep_params(norm_mix_g, norm_mem_g, w_in, w_pool, pool_scale, w_mem_kv, w_br_pool, w_br_attn, w_br_cross,
                     w_out, norm_ffn_g, w_router, b_router, w_up, b_up, w_down, b_down, norm_final_g)
    y_prompt = _trunk(x_prompt, mem_prompt, p)
    y_sample = _trunk(x_sample, mem_sample, p)
    return (y_prompt, y_sample)
```

```python
import functools
import math

import jax
import jax.numpy as jnp
from jax import lax
from jax.experimental import pallas as pl
from jax.experimental.pallas import tpu as pltpu
from jax.experimental.pallas import tpu_sc as plsc

F32 = jnp.float32
BF16 = jnp.bfloat16
I32 = jnp.int32
U32 = jnp.uint32

D_MODEL = 1024
POOL_GROUPS = 4
POOL_WIDTH = 512
POOL_GROUP_DIM = 128
POOL_WINDOWS = (2, 4, 8, 16)
POOL_HALO = 16
POOL_EDGE = 8
HEAD_DIM = 64
DIL_CONFIGS = ((128, 1), (512, 4), (2048, 16))
HEADS_PER_DIL = 4
ATTN_WIDTH = 768
ATTN_OUT_WIDTH = 256
BAND_BLOCK = 64
ROPE_DIM = 16
ROPE_THETA = 500000.0
CROSS_HEADS = 4
CROSS_HEAD_DIM = 128
CROSS_WIDTH = 512
N_BRANCHES = 3
QKV_WIDTH = 3 * ATTN_WIDTH
IN_WIDTH = POOL_WIDTH + QKV_WIDTH + CROSS_WIDTH + N_BRANCHES * D_MODEL
N_EXPERTS = 32
TOP_K = 4
D_FF = 1024
SWIGLU_LIMIT = 7.0
SWIGLU_ALPHA = 1.702
EPS = 1e-5
NEG_INF = -1e30

LANES = 128
PACKED = D_MODEL // 2
TOKEN_TILE = 512
FINAL_TILE = 1024
ROUTE_TILE = 1024
PROJ_CHUNK = 512
MIX_CHUNK = 256
FF_CHUNK = 512
CAST_CHUNK = 512
EXPERT_BLOCK = 1024
SC_WINDOW = 128
VMEM_LIMIT = 52 * 1024 * 1024


def _rms(x, g):
    r = lax.rsqrt(jnp.mean(x * x, axis=-1, keepdims=True) + EPS)
    return x * r * g


def _pack_bf16_pair(x):
    bits = lax.bitcast_convert_type(x.astype(BF16).astype(F32), U32)
    packed = (bits[:, :PACKED] >> 16) | bits[:, PACKED:]
    return lax.bitcast_convert_type(packed, I32)


def _unpack_bf16_pair(w):
    u = lax.bitcast_convert_type(w, U32)
    lo = lax.bitcast_convert_type(u << 16, F32)
    hi = lax.bitcast_convert_type(u & jnp.uint32(0xFFFF0000), F32)
    return lo, hi


def _memkv_kernel(mem_ref, g_ref, w_ref, o_ref):
    xn = _rms(mem_ref[...], g_ref[...]).astype(BF16)
    o_ref[...] = jnp.dot(xn, w_ref[...], preferred_element_type=F32).astype(BF16)


def _memkv(mem2d, g, w_bf):
    rows = mem2d.shape[0]
    tm = 256
    return pl.pallas_call(
        _memkv_kernel,
        grid=(rows // tm,),
        in_specs=[
            pl.BlockSpec((tm, D_MODEL), lambda i: (i, 0)),
            pl.BlockSpec((1, D_MODEL), lambda i: (0, 0)),
            pl.BlockSpec((D_MODEL, 2 * CROSS_WIDTH), lambda i: (0, 0)),
        ],
        out_specs=pl.BlockSpec((tm, 2 * CROSS_WIDTH), lambda i: (i, 0)),
        out_shape=jax.ShapeDtypeStruct((rows, 2 * CROSS_WIDTH), BF16),
        compiler_params=pltpu.CompilerParams(dimension_semantics=("parallel",)),
        name="memkv",
    )(mem2d, g, w_bf)


_O_QKV = POOL_WIDTH
_O_QC = _O_QKV + QKV_WIDTH
_O_GATE = _O_QC + CROSS_WIDTH


def _proj_kernel(x_ref, g_ref, w_ref, tab_ref, u_ref, a0_ref, a1_ref, a2_ref, qc_ref, gate_ref, xcols):
    tm = TOKEN_TILE
    n_cols = D_MODEL // LANES
    g = g_ref[...]
    xn = _rms(x_ref[...], g).astype(BF16)
    for c in range(n_cols):
        xcols[c] = x_ref[:, c * LANES:(c + 1) * LANES]

    def mm(lhs, c0, width=PROJ_CHUNK):
        return jnp.dot(lhs, w_ref[:, c0:c0 + width], preferred_element_type=F32)

    for j in range(POOL_WIDTH // PROJ_CHUNK):
        u_ref[:, j * PROJ_CHUNK:(j + 1) * PROJ_CHUNK] = mm(xn, j * PROJ_CHUNK)

    for grp, a_ref in enumerate((a0_ref, a1_ref, a2_ref)):
        dil = DIL_CONFIGS[grp][1]
        res_rows = tm // dil
        if dil == 1:
            lhs = xn
        else:
            xp = jnp.concatenate(
                [jnp.concatenate([xcols[c, pl.ds(r, res_rows, stride=dil), :] for c in range(n_cols)], axis=1)
                 for r in range(dil)], axis=0)
            lhs = _rms(xp, g).astype(BF16)
        cos, s1, s2 = tab_ref[0, grp], tab_ref[1, grp], tab_ref[2, grp]
        for which in range(3):
            t = mm(lhs, _O_QKV + which * ATTN_WIDTH + grp * ATTN_OUT_WIDTH, ATTN_OUT_WIDTH)
            if which < 2:
                halves = []
                for hh in range(ATTN_OUT_WIDTH // LANES):
                    th = t[:, hh * LANES:(hh + 1) * LANES]
                    th = (th * cos + pltpu.roll(th, LANES - ROPE_DIM // 2, axis=1) * s1
                          + pltpu.roll(th, ROPE_DIM // 2, axis=1) * s2)
                    halves.append(th)
                t = jnp.concatenate(halves, axis=1)
                if which == 0:
                    t = t * (1.0 / math.sqrt(HEAD_DIM))
            tb = t.astype(BF16)
            for r in range(dil):
                c0 = (r * 3 + which) * ATTN_OUT_WIDTH
                a_ref[:, c0:c0 + ATTN_OUT_WIDTH] = tb[r * res_rows:(r + 1) * res_rows, :]

    for j in range(CROSS_WIDTH // PROJ_CHUNK):
        qc_ref[:, j * PROJ_CHUNK:(j + 1) * PROJ_CHUNK] = mm(xn, _O_QC + j * PROJ_CHUNK).astype(BF16)

    for j in range(N_BRANCHES * D_MODEL // PROJ_CHUNK):
        gate_ref[:, j * PROJ_CHUNK:(j + 1) * PROJ_CHUNK] = jax.nn.sigmoid(mm(xn, _O_GATE + j * PROJ_CHUNK)).astype(BF16)


def _proj(x2d, g, w_in_bf, tabs, seq):
    rows = x2d.shape[0]
    tm = TOKEN_TILE
    tiles_per_seq = seq // tm
    n_grp = len(DIL_CONFIGS)
    a_specs = [pl.BlockSpec((tm // d, d * ATTN_WIDTH), lambda i: (i, 0)) for _, d in DIL_CONFIGS]
    a_shapes = [jax.ShapeDtypeStruct((rows // d, d * ATTN_WIDTH), BF16) for _, d in DIL_CONFIGS]
    return pl.pallas_call(
        _proj_kernel,
        grid=(rows // tm,),
        in_specs=[
            pl.BlockSpec((tm, D_MODEL), lambda i: (i, 0)),
            pl.BlockSpec((1, D_MODEL), lambda i: (0, 0)),
            pl.BlockSpec((D_MODEL, IN_WIDTH), lambda i: (0, 0), pipeline_mode=pl.Buffered(1)),
            pl.BlockSpec((3, n_grp, tm, LANES), lambda i: (0, 0, i % tiles_per_seq, 0)),
        ],
        out_specs=[
            pl.BlockSpec((tm, POOL_WIDTH), lambda i: (i, 0)),
            *a_specs,
            pl.BlockSpec((tm, CROSS_WIDTH), lambda i: (i, 0)),
            pl.BlockSpec((tm, N_BRANCHES * D_MODEL), lambda i: (i, 0)),
        ],
        out_shape=[
            jax.ShapeDtypeStruct((rows, POOL_WIDTH), F32),
            *a_shapes,
            jax.ShapeDtypeStruct((rows, CROSS_WIDTH), BF16),
            jax.ShapeDtypeStruct((rows, N_BRANCHES * D_MODEL), BF16),
        ],
        scratch_shapes=[pltpu.VMEM((D_MODEL // LANES, tm, LANES), F32)],
        compiler_params=pltpu.CompilerParams(dimension_semantics=("parallel",), vmem_limit_bytes=VMEM_LIMIT),
        name="proj",
    )(x2d, g, w_in_bf, tabs)


def _rope_tables(seq):
    pos = jnp.arange(seq, dtype=F32)
    inv = jnp.power(jnp.float32(ROPE_THETA), -jnp.arange(0, ROPE_DIM, 2, dtype=F32) / ROPE_DIM)
    ang = pos[:, None] * inv[None, :]
    half = ROPE_DIM // 2
    j = jnp.arange(LANES) % HEAD_DIM
    cos_l = jnp.cos(ang)[:, j % half]
    sin_l = jnp.sin(ang)[:, j % half]
    kinds = jnp.stack([
        jnp.where(j[None, :] < ROPE_DIM, cos_l, 1.0),
        jnp.where(j[None, :] < half, -sin_l, 0.0),
        jnp.where((j[None, :] >= half) & (j[None, :] < ROPE_DIM), sin_l, 0.0),
    ]).astype(F32)
    tm = TOKEN_TILE
    per_group = []
    for _, d in DIL_CONFIGS:
        t = kinds.reshape(3, seq // tm, tm // d, d, LANES).transpose(0, 1, 3, 2, 4).reshape(3, seq, LANES)
        per_group.append(t)
    return jnp.stack(per_group, axis=1)


def _dilattn_kernel(a_ref, bias_ref, o_ref, lse_ref, *, res_len, tq, win, n_res):
    t = pl.program_id(2)
    nblk = tq // BAND_BLOCK
    stacked = HEADS_PER_DIL * BAND_BLOCK
    lane_head = lax.broadcasted_iota(I32, (BAND_BLOCK, ATTN_OUT_WIDTH), 1) // HEAD_DIM
    head_sel = [lane_head == h for h in range(HEADS_PER_DIL)]
    head_mask = [jnp.where(sel, 1.0, 0.0).astype(BF16) for sel in head_sel]
    ones = jnp.ones((win, LANES), BF16)
    blocks = [(rr, n) for rr in range(n_res) for n in range(nblk)]

    s_parts, v_wins = [], []
    for rr, n in blocks:
        c0 = rr * ATTN_WIDTH
        qpos0 = pl.multiple_of(t * tq + n * BAND_BLOCK, BAND_BLOCK)
        start = pl.multiple_of(jnp.clip(qpos0 - BAND_BLOCK, 0, res_len - win), BAND_BLOCK)
        qb = a_ref[0, pl.ds(qpos0, BAND_BLOCK), c0:c0 + ATTN_OUT_WIDTH]
        kw = a_ref[0, pl.ds(start, win), c0 + ATTN_OUT_WIDTH:c0 + 2 * ATTN_OUT_WIDTH]
        v_wins.append(a_ref[0, pl.ds(start, win), c0 + 2 * ATTN_OUT_WIDTH:c0 + 3 * ATTN_OUT_WIDTH])
        qs = jnp.concatenate([qb * hm for hm in head_mask], axis=0)
        s = lax.dot_general(qs, kw, (((1,), (1,)), ((), ())), preferred_element_type=F32)
        s_parts.append(s + bias_ref[(qpos0 - start) // BAND_BLOCK])
    s_all = jnp.concatenate(s_parts, axis=0)
    m_all = jnp.max(s_all, axis=-1, keepdims=True)
    p_all = jnp.exp(s_all - m_all).astype(BF16)

    for i, (rr, n) in enumerate(blocks):
        p = p_all[i * stacked:(i + 1) * stacked]
        o_st = jnp.dot(p, v_wins[i], preferred_element_type=F32)
        l_st = jnp.dot(p, ones, preferred_element_type=F32)
        m_st = m_all[i * stacked:(i + 1) * stacked]
        o = jnp.zeros((BAND_BLOCK, ATTN_OUT_WIDTH), F32)
        l = jnp.ones((BAND_BLOCK, ATTN_OUT_WIDTH), F32)
        m = jnp.zeros((BAND_BLOCK, ATTN_OUT_WIDTH), F32)
        for h in range(HEADS_PER_DIL):
            rs = slice(h * BAND_BLOCK, (h + 1) * BAND_BLOCK)
            o = jnp.where(head_sel[h], o_st[rs], o)
            l = jnp.where(head_sel[h], jnp.concatenate([l_st[rs], l_st[rs]], axis=1), l)
            m = jnp.where(head_sel[h], m_st[rs], m)
        rows = slice(n * BAND_BLOCK, (n + 1) * BAND_BLOCK)
        cols = slice(rr * ATTN_OUT_WIDTH, (rr + 1) * ATTN_OUT_WIDTH)
        o_ref[0, rows, cols] = (o / l).astype(BF16)
        lse_ref[0, rows, cols] = m + jnp.log(l)


ATTN_BLOCKS_PER_STEP = 32


def _dilattn(a_grp, batch, seq, group):
    window, dil = DIL_CONFIGS[group]
    half = window // (2 * dil)
    res_len = seq // dil
    win = min(3 * BAND_BLOCK, res_len)
    tq = min(res_len, ATTN_BLOCKS_PER_STEP * BAND_BLOCK)
    n_res = min(dil, ATTN_BLOCKS_PER_STEP * BAND_BLOCK // tq)
    qkv3 = a_grp.reshape(batch, res_len, dil * ATTN_WIDTH)
    kern = functools.partial(_dilattn_kernel, res_len=res_len, tq=tq, win=win, n_res=n_res)
    out_spec = pl.BlockSpec((1, tq, n_res * ATTN_OUT_WIDTH), lambda b, r, t: (b, t, r))
    n_off = win // BAND_BLOCK
    stacked = HEADS_PER_DIL * BAND_BLOCK
    key_minus_query = (jnp.arange(win)[None, None, :] - (jnp.arange(stacked) % BAND_BLOCK)[None, :, None]
                       - BAND_BLOCK * jnp.arange(n_off)[:, None, None])
    bias = jnp.where(jnp.abs(key_minus_query) <= half, 0.0, NEG_INF).astype(F32)
    o, lse = pl.pallas_call(
        kern,
        grid=(batch, dil // n_res, res_len // tq),
        in_specs=[pl.BlockSpec((1, res_len, n_res * ATTN_WIDTH), lambda b, r, t: (b, 0, r)),
                  pl.BlockSpec((n_off, stacked, win), lambda b, r, t: (0, 0, 0))],
        out_specs=[out_spec, out_spec],
        out_shape=[
            jax.ShapeDtypeStruct((batch, res_len, dil * ATTN_OUT_WIDTH), BF16),
            jax.ShapeDtypeStruct((batch, res_len, dil * ATTN_OUT_WIDTH), F32),
        ],
        compiler_params=pltpu.CompilerParams(dimension_semantics=("parallel", "parallel", "parallel"),
                                             vmem_limit_bytes=VMEM_LIMIT),
        name=f"dilattn{group}",
    )(qkv3, bias)
    return (o.reshape(batch * res_len, dil * ATTN_OUT_WIDTH), lse.reshape(batch * res_len, dil * ATTN_OUT_WIDTH))


def _mix_kernel(x_ref, u_ref, up_ref, un_ref, qc_ref, gate_ref, o0_ref, o1_ref, o2_ref, l0_ref, l1_ref, l2_ref,
                kv_ref, wpool_ref, pscale_ref, wbp_ref, wba_ref, wbc_ref, wout_ref, h_ref, ubuf, merged, relay, runs,
                *, seq):
    tm = TOKEN_TILE
    i = pl.program_id(1)
    nt = pl.num_programs(1)

    ubuf[0:POOL_HALO, :] = jnp.where(i > 0, up_ref[...], 0.0)
    ubuf[POOL_HALO:POOL_HALO + tm, :] = u_ref[...]
    ubuf[POOL_HALO + tm:, :] = jnp.where(i < nt - 1, un_ref[...], 0.0)
    edge_row = lax.broadcasted_iota(I32, (POOL_EDGE, POOL_GROUP_DIM), 0)
    pos_top = i * tm + edge_row
    pos_bot = i * tm + (tm - POOL_EDGE) + edge_row
    pool_parts = []
    for g, w in enumerate(POOL_WINDOWS):
        cs = slice(g * POOL_GROUP_DIM, (g + 1) * POOL_GROUP_DIM)
        half_w = w // 2
        need = {half_w: tm + POOL_HALO}
        m = half_w
        while m > 1:
            need[m // 2] = need[m] + m // 2
            m //= 2
        src, m = None, 1
        while m < half_w:
            n = need[2 * m]
            dst = runs.at[(m.bit_length() - 1) % 2]
            if src is None:
                dst[0:n, :] = ubuf[0:n, cs] + ubuf[m:m + n, cs]
            else:
                dst[0:n, :] = src[0:n, :] + src[m:m + n, :]
            src, m = dst, 2 * m
        lo = POOL_HALO - half_w
        if src is None:
            acc = ubuf[lo:lo + tm, cs] + ubuf[POOL_HALO:POOL_HALO + tm, cs]
        else:
            acc = src[lo:lo + tm, :] + src[POOL_HALO:POOL_HALO + tm, :]
        cnt_top = (jnp.minimum(pos_top + w // 2, seq) - jnp.maximum(pos_top - w // 2, 0)).astype(F32)
        cnt_bot = (jnp.minimum(pos_bot + w // 2, seq) - jnp.maximum(pos_bot - w // 2, 0)).astype(F32)
        mean = jnp.concatenate([acc[:POOL_EDGE] / cnt_top,
                                acc[POOL_EDGE:tm - POOL_EDGE] * (1.0 / w),
                                acc[tm - POOL_EDGE:] / cnt_bot], axis=0)
        z = mean - ubuf[POOL_HALO:POOL_HALO + tm, cs]
        zp = jnp.dot(z.astype(BF16), wpool_ref[g], preferred_element_type=F32)
        pool_parts.append(zp * pscale_ref[:, cs])
    pool_bf = jnp.concatenate(pool_parts, axis=1).astype(BF16)

    cross_parts = []
    for h in range(CROSS_HEADS):
        cs = slice(h * CROSS_HEAD_DIM, (h + 1) * CROSS_HEAD_DIM)
        kh = kv_ref[0, :, cs]
        vh = kv_ref[0, :, CROSS_WIDTH + h * CROSS_HEAD_DIM:CROSS_WIDTH + (h + 1) * CROSS_HEAD_DIM]
        s = lax.dot_general(qc_ref[:, cs], kh, (((1,), (1,)), ((), ())), preferred_element_type=F32)
        s = s * (1.0 / math.sqrt(CROSS_HEAD_DIM))
        m = jnp.max(s, axis=-1, keepdims=True)
        p = jnp.exp(s - m)
        l = jnp.sum(p, axis=-1, keepdims=True)
        cross_parts.append(jnp.dot(p.astype(BF16), vh, preferred_element_type=F32) / l)
    cross_bf = jnp.concatenate(cross_parts, axis=1).astype(BF16)

    def token_major(ref, slot, dil):
        if dil == 1:
            return ref[...].astype(F32)
        res_rows = tm // dil
        for r in range(dil):
            for hh in range(ATTN_OUT_WIDTH // LANES):
                c0 = r * ATTN_OUT_WIDTH + hh * LANES
                relay[slot, hh, pl.ds(r, res_rows, stride=dil), :] = ref[:, c0:c0 + LANES].astype(F32)
        return jnp.concatenate([relay[slot, hh] for hh in range(ATTN_OUT_WIDTH // LANES)], axis=1)

    dils = [d for _, d in DIL_CONFIGS]
    o0, o1, o2 = [token_major(r, s, d) for r, s, d in zip((o0_ref, o1_ref, o2_ref), (0, 1, 2), dils)]
    l0, l1, l2 = [token_major(r, s, d) for r, s, d in zip((l0_ref, l1_ref, l2_ref), (3, 4, 5), dils)]
    mx = jnp.maximum(jnp.maximum(l0, l1), l2)
    e0, e1, e2 = jnp.exp(l0 - mx), jnp.exp(l1 - mx), jnp.exp(l2 - mx)
    attn = (e0 * o0 + e1 * o1 + e2 * o2) / (e0 + e1 + e2)
    attn_bf = attn.astype(BF16)

    for c in range(D_MODEL // MIX_CHUNK):
        cs = slice(c * MIX_CHUNK, (c + 1) * MIX_CHUNK)
        mrg = gate_ref[:, cs].astype(F32) * jnp.dot(pool_bf, wbp_ref[:, cs], preferred_element_type=F32)
        mrg = mrg + gate_ref[:, D_MODEL + c * MIX_CHUNK:D_MODEL + (c + 1) * MIX_CHUNK].astype(F32) * jnp.dot(
            attn_bf, wba_ref[:, cs], preferred_element_type=F32)
        mrg = mrg + gate_ref[:, 2 * D_MODEL + c * MIX_CHUNK:2 * D_MODEL + (c + 1) * MIX_CHUNK].astype(F32) * jnp.dot(
            cross_bf, wbc_ref[:, cs], preferred_element_type=F32)
        merged[:, cs] = mrg.astype(BF16)
    h_ref[...] = x_ref[...] + jnp.dot(merged[...], wout_ref[...], preferred_element_type=F32)


def _mix(x2d, u, qc, gate, outs, lses, kv3, wpool_bf, pscale, wbp, wba, wbc, wout, batch, seq):
    tm = TOKEN_TILE
    ts = seq // tm
    rows = batch * seq
    hb = tm // POOL_HALO
    n_halo = rows // POOL_HALO

    def row(b, i):
        return (b * ts + i, 0)

    def const(b, i):
        return (0, 0)

    tok = lambda w: pl.BlockSpec((tm, w), row)
    in_specs = [
        tok(D_MODEL),
        tok(POOL_WIDTH),
        pl.BlockSpec((POOL_HALO, POOL_WIDTH), lambda b, i: (jnp.maximum((b * ts + i) * hb - 1, 0), 0)),
        pl.BlockSpec((POOL_HALO, POOL_WIDTH), lambda b, i: (jnp.minimum((b * ts + i + 1) * hb, n_halo - 1), 0)),
        tok(CROSS_WIDTH),
        tok(N_BRANCHES * D_MODEL),
        *[pl.BlockSpec((tm // d, d * ATTN_OUT_WIDTH), row) for _, d in DIL_CONFIGS],
        *[pl.BlockSpec((tm // d, d * ATTN_OUT_WIDTH), row) for _, d in DIL_CONFIGS],
        pl.BlockSpec((1, kv3.shape[1], 2 * CROSS_WIDTH), lambda b, i: (b, 0, 0)),
        pl.BlockSpec((POOL_GROUPS, POOL_GROUP_DIM, POOL_GROUP_DIM), lambda b, i: (0, 0, 0)),
        pl.BlockSpec((1, POOL_WIDTH), const),
        pl.BlockSpec((POOL_WIDTH, D_MODEL), const),
        pl.BlockSpec((ATTN_OUT_WIDTH, D_MODEL), const),
        pl.BlockSpec((CROSS_WIDTH, D_MODEL), const),
        pl.BlockSpec((D_MODEL, D_MODEL), const),
    ]
    return pl.pallas_call(
        functools.partial(_mix_kernel, seq=seq),
        grid=(batch, ts),
        in_specs=in_specs,
        out_specs=pl.BlockSpec((tm, D_MODEL), row),
        out_shape=jax.ShapeDtypeStruct((rows, D_MODEL), F32),
        scratch_shapes=[pltpu.VMEM((tm + 2 * POOL_HALO, POOL_WIDTH), F32), pltpu.VMEM((tm, D_MODEL), BF16),
                        pltpu.VMEM((2 * len(DIL_CONFIGS), ATTN_OUT_WIDTH // LANES, tm, LANES), F32),
                        pltpu.VMEM((2, tm + 2 * POOL_HALO, POOL_GROUP_DIM), F32)],
        compiler_params=pltpu.CompilerParams(dimension_semantics=("parallel", "parallel"), vmem_limit_bytes=VMEM_LIMIT),
        name="mix",
    )(x2d, u, u, u, qc, gate, *outs, *lses, kv3, wpool_bf, pscale, wbp, wba, wbc, wout)


def _route_kernel(h_ref, g_ref, whi_ref, wlo_ref, b_ref, hn_ref, ri_ref, rg_ref, cnt_ref, carry, tri):
    tm = ROUTE_TILE
    step = pl.program_id(0)

    @pl.when(step == 0)
    def _():
        carry[...] = jnp.zeros_like(carry)
        r = lax.broadcasted_iota(I32, (tm, tm), 0)
        c = lax.broadcasted_iota(I32, (tm, tm), 1)
        tri[...] = jnp.where(r < c, 1.0, 0.0).astype(BF16)

    hn = _rms(h_ref[...], g_ref[...])
    hn_ref[...] = _pack_bf16_pair(hn)
    hi = hn.astype(BF16)
    lo = (hn - hi.astype(F32)).astype(BF16)
    logits = (jnp.dot(hi, whi_ref[...], preferred_element_type=F32)
              + jnp.dot(lo, whi_ref[...], preferred_element_type=F32)
              + jnp.dot(hi, wlo_ref[...], preferred_element_type=F32)) + b_ref[...]
    work = logits.T[:N_EXPERTS]
    row = lax.broadcasted_iota(I32, (N_EXPERTS, tm), 0).astype(F32)
    idxs, vals = [], []
    onehot = jnp.zeros((N_EXPERTS, tm), F32)
    for _ in range(TOP_K):
        m = jnp.max(work, axis=0, keepdims=True)
        idx = jnp.min(jnp.where(work == m, row, float(N_EXPERTS)), axis=0, keepdims=True)
        sel = row == idx
        onehot = jnp.where(sel, 1.0, onehot)
        work = jnp.where(sel, -jnp.inf, work)
        idxs.append(idx)
        vals.append(m)
    exps = [jnp.exp(v - vals[0]) for v in vals]
    den = exps[0] + exps[1] + exps[2] + exps[3]
    gates = [e / den for e in exps]

    prefix = jnp.dot(onehot.astype(BF16), tri[...], preferred_element_type=F32) + carry[:, 0:1]
    ranks = [jnp.sum(jnp.where(row == idx, prefix, 0.0), axis=0, keepdims=True) for idx in idxs]
    carry[...] = carry[...] + jnp.sum(onehot, axis=1, keepdims=True)

    row8 = lax.broadcasted_iota(I32, (2 * TOP_K, tm), 0)
    ri = jnp.zeros((2 * TOP_K, tm), F32)
    row128 = lax.broadcasted_iota(I32, (LANES, tm), 0)
    rg = jnp.zeros((LANES, tm), F32)
    for k in range(TOP_K):
        ri = jnp.where(row8 == k, idxs[k], ri)
        ri = jnp.where(row8 == TOP_K + k, ranks[k], ri)
        rg = jnp.where(row128 == k, gates[k], rg)
    ri_ref[...] = ri.astype(I32)
    rg_ref[...] = rg.T
    cnt_ref[...] = carry[...].astype(I32)


def _route(h2d, g, whi, wlo, b):
    rows = h2d.shape[0]
    tm = ROUTE_TILE
    const = lambda i: (0, 0)
    return pl.pallas_call(
        _route_kernel,
        grid=(rows // tm,),
        in_specs=[
            pl.BlockSpec((tm, D_MODEL), lambda i: (i, 0)),
            pl.BlockSpec((1, D_MODEL), const),
            pl.BlockSpec((D_MODEL, LANES), const),
            pl.BlockSpec((D_MODEL, LANES), const),
            pl.BlockSpec((1, LANES), const),
        ],
        out_specs=[
            pl.BlockSpec((tm, PACKED), lambda i: (i, 0)),
            pl.BlockSpec((2 * TOP_K, tm), lambda i: (0, i)),
            pl.BlockSpec((tm, LANES), lambda i: (i, 0)),
            pl.BlockSpec((N_EXPERTS, LANES), const),
        ],
        out_shape=[
            jax.ShapeDtypeStruct((rows, PACKED), I32),
            jax.ShapeDtypeStruct((2 * TOP_K, rows), I32),
            jax.ShapeDtypeStruct((rows, LANES), F32),
            jax.ShapeDtypeStruct((N_EXPERTS, LANES), I32),
        ],
        scratch_shapes=[pltpu.VMEM((N_EXPERTS, LANES), F32), pltpu.VMEM((tm, tm), BF16)],
        compiler_params=pltpu.CompilerParams(dimension_semantics=("arbitrary",), vmem_limit_bytes=VMEM_LIMIT),
        name="route",
    )(h2d, g, whi, wlo, b)


def _sc_mesh():
    return plsc.VectorSubcoreMesh(core_axis_name="c", subcore_axis_name="s")


def _sc_workers():
    info = plsc.get_sparse_core_info()
    return info.num_cores, info.num_cores * info.num_subcores


def _sc_dispatch(rows_packed, slots, n_slots):
    n_tok = rows_packed.shape[0]
    n_cores, n_workers = _sc_workers()
    chunks_per_worker = n_tok // SC_WINDOW // n_workers

    @functools.partial(pl.kernel, out_type=jax.ShapeDtypeStruct((n_slots, PACKED), I32), mesh=_sc_mesh(),
                       scratch_types=[pltpu.VMEM((TOP_K, SC_WINDOW), I32), pltpu.VMEM((SC_WINDOW, PACKED), I32)],
                       name="dispatch")
    def kern(x_hbm, i_hbm, o_hbm, idx_v, rows_v):
        wid = lax.axis_index("s") * n_cores + lax.axis_index("c")

        @pl.loop(0, chunks_per_worker)
        def _(j):
            chunk = wid * chunks_per_worker + j
            pltpu.sync_copy(i_hbm.at[chunk], idx_v)
            pltpu.sync_copy(x_hbm.at[pl.ds(chunk * SC_WINDOW, SC_WINDOW)], rows_v)
            for k in range(TOP_K):
                pltpu.sync_copy(rows_v, o_hbm.at[idx_v.at[k]])

    return kern(rows_packed, slots)


def _sc_combine(ys, slots):
    n_tok = slots.shape[0] * SC_WINDOW
    n_cores, n_workers = _sc_workers()
    chunks_per_worker = n_tok // SC_WINDOW // n_workers

    @functools.partial(pl.kernel, out_type=jax.ShapeDtypeStruct((TOP_K, n_tok, PACKED), I32), mesh=_sc_mesh(),
                       scratch_types=[pltpu.VMEM((TOP_K, SC_WINDOW), I32), pltpu.VMEM((SC_WINDOW, PACKED), I32)],
                       name="combine")
    def kern(y_hbm, i_hbm, o_hbm, idx_v, rows_v):
        wid = lax.axis_index("s") * n_cores + lax.axis_index("c")

        @pl.loop(0, chunks_per_worker)
        def _(j):
            chunk = wid * chunks_per_worker + j
            pltpu.sync_copy(i_hbm.at[chunk], idx_v)
            for k in range(TOP_K):
                pltpu.sync_copy(y_hbm.at[idx_v.at[k]], rows_v)
                pltpu.sync_copy(rows_v, o_hbm.at[k, pl.ds(chunk * SC_WINDOW, SC_WINDOW)])

    return kern(ys, slots)


def _expert_kernel(blk_e_ref, nused_ref, x_ref, wu_ref, bu_ref, wd_ref, bd_ref, o_ref, wu_bf, wd_bf):
    i = pl.program_id(0)
    used = i < nused_ref[0]

    @pl.when(used & ((i == 0) | (blk_e_ref[i] != blk_e_ref[jnp.maximum(i - 1, 0)])))
    def _():
        for c in range(2 * D_FF // CAST_CHUNK):
            cs = slice(c * CAST_CHUNK, (c + 1) * CAST_CHUNK)
            wu_bf[:, cs] = wu_ref[0, :, cs].astype(BF16)
        for c in range(D_MODEL // CAST_CHUNK):
            cs = slice(c * CAST_CHUNK, (c + 1) * CAST_CHUNK)
            wd_bf[:, cs] = wd_ref[0, :, cs].astype(BF16)

    @pl.when(used)
    def _():
        lo, hi = _unpack_bf16_pair(x_ref[...])
        x = jnp.concatenate([lo.astype(BF16), hi.astype(BF16)], axis=1)
        y = jnp.zeros((EXPERT_BLOCK, D_MODEL), F32)
        for c in range(D_FF // FF_CHUNK):
            gs = slice(c * FF_CHUNK, (c + 1) * FF_CHUNK)
            us = slice(D_FF + c * FF_CHUNK, D_FF + (c + 1) * FF_CHUNK)
            gate = jnp.dot(x, wu_bf[:, gs], preferred_element_type=F32) + bu_ref[0, :, gs]
            up = jnp.dot(x, wu_bf[:, us], preferred_element_type=F32) + bu_ref[0, :, us]
            gate = jnp.minimum(gate, SWIGLU_LIMIT)
            up = jnp.clip(up, -SWIGLU_LIMIT, SWIGLU_LIMIT)
            act = (up + 1.0) * (gate * jax.nn.sigmoid(SWIGLU_ALPHA * gate))
            y = y + jnp.dot(act.astype(BF16), wd_bf[gs, :], preferred_element_type=F32)
        o_ref[...] = _pack_bf16_pair(y + bd_ref[0])

    @pl.when(jnp.logical_not(used))
    def _():
        o_ref[...] = jnp.zeros_like(o_ref)


def _experts(xs, blk_e, nused, wu, bu, wd, bd):
    n_slots = xs.shape[0]
    tb = EXPERT_BLOCK
    grid_spec = pltpu.PrefetchScalarGridSpec(
        num_scalar_prefetch=2,
        grid=(n_slots // tb,),
        in_specs=[
            pl.BlockSpec((tb, PACKED), lambda i, be, nu: (i, 0)),
            pl.BlockSpec((1, D_MODEL, 2 * D_FF), lambda i, be, nu: (be[i], 0, 0)),
            pl.BlockSpec((1, 1, 2 * D_FF), lambda i, be, nu: (be[i], 0, 0)),
            pl.BlockSpec((1, D_FF, D_MODEL), lambda i, be, nu: (be[i], 0, 0)),
            pl.BlockSpec((1, 1, D_MODEL), lambda i, be, nu: (be[i], 0, 0)),
        ],
        out_specs=pl.BlockSpec((tb, PACKED), lambda i, be, nu: (i, 0)),
        scratch_shapes=[pltpu.VMEM((D_MODEL, 2 * D_FF), BF16), pltpu.VMEM((D_FF, D_MODEL), BF16)],
    )
    return pl.pallas_call(
        _expert_kernel,
        grid_spec=grid_spec,
        out_shape=jax.ShapeDtypeStruct((n_slots, PACKED), I32),
        compiler_params=pltpu.CompilerParams(dimension_semantics=("arbitrary",), vmem_limit_bytes=VMEM_LIMIT),
        name="experts",
    )(blk_e, nused, xs, wu, bu, wd, bd)


def _final_kernel(h_ref, yg_ref, rg_ref, g_ref, o_ref):
    rg = rg_ref[...]
    lo = jnp.zeros((FINAL_TILE, PACKED), F32)
    hi = jnp.zeros((FINAL_TILE, PACKED), F32)
    for k in range(TOP_K):
        yl, yh = _unpack_bf16_pair(yg_ref[k])
        w = rg[:, k:k + 1]
        lo = lo + w * yl
        hi = hi + w * yh
    x = h_ref[...] + jnp.concatenate([lo, hi], axis=1)
    o_ref[...] = _rms(x, g_ref[...])


def _final(h2d, yg, rg, g):
    rows = h2d.shape[0]
    tm = FINAL_TILE
    return pl.pallas_call(
        _final_kernel,
        grid=(rows // tm,),
        in_specs=[
            pl.BlockSpec((tm, D_MODEL), lambda i: (i, 0)),
            pl.BlockSpec((TOP_K, tm, PACKED), lambda i: (0, i, 0)),
            pl.BlockSpec((tm, LANES), lambda i: (i, 0)),
            pl.BlockSpec((1, D_MODEL), lambda i: (0, 0)),
        ],
        out_specs=pl.BlockSpec((tm, D_MODEL), lambda i: (i, 0)),
        out_shape=jax.ShapeDtypeStruct((rows, D_MODEL), F32),
        compiler_params=pltpu.CompilerParams(dimension_semantics=("parallel",), vmem_limit_bytes=VMEM_LIMIT),
        name="final",
    )(h2d, yg, rg, g)


def _plan(ri, counts, n_tok):
    tb = EXPERT_BLOCK
    n_slots = n_tok * TOP_K + N_EXPERTS * tb
    eid = ri[:TOP_K]
    rank = ri[TOP_K:]
    c = counts[:, 0]
    pc = (c + tb - 1) // tb * tb
    pend = jnp.cumsum(pc)
    pstart = pend - pc
    base = jnp.zeros_like(eid)
    for e in range(N_EXPERTS):
        base = jnp.where(eid == e, pstart[e], base)
    slots = (base + rank).astype(I32)
    slots = slots.reshape(TOP_K, n_tok // SC_WINDOW, SC_WINDOW).transpose(1, 0, 2)
    blk_start = jnp.arange(n_slots // tb, dtype=I32) * tb
    blk_e = jnp.minimum(jnp.sum(blk_start[:, None] >= pend[None, :], axis=1), N_EXPERTS - 1).astype(I32)
    nused = (pend[-1:] // tb).astype(I32)
    return slots, blk_e, nused, n_slots


def _trunk(x, mem, p):
    batch, seq, _ = x.shape
    n_tok = batch * seq
    x2d = x.reshape(n_tok, D_MODEL)
    kv = _memkv(mem.reshape(-1, D_MODEL), p["norm_mem_g"], p["w_mem_kv"])
    kv3 = kv.reshape(batch, mem.shape[1], 2 * CROSS_WIDTH)
    u, a0, a1, a2, qc, gate = _proj(x2d, p["norm_mix_g"], p["w_in"], _rope_tables(seq), seq)
    outs, lses = [], []
    for g, a_grp in enumerate((a0, a1, a2)):
        o, lse = _dilattn(a_grp, batch, seq, g)
        outs.append(o)
        lses.append(lse)
    h = _mix(x2d, u, qc, gate, outs, lses, kv3, p["w_pool"], p["pool_scale"], p["w_br_pool"], p["w_br_attn"],
             p["w_br_cross"], p["w_out"], batch, seq)
    hn, ri, rg, counts = _route(h, p["norm_ffn_g"], p["w_router_hi"], p["w_router_lo"], p["b_router"])
    slots, blk_e, nused, n_slots = _plan(ri, counts, n_tok)
    xs = _sc_dispatch(hn, slots, n_slots)
    ys = _experts(xs, blk_e, nused, p["w_up"], p["b_up"], p["w_down"], p["b_down"])
    yg = _sc_combine(ys, slots)
    out = _final(h, yg, rg, p["norm_final_g"])
    return out.reshape(batch, seq, D_MODEL)


def _prep_params(norm_mix_g, norm_mem_g, w_in, w_pool, pool_scale, w_mem_kv, w_br_pool, w_br_attn, w_br_cross,
                 w_out, norm_ffn_g, w_router, b_router, w_up, b_up, w_down, b_down, norm_final_g):
    wr = w_router[0]
    wr_hi = wr.astype(BF16)
    return dict(
        norm_mix_g=norm_mix_g[0].reshape(1, D_MODEL),
        norm_mem_g=norm_mem_g[0].reshape(1, D_MODEL),
        w_in=w_in[0].astype(BF16),
        w_pool=w_pool[0].astype(BF16),
        pool_scale=pool_scale[0].reshape(1, POOL_WIDTH),
        w_mem_kv=w_mem_kv[0].astype(BF16),
        w_br_pool=w_br_pool[0].astype(BF16),
        w_br_attn=w_br_attn[0].astype(BF16),
        w_br_cross=w_br_cross[0].astype(BF16),
        w_out=w_out[0].astype(BF16),
        norm_ffn_g=norm_ffn_g[0].reshape(1, D_MODEL),
        w_router_hi=jnp.pad(wr_hi, ((0, 0), (0, LANES - N_EXPERTS))),
        w_router_lo=jnp.pad((wr - wr_hi.astype(F32)).astype(BF16), ((0, 0), (0, LANES - N_EXPERTS))),
        b_router=jnp.pad(b_router[0].reshape(1, N_EXPERTS), ((0, 0), (0, LANES - N_EXPERTS))),
        w_up=w_up[0],
        b_up=b_up[0].reshape(N_EXPERTS, 1, 2 * D_FF),
        w_down=w_down[0],
        b_down=b_down[0].reshape(N_EXPERTS, 1, D_MODEL),
        norm_final_g=norm_final_g.reshape(1, D_MODEL),
    )


def kernel(x_prompt, x_sample, mem_prompt, mem_sample, norm_mix_g, norm_mem_g, w_in, w_pool, pool_scale, w_mem_kv,
           w_br_pool, w_br_attn, w_br_cross, w_out, norm_ffn_g, w_router, b_router, w_up, b_up, w_down, b_down,
           norm_final_g):
    p = _prep_params(norm_mix_g, norm_mem_g, w_in, w_pool, pool_scale, w_mem_kv, w_br_pool, w_br_attn, w_br_cross,
                     w_out, norm_ffn_g, w_router, b_router, w_up, b_up, w_down, b_down, norm_final_g)
    y_prompt = _trunk(x_prompt, mem_prompt, p)
    y_sample = _trunk(x_sample, mem_sample, p)
    return (y_prompt, y_sample)
```

```python
import functools
import math

import jax
import jax.numpy as jnp
from jax import lax
from jax.experimental import pallas as pl
from jax.experimental.pallas import tpu as pltpu
from jax.experimental.pallas import tpu_sc as plsc

F32 = jnp.float32
BF16 = jnp.bfloat16
I32 = jnp.int32
U32 = jnp.uint32

D_MODEL = 1024
POOL_GROUPS = 4
POOL_WIDTH = 512
POOL_GROUP_DIM = 128
POOL_WINDOWS = (2, 4, 8, 16)
POOL_HALO = 16
POOL_EDGE = 8
HEAD_DIM = 64
DIL_CONFIGS = ((128, 1), (512, 4), (2048, 16))
HEADS_PER_DIL = 4
ATTN_WIDTH = 768
ATTN_OUT_WIDTH = 256
BAND_BLOCK = 64
ROPE_DIM = 16
ROPE_THETA = 500000.0
CROSS_HEADS = 4
CROSS_HEAD_DIM = 128
CROSS_WIDTH = 512
N_BRANCHES = 3
QKV_WIDTH = 3 * ATTN_WIDTH
IN_WIDTH = POOL_WIDTH + QKV_WIDTH + CROSS_WIDTH + N_BRANCHES * D_MODEL
N_EXPERTS = 32
TOP_K = 4
D_FF = 1024
SWIGLU_LIMIT = 7.0
SWIGLU_ALPHA = 1.702
EPS = 1e-5
NEG_INF = -1e30

LANES = 128
PACKED = D_MODEL // 2
TOKEN_TILE = 512
FINAL_TILE = 1024
ROUTE_TILE = 1024
PROJ_CHUNK = 512
MIX_CHUNK = 256
FF_CHUNK = 512
CAST_CHUNK = 512
EXPERT_BLOCK = 1024
SC_WINDOW = 128
VMEM_LIMIT = 52 * 1024 * 1024


def _rms(x, g):
    r = lax.rsqrt(jnp.mean(x * x, axis=-1, keepdims=True) + EPS)
    return x * r * g


def _pack_bf16_pair(x):
    bits = lax.bitcast_convert_type(x.astype(BF16).astype(F32), U32)
    packed = (bits[:, :PACKED] >> 16) | bits[:, PACKED:]
    return lax.bitcast_convert_type(packed, I32)


def _unpack_bf16_pair(w):
    u = lax.bitcast_convert_type(w, U32)
    lo = lax.bitcast_convert_type(u << 16, F32)
    hi = lax.bitcast_convert_type(u & jnp.uint32(0xFFFF0000), F32)
    return lo, hi


def _memkv_kernel(mem_ref, g_ref, w_ref, o_ref):
    xn = _rms(mem_ref[...], g_ref[...]).astype(BF16)
    o_ref[...] = jnp.dot(xn, w_ref[...], preferred_element_type=F32).astype(BF16)


def _memkv(mem2d, g, w_bf):
    rows = mem2d.shape[0]
    tm = 256
    return pl.pallas_call(
        _memkv_kernel,
        grid=(rows // tm,),
        in_specs=[
            pl.BlockSpec((tm, D_MODEL), lambda i: (i, 0)),
            pl.BlockSpec((1, D_MODEL), lambda i: (0, 0)),
            pl.BlockSpec((D_MODEL, 2 * CROSS_WIDTH), lambda i: (0, 0)),
        ],
        out_specs=pl.BlockSpec((tm, 2 * CROSS_WIDTH), lambda i: (i, 0)),
        out_shape=jax.ShapeDtypeStruct((rows, 2 * CROSS_WIDTH), BF16),
        compiler_params=pltpu.CompilerParams(dimension_semantics=("parallel",)),
        name="memkv",
    )(mem2d, g, w_bf)


_O_QKV = POOL_WIDTH
_O_QC = _O_QKV + QKV_WIDTH
_O_GATE = _O_QC + CROSS_WIDTH


def _proj_kernel(x_ref, g_ref, w_ref, tab_ref, u_ref, a0_ref, a1_ref, a2_ref, qc_ref, gate_ref, xcols):
    tm = TOKEN_TILE
    n_cols = D_MODEL // LANES
    g = g_ref[...]
    xn = _rms(x_ref[...], g).astype(BF16)
    for c in range(n_cols):
        xcols[c] = x_ref[:, c * LANES:(c + 1) * LANES]

    def mm(lhs, c0, width=PROJ_CHUNK):
        return jnp.dot(lhs, w_ref[:, c0:c0 + width], preferred_element_type=F32)

    for j in range(POOL_WIDTH // PROJ_CHUNK):
        u_ref[:, j * PROJ_CHUNK:(j + 1) * PROJ_CHUNK] = mm(xn, j * PROJ_CHUNK)

    for grp, a_ref in enumerate((a0_ref, a1_ref, a2_ref)):
        dil = DIL_CONFIGS[grp][1]
        res_rows = tm // dil
        if dil == 1:
            lhs = xn
        else:
            xp = jnp.concatenate(
                [jnp.concatenate([xcols[c, pl.ds(r, res_rows, stride=dil), :] for c in range(n_cols)], axis=1)
                 for r in range(dil)], axis=0)
            lhs = _rms(xp, g).astype(BF16)
        cos, s1, s2 = tab_ref[0, grp], tab_ref[1, grp], tab_ref[2, grp]
        for which in range(3):
            t = mm(lhs, _O_QKV + which * ATTN_WIDTH + grp * ATTN_OUT_WIDTH, ATTN_OUT_WIDTH)
            if which < 2:
                halves = []
                for hh in range(ATTN_OUT_WIDTH // LANES):
                    th = t[:, hh * LANES:(hh + 1) * LANES]
                    th = (th * cos + pltpu.roll(th, LANES - ROPE_DIM // 2, axis=1) * s1
                          + pltpu.roll(th, ROPE_DIM // 2, axis=1) * s2)
                    halves.append(th)
                t = jnp.concatenate(halves, axis=1)
                if which == 0:
                    t = t * (1.0 / math.sqrt(HEAD_DIM))
            tb = t.astype(BF16)
            for r in range(dil):
                c0 = (r * 3 + which) * ATTN_OUT_WIDTH
                a_ref[:, c0:c0 + ATTN_OUT_WIDTH] = tb[r * res_rows:(r + 1) * res_rows, :]

    for j in range(CROSS_WIDTH // PROJ_CHUNK):
        qc_ref[:, j * PROJ_CHUNK:(j + 1) * PROJ_CHUNK] = mm(xn, _O_QC + j * PROJ_CHUNK).astype(BF16)

    for j in range(N_BRANCHES * D_MODEL // PROJ_CHUNK):
        gate_ref[:, j * PROJ_CHUNK:(j + 1) * PROJ_CHUNK] = jax.nn.sigmoid(mm(xn, _O_GATE + j * PROJ_CHUNK)).astype(BF16)


def _proj(x2d, g, w_in_bf, tabs, seq):
    rows = x2d.shape[0]
    tm = TOKEN_TILE
    tiles_per_seq = seq // tm
    n_grp = len(DIL_CONFIGS)
    a_specs = [pl.BlockSpec((tm // d, d * ATTN_WIDTH), lambda i: (i, 0)) for _, d in DIL_CONFIGS]
    a_shapes = [jax.ShapeDtypeStruct((rows // d, d * ATTN_WIDTH), BF16) for _, d in DIL_CONFIGS]
    return pl.pallas_call(
        _proj_kernel,
        grid=(rows // tm,),
        in_specs=[
            pl.BlockSpec((tm, D_MODEL), lambda i: (i, 0)),
            pl.BlockSpec((1, D_MODEL), lambda i: (0, 0)),
            pl.BlockSpec((D_MODEL, IN_WIDTH), lambda i: (0, 0), pipeline_mode=pl.Buffered(1)),
            pl.BlockSpec((3, n_grp, tm, LANES), lambda i: (0, 0, i % tiles_per_seq, 0)),
        ],
        out_specs=[
            pl.BlockSpec((tm, POOL_WIDTH), lambda i: (i, 0)),
            *a_specs,
            pl.BlockSpec((tm, CROSS_WIDTH), lambda i: (i, 0)),
            pl.BlockSpec((tm, N_BRANCHES * D_MODEL), lambda i: (i, 0)),
        ],
        out_shape=[
            jax.ShapeDtypeStruct((rows, POOL_WIDTH), F32),
            *a_shapes,
            jax.ShapeDtypeStruct((rows, CROSS_WIDTH), BF16),
            jax.ShapeDtypeStruct((rows, N_BRANCHES * D_MODEL), BF16),
        ],
        scratch_shapes=[pltpu.VMEM((D_MODEL // LANES, tm, LANES), F32)],
        compiler_params=pltpu.CompilerParams(dimension_semantics=("parallel",), vmem_limit_bytes=VMEM_LIMIT),
        name="proj",
    )(x2d, g, w_in_bf, tabs)


def _rope_tables(seq):
    pos = jnp.arange(seq, dtype=F32)
    inv = jnp.power(jnp.float32(ROPE_THETA), -jnp.arange(0, ROPE_DIM, 2, dtype=F32) / ROPE_DIM)
    ang = pos[:, None] * inv[None, :]
    half = ROPE_DIM // 2
    j = jnp.arange(LANES) % HEAD_DIM
    cos_l = jnp.cos(ang)[:, j % half]
    sin_l = jnp.sin(ang)[:, j % half]
    kinds = jnp.stack([
        jnp.where(j[None, :] < ROPE_DIM, cos_l, 1.0),
        jnp.where(j[None, :] < half, -sin_l, 0.0),
        jnp.where((j[None, :] >= half) & (j[None, :] < ROPE_DIM), sin_l, 0.0),
    ]).astype(F32)
    tm = TOKEN_TILE
    per_group = []
    for _, d in DIL_CONFIGS:
        t = kinds.reshape(3, seq // tm, tm // d, d, LANES).transpose(0, 1, 3, 2, 4).reshape(3, seq, LANES)
        per_group.append(t)
    return jnp.stack(per_group, axis=1)


def _dilattn_kernel(a_ref, bias_ref, o_ref, lse_ref, *, res_len, tq, win, n_res):
    t = pl.program_id(2)
    nblk = tq // BAND_BLOCK
    stacked = HEADS_PER_DIL * BAND_BLOCK
    lane_head = lax.broadcasted_iota(I32, (BAND_BLOCK, ATTN_OUT_WIDTH), 1) // HEAD_DIM
    head_sel = [lane_head == h for h in range(HEADS_PER_DIL)]
    head_mask = [jnp.where(sel, 1.0, 0.0).astype(BF16) for sel in head_sel]
    ones = jnp.ones((win, LANES), BF16)
    blocks = [(rr, n) for rr in range(n_res) for n in range(nblk)]

    s_parts, v_wins = [], []
    for rr, n in blocks:
        c0 = rr * ATTN_WIDTH
        qpos0 = pl.multiple_of(t * tq + n * BAND_BLOCK, BAND_BLOCK)
        start = pl.multiple_of(jnp.clip(qpos0 - BAND_BLOCK, 0, res_len - win), BAND_BLOCK)
        qb = a_ref[0, pl.ds(qpos0, BAND_BLOCK), c0:c0 + ATTN_OUT_WIDTH]
        kw = a_ref[0, pl.ds(start, win), c0 + ATTN_OUT_WIDTH:c0 + 2 * ATTN_OUT_WIDTH]
        v_wins.append(a_ref[0, pl.ds(start, win), c0 + 2 * ATTN_OUT_WIDTH:c0 + 3 * ATTN_OUT_WIDTH])
        qs = jnp.concatenate([qb * hm for hm in head_mask], axis=0)
        s = lax.dot_general(qs, kw, (((1,), (1,)), ((), ())), preferred_element_type=F32)
        s_parts.append(s + bias_ref[(qpos0 - start) // BAND_BLOCK])
    s_all = jnp.concatenate(s_parts, axis=0)
    m_all = jnp.max(s_all, axis=-1, keepdims=True)
    p_all = jnp.exp(s_all - m_all).astype(BF16)

    for i, (rr, n) in enumerate(blocks):
        p = p_all[i * stacked:(i + 1) * stacked]
        o_st = jnp.dot(p, v_wins[i], preferred_element_type=F32)
        l_st = jnp.dot(p, ones, preferred_element_type=F32)
        m_st = m_all[i * stacked:(i + 1) * stacked]
        o = jnp.zeros((BAND_BLOCK, ATTN_OUT_WIDTH), F32)
        l = jnp.ones((BAND_BLOCK, ATTN_OUT_WIDTH), F32)
        m = jnp.zeros((BAND_BLOCK, ATTN_OUT_WIDTH), F32)
        for h in range(HEADS_PER_DIL):
            rs = slice(h * BAND_BLOCK, (h + 1) * BAND_BLOCK)
            o = jnp.where(head_sel[h], o_st[rs], o)
            l = jnp.where(head_sel[h], jnp.concatenate([l_st[rs], l_st[rs]], axis=1), l)
            m = jnp.where(head_sel[h], m_st[rs], m)
        rows = slice(n * BAND_BLOCK, (n + 1) * BAND_BLOCK)
        cols = slice(rr * ATTN_OUT_WIDTH, (rr + 1) * ATTN_OUT_WIDTH)
        o_ref[0, rows, cols] = (o / l).astype(BF16)
        lse_ref[0, rows, cols] = m + jnp.log(l)


ATTN_BLOCKS_PER_STEP = 64


def _dilattn(a_grp, batch, seq, group):
    window, dil = DIL_CONFIGS[group]
    half = window // (2 * dil)
    res_len = seq // dil
    win = min(3 * BAND_BLOCK, res_len)
    tq = min(res_len, ATTN_BLOCKS_PER_STEP * BAND_BLOCK)
    n_res = min(dil, ATTN_BLOCKS_PER_STEP * BAND_BLOCK // tq)
    qkv3 = a_grp.reshape(batch, res_len, dil * ATTN_WIDTH)
    kern = functools.partial(_dilattn_kernel, res_len=res_len, tq=tq, win=win, n_res=n_res)
    out_spec = pl.BlockSpec((1, tq, n_res * ATTN_OUT_WIDTH), lambda b, r, t: (b, t, r))
    n_off = win // BAND_BLOCK
    stacked = HEADS_PER_DIL * BAND_BLOCK
    key_minus_query = (jnp.arange(win)[None, None, :] - (jnp.arange(stacked) % BAND_BLOCK)[None, :, None]
                       - BAND_BLOCK * jnp.arange(n_off)[:, None, None])
    bias = jnp.where(jnp.abs(key_minus_query) <= half, 0.0, NEG_INF).astype(F32)
    o, lse = pl.pallas_call(
        kern,
        grid=(batch, dil // n_res, res_len // tq),
        in_specs=[pl.BlockSpec((1, res_len, n_res * ATTN_WIDTH), lambda b, r, t: (b, 0, r)),
                  pl.BlockSpec((n_off, stacked, win), lambda b, r, t: (0, 0, 0))],
        out_specs=[out_spec, out_spec],
        out_shape=[
            jax.ShapeDtypeStruct((batch, res_len, dil * ATTN_OUT_WIDTH), BF16),
            jax.ShapeDtypeStruct((batch, res_len, dil * ATTN_OUT_WIDTH), F32),
        ],
        compiler_params=pltpu.CompilerParams(dimension_semantics=("parallel", "parallel", "parallel"),
                                             vmem_limit_bytes=VMEM_LIMIT),
        name=f"dilattn{group}",
    )(qkv3, bias)
    return (o.reshape(batch * res_len, dil * ATTN_OUT_WIDTH), lse.reshape(batch * res_len, dil * ATTN_OUT_WIDTH))


def _mix_kernel(x_ref, u_ref, up_ref, un_ref, qc_ref, gate_ref, o0_ref, o1_ref, o2_ref, l0_ref, l1_ref, l2_ref,
                kv_ref, wpool_ref, pscale_ref, wbp_ref, wba_ref, wbc_ref, wout_ref, h_ref, ubuf, merged, relay, runs,
                *, seq):
    tm = TOKEN_TILE
    i = pl.program_id(1)
    nt = pl.num_programs(1)

    ubuf[0:POOL_HALO, :] = jnp.where(i > 0, up_ref[...], 0.0)
    ubuf[POOL_HALO:POOL_HALO + tm, :] = u_ref[...]
    ubuf[POOL_HALO + tm:, :] = jnp.where(i < nt - 1, un_ref[...], 0.0)
    edge_row = lax.broadcasted_iota(I32, (POOL_EDGE, POOL_GROUP_DIM), 0)
    pos_top = i * tm + edge_row
    pos_bot = i * tm + (tm - POOL_EDGE) + edge_row
    pool_parts = []
    for g, w in enumerate(POOL_WINDOWS):
        cs = slice(g * POOL_GROUP_DIM, (g + 1) * POOL_GROUP_DIM)
        half_w = w // 2
        need = {half_w: tm + POOL_HALO}
        m = half_w
        while m > 1:
            need[m // 2] = need[m] + m // 2
            m //= 2
        src, m = None, 1
        while m < half_w:
            n = need[2 * m]
            dst = runs.at[(m.bit_length() - 1) % 2]
            if src is None:
                dst[0:n, :] = ubuf[0:n, cs] + ubuf[m:m + n, cs]
            else:
                dst[0:n, :] = src[0:n, :] + src[m:m + n, :]
            src, m = dst, 2 * m
        lo = POOL_HALO - half_w
        if src is None:
            acc = ubuf[lo:lo + tm, cs] + ubuf[POOL_HALO:POOL_HALO + tm, cs]
        else:
            acc = src[lo:lo + tm, :] + src[POOL_HALO:POOL_HALO + tm, :]
        cnt_top = (jnp.minimum(pos_top + w // 2, seq) - jnp.maximum(pos_top - w // 2, 0)).astype(F32)
        cnt_bot = (jnp.minimum(pos_bot + w // 2, seq) - jnp.maximum(pos_bot - w // 2, 0)).astype(F32)
        mean = jnp.concatenate([acc[:POOL_EDGE] / cnt_top,
                                acc[POOL_EDGE:tm - POOL_EDGE] * (1.0 / w),
                                acc[tm - POOL_EDGE:] / cnt_bot], axis=0)
        z = mean - ubuf[POOL_HALO:POOL_HALO + tm, cs]
        zp = jnp.dot(z.astype(BF16), wpool_ref[g], preferred_element_type=F32)
        pool_parts.append(zp * pscale_ref[:, cs])
    pool_bf = jnp.concatenate(pool_parts, axis=1).astype(BF16)

    cross_parts = []
    for h in range(CROSS_HEADS):
        cs = slice(h * CROSS_HEAD_DIM, (h + 1) * CROSS_HEAD_DIM)
        kh = kv_ref[0, :, cs]
        vh = kv_ref[0, :, CROSS_WIDTH + h * CROSS_HEAD_DIM:CROSS_WIDTH + (h + 1) * CROSS_HEAD_DIM]
        s = lax.dot_general(qc_ref[:, cs], kh, (((1,), (1,)), ((), ())), preferred_element_type=F32)
        s = s * (1.0 / math.sqrt(CROSS_HEAD_DIM))
        m = jnp.max(s, axis=-1, keepdims=True)
        p = jnp.exp(s - m)
        l = jnp.sum(p, axis=-1, keepdims=True)
        cross_parts.append(jnp.dot(p.astype(BF16), vh, preferred_element_type=F32) / l)
    cross_bf = jnp.concatenate(cross_parts, axis=1).astype(BF16)

    def token_major(ref, slot, dil):
        if dil == 1:
            return ref[...].astype(F32)
        res_rows = tm // dil
        for r in range(dil):
            for hh in range(ATTN_OUT_WIDTH // LANES):
                c0 = r * ATTN_OUT_WIDTH + hh * LANES
                relay[slot, hh, pl.ds(r, res_rows, stride=dil), :] = ref[:, c0:c0 + LANES].astype(F32)
        return jnp.concatenate([relay[slot, hh] for hh in range(ATTN_OUT_WIDTH // LANES)], axis=1)

    dils = [d for _, d in DIL_CONFIGS]
    o0, o1, o2 = [token_major(r, s, d) for r, s, d in zip((o0_ref, o1_ref, o2_ref), (0, 1, 2), dils)]
    l0, l1, l2 = [token_major(r, s, d) for r, s, d in zip((l0_ref, l1_ref, l2_ref), (3, 4, 5), dils)]
    mx = jnp.maximum(jnp.maximum(l0, l1), l2)
    e0, e1, e2 = jnp.exp(l0 - mx), jnp.exp(l1 - mx), jnp.exp(l2 - mx)
    attn = (e0 * o0 + e1 * o1 + e2 * o2) / (e0 + e1 + e2)
    attn_bf = attn.astype(BF16)

    for c in range(D_MODEL // MIX_CHUNK):
        cs = slice(c * MIX_CHUNK, (c + 1) * MIX_CHUNK)
        mrg = gate_ref[:, cs].astype(F32) * jnp.dot(pool_bf, wbp_ref[:, cs], preferred_element_type=F32)
        mrg = mrg + gate_ref[:, D_MODEL + c * MIX_CHUNK:D_MODEL + (c + 1) * MIX_CHUNK].astype(F32) * jnp.dot(
            attn_bf, wba_ref[:, cs], preferred_element_type=F32)
        mrg = mrg + gate_ref[:, 2 * D_MODEL + c * MIX_CHUNK:2 * D_MODEL + (c + 1) * MIX_CHUNK].astype(F32) * jnp.dot(
            cross_bf, wbc_ref[:, cs], preferred_element_type=F32)
        merged[:, cs] = mrg.astype(BF16)
    h_ref[...] = x_ref[...] + jnp.dot(merged[...], wout_ref[...], preferred_element_type=F32)


def _mix(x2d, u, qc, gate, outs, lses, kv3, wpool_bf, pscale, wbp, wba, wbc, wout, batch, seq):
    tm = TOKEN_TILE
    ts = seq // tm
    rows = batch * seq
    hb = tm // POOL_HALO
    n_halo = rows // POOL_HALO

    def row(b, i):
        return (b * ts + i, 0)

    def const(b, i):
        return (0, 0)

    tok = lambda w: pl.BlockSpec((tm, w), row)
    in_specs = [
        tok(D_MODEL),
        tok(POOL_WIDTH),
        pl.BlockSpec((POOL_HALO, POOL_WIDTH), lambda b, i: (jnp.maximum((b * ts + i) * hb - 1, 0), 0)),
        pl.BlockSpec((POOL_HALO, POOL_WIDTH), lambda b, i: (jnp.minimum((b * ts + i + 1) * hb, n_halo - 1), 0)),
        tok(CROSS_WIDTH),
        tok(N_BRANCHES * D_MODEL),
        *[pl.BlockSpec((tm // d, d * ATTN_OUT_WIDTH), row) for _, d in DIL_CONFIGS],
        *[pl.BlockSpec((tm // d, d * ATTN_OUT_WIDTH), row) for _, d in DIL_CONFIGS],
        pl.BlockSpec((1, kv3.shape[1], 2 * CROSS_WIDTH), lambda b, i: (b, 0, 0)),
        pl.BlockSpec((POOL_GROUPS, POOL_GROUP_DIM, POOL_GROUP_DIM), lambda b, i: (0, 0, 0)),
        pl.BlockSpec((1, POOL_WIDTH), const),
        pl.BlockSpec((POOL_WIDTH, D_MODEL), const),
        pl.BlockSpec((ATTN_OUT_WIDTH, D_MODEL), const),
        pl.BlockSpec((CROSS_WIDTH, D_MODEL), const),
        pl.BlockSpec((D_MODEL, D_MODEL), const),
    ]
    return pl.pallas_call(
        functools.partial(_mix_kernel, seq=seq),
        grid=(batch, ts),
        in_specs=in_specs,
        out_specs=pl.BlockSpec((tm, D_MODEL), row),
        out_shape=jax.ShapeDtypeStruct((rows, D_MODEL), F32),
        scratch_shapes=[pltpu.VMEM((tm + 2 * POOL_HALO, POOL_WIDTH), F32), pltpu.VMEM((tm, D_MODEL), BF16),
                        pltpu.VMEM((2 * len(DIL_CONFIGS), ATTN_OUT_WIDTH // LANES, tm, LANES), F32),
                        pltpu.VMEM((2, tm + 2 * POOL_HALO, POOL_GROUP_DIM), F32)],
        compiler_params=pltpu.CompilerParams(dimension_semantics=("parallel", "parallel"), vmem_limit_bytes=VMEM_LIMIT),
        name="mix",
    )(x2d, u, u, u, qc, gate, *outs, *lses, kv3, wpool_bf, pscale, wbp, wba, wbc, wout)


def _route_kernel(h_ref, g_ref, whi_ref, wlo_ref, b_ref, hn_ref, ri_ref, rg_ref, cnt_ref, carry, tri):
    tm = ROUTE_TILE
    step = pl.program_id(0)

    @pl.when(step == 0)
    def _():
        carry[...] = jnp.zeros_like(carry)
        r = lax.broadcasted_iota(I32, (tm, tm), 0)
        c = lax.broadcasted_iota(I32, (tm, tm), 1)
        tri[...] = jnp.where(r < c, 1.0, 0.0).astype(BF16)

    hn = _rms(h_ref[...], g_ref[...])
    hn_ref[...] = _pack_bf16_pair(hn)
    hi = hn.astype(BF16)
    lo = (hn - hi.astype(F32)).astype(BF16)
    logits = (jnp.dot(hi, whi_ref[...], preferred_element_type=F32)
              + jnp.dot(lo, whi_ref[...], preferred_element_type=F32)
              + jnp.dot(hi, wlo_ref[...], preferred_element_type=F32)) + b_ref[...]
    work = logits.T[:N_EXPERTS]
    row = lax.broadcasted_iota(I32, (N_EXPERTS, tm), 0).astype(F32)
    idxs, vals = [], []
    onehot = jnp.zeros((N_EXPERTS, tm), F32)
    for _ in range(TOP_K):
        m = jnp.max(work, axis=0, keepdims=True)
        idx = jnp.min(jnp.where(work == m, row, float(N_EXPERTS)), axis=0, keepdims=True)
        sel = row == idx
        onehot = jnp.where(sel, 1.0, onehot)
        work = jnp.where(sel, -jnp.inf, work)
        idxs.append(idx)
        vals.append(m)
    exps = [jnp.exp(v - vals[0]) for v in vals]
    den = exps[0] + exps[1] + exps[2] + exps[3]
    gates = [e / den for e in exps]

    prefix = jnp.dot(onehot.astype(BF16), tri[...], preferred_element_type=F32) + carry[:, 0:1]
    ranks = [jnp.sum(jnp.where(row == idx, prefix, 0.0), axis=0, keepdims=True) for idx in idxs]
    carry[...] = carry[...] + jnp.sum(onehot, axis=1, keepdims=True)

    row8 = lax.broadcasted_iota(I32, (2 * TOP_K, tm), 0)
    ri = jnp.zeros((2 * TOP_K, tm), F32)
    row128 = lax.broadcasted_iota(I32, (LANES, tm), 0)
    rg = jnp.zeros((LANES, tm), F32)
    for k in range(TOP_K):
        ri = jnp.where(row8 == k, idxs[k], ri)
        ri = jnp.where(row8 == TOP_K + k, ranks[k], ri)
        rg = jnp.where(row128 == k, gates[k], rg)
    ri_ref[...] = ri.astype(I32)
    rg_ref[...] = rg.T
    cnt_ref[...] = carry[...].astype(I32)


def _route(h2d, g, whi, wlo, b):
    rows = h2d.shape[0]
    tm = ROUTE_TILE
    const = lambda i: (0, 0)
    return pl.pallas_call(
        _route_kernel,
        grid=(rows // tm,),
        in_specs=[
            pl.BlockSpec((tm, D_MODEL), lambda i: (i, 0)),
            pl.BlockSpec((1, D_MODEL), const),
            pl.BlockSpec((D_MODEL, LANES), const),
            pl.BlockSpec((D_MODEL, LANES), const),
            pl.BlockSpec((1, LANES), const),
        ],
        out_specs=[
            pl.BlockSpec((tm, PACKED), lambda i: (i, 0)),
            pl.BlockSpec((2 * TOP_K, tm), lambda i: (0, i)),
            pl.BlockSpec((tm, LANES), lambda i: (i, 0)),
            pl.BlockSpec((N_EXPERTS, LANES), const),
        ],
        out_shape=[
            jax.ShapeDtypeStruct((rows, PACKED), I32),
            jax.ShapeDtypeStruct((2 * TOP_K, rows), I32),
            jax.ShapeDtypeStruct((rows, LANES), F32),
            jax.ShapeDtypeStruct((N_EXPERTS, LANES), I32),
        ],
        scratch_shapes=[pltpu.VMEM((N_EXPERTS, LANES), F32), pltpu.VMEM((tm, tm), BF16)],
        compiler_params=pltpu.CompilerParams(dimension_semantics=("arbitrary",), vmem_limit_bytes=VMEM_LIMIT),
        name="route",
    )(h2d, g, whi, wlo, b)


def _sc_mesh():
    return plsc.VectorSubcoreMesh(core_axis_name="c", subcore_axis_name="s")


def _sc_workers():
    info = plsc.get_sparse_core_info()
    return info.num_cores, info.num_cores * info.num_subcores


def _sc_dispatch(rows_packed, slots, n_slots):
    n_tok = rows_packed.shape[0]
    n_cores, n_workers = _sc_workers()
    chunks_per_worker = n_tok // SC_WINDOW // n_workers

    @functools.partial(pl.kernel, out_type=jax.ShapeDtypeStruct((n_slots, PACKED), I32), mesh=_sc_mesh(),
                       scratch_types=[pltpu.VMEM((TOP_K, SC_WINDOW), I32), pltpu.VMEM((SC_WINDOW, PACKED), I32)],
                       name="dispatch")
    def kern(x_hbm, i_hbm, o_hbm, idx_v, rows_v):
        wid = lax.axis_index("s") * n_cores + lax.axis_index("c")

        @pl.loop(0, chunks_per_worker)
        def _(j):
            chunk = wid * chunks_per_worker + j
            pltpu.sync_copy(i_hbm.at[chunk], idx_v)
            pltpu.sync_copy(x_hbm.at[pl.ds(chunk * SC_WINDOW, SC_WINDOW)], rows_v)
            for k in range(TOP_K):
                pltpu.sync_copy(rows_v, o_hbm.at[idx_v.at[k]])

    return kern(rows_packed, slots)


def _sc_combine(ys, slots):
    n_tok = slots.shape[0] * SC_WINDOW
    n_cores, n_workers = _sc_workers()
    chunks_per_worker = n_tok // SC_WINDOW // n_workers

    @functools.partial(pl.kernel, out_type=jax.ShapeDtypeStruct((TOP_K, n_tok, PACKED), I32), mesh=_sc_mesh(),
                       scratch_types=[pltpu.VMEM((TOP_K, SC_WINDOW), I32), pltpu.VMEM((SC_WINDOW, PACKED), I32)],
                       name="combine")
    def kern(y_hbm, i_hbm, o_hbm, idx_v, rows_v):
        wid = lax.axis_index("s") * n_cores + lax.axis_index("c")

        @pl.loop(0, chunks_per_worker)
        def _(j):
            chunk = wid * chunks_per_worker + j
            pltpu.sync_copy(i_hbm.at[chunk], idx_v)
            for k in range(TOP_K):
                pltpu.sync_copy(y_hbm.at[idx_v.at[k]], rows_v)
                pltpu.sync_copy(rows_v, o_hbm.at[k, pl.ds(chunk * SC_WINDOW, SC_WINDOW)])

    return kern(ys, slots)


def _expert_kernel(blk_e_ref, nused_ref, x_ref, wu_ref, bu_ref, wd_ref, bd_ref, o_ref, wu_bf, wd_bf):
    i = pl.program_id(0)
    used = i < nused_ref[0]

    @pl.when(used & ((i == 0) | (blk_e_ref[i] != blk_e_ref[jnp.maximum(i - 1, 0)])))
    def _():
        for c in range(2 * D_FF // CAST_CHUNK):
            cs = slice(c * CAST_CHUNK, (c + 1) * CAST_CHUNK)
            wu_bf[:, cs] = wu_ref[0, :, cs].astype(BF16)
        for c in range(D_MODEL // CAST_CHUNK):
            cs = slice(c * CAST_CHUNK, (c + 1) * CAST_CHUNK)
            wd_bf[:, cs] = wd_ref[0, :, cs].astype(BF16)

    @pl.when(used)
    def _():
        lo, hi = _unpack_bf16_pair(x_ref[...])
        x = jnp.concatenate([lo.astype(BF16), hi.astype(BF16)], axis=1)
        y = jnp.zeros((EXPERT_BLOCK, D_MODEL), F32)
        for c in range(D_FF // FF_CHUNK):
            gs = slice(c * FF_CHUNK, (c + 1) * FF_CHUNK)
            us = slice(D_FF + c * FF_CHUNK, D_FF + (c + 1) * FF_CHUNK)
            gate = jnp.dot(x, wu_bf[:, gs], preferred_element_type=F32) + bu_ref[0, :, gs]
            up = jnp.dot(x, wu_bf[:, us], preferred_element_type=F32) + bu_ref[0, :, us]
            gate = jnp.minimum(gate, SWIGLU_LIMIT)
            up = jnp.clip(up, -SWIGLU_LIMIT, SWIGLU_LIMIT)
            act = (up + 1.0) * (gate * jax.nn.sigmoid(SWIGLU_ALPHA * gate))
            y = y + jnp.dot(act.astype(BF16), wd_bf[gs, :], preferred_element_type=F32)
        o_ref[...] = _pack_bf16_pair(y + bd_ref[0])

    @pl.when(jnp.logical_not(used))
    def _():
        o_ref[...] = jnp.zeros_like(o_ref)


def _experts(xs, blk_e, nused, wu, bu, wd, bd):
    n_slots = xs.shape[0]
    tb = EXPERT_BLOCK
    grid_spec = pltpu.PrefetchScalarGridSpec(
        num_scalar_prefetch=2,
        grid=(n_slots // tb,),
        in_specs=[
            pl.BlockSpec((tb, PACKED), lambda i, be, nu: (i, 0)),
            pl.BlockSpec((1, D_MODEL, 2 * D_FF), lambda i, be, nu: (be[i], 0, 0)),
            pl.BlockSpec((1, 1, 2 * D_FF), lambda i, be, nu: (be[i], 0, 0)),
            pl.BlockSpec((1, D_FF, D_MODEL), lambda i, be, nu: (be[i], 0, 0)),
            pl.BlockSpec((1, 1, D_MODEL), lambda i, be, nu: (be[i], 0, 0)),
        ],
        out_specs=pl.BlockSpec((tb, PACKED), lambda i, be, nu: (i, 0)),
        scratch_shapes=[pltpu.VMEM((D_MODEL, 2 * D_FF), BF16), pltpu.VMEM((D_FF, D_MODEL), BF16)],
    )
    return pl.pallas_call(
        _expert_kernel,
        grid_spec=grid_spec,
        out_shape=jax.ShapeDtypeStruct((n_slots, PACKED), I32),
        compiler_params=pltpu.CompilerParams(dimension_semantics=("arbitrary",), vmem_limit_bytes=VMEM_LIMIT),
        name="experts",
    )(blk_e, nused, xs, wu, bu, wd, bd)


def _final_kernel(h_ref, yg_ref, rg_ref, g_ref, o_ref):
    rg = rg_ref[...]
    lo = jnp.zeros((FINAL_TILE, PACKED), F32)
    hi = jnp.zeros((FINAL_TILE, PACKED), F32)
    for k in range(TOP_K):
        yl, yh = _unpack_bf16_pair(yg_ref[k])
        w = rg[:, k:k + 1]
        lo = lo + w * yl
        hi = hi + w * yh
    x = h_ref[...] + jnp.concatenate([lo, hi], axis=1)
    o_ref[...] = _rms(x, g_ref[...])


def _final(h2d, yg, rg, g):
    rows = h2d.shape[0]
    tm = FINAL_TILE
    return pl.pallas_call(
        _final_kernel,
        grid=(rows // tm,),
        in_specs=[
            pl.BlockSpec((tm, D_MODEL), lambda i: (i, 0)),
            pl.BlockSpec((TOP_K, tm, PACKED), lambda i: (0, i, 0)),
            pl.BlockSpec((tm, LANES), lambda i: (i, 0)),
            pl.BlockSpec((1, D_MODEL), lambda i: (0, 0)),
        ],
        out_specs=pl.BlockSpec((tm, D_MODEL), lambda i: (i, 0)),
        out_shape=jax.ShapeDtypeStruct((rows, D_MODEL), F32),
        compiler_params=pltpu.CompilerParams(dimension_semantics=("parallel",), vmem_limit_bytes=VMEM_LIMIT),
        name="final",
    )(h2d, yg, rg, g)


def _plan(ri, counts, n_tok):
    tb = EXPERT_BLOCK
    n_slots = n_tok * TOP_K + N_EXPERTS * tb
    eid = ri[:TOP_K]
    rank = ri[TOP_K:]
    c = counts[:, 0]
    pc = (c + tb - 1) // tb * tb
    pend = jnp.cumsum(pc)
    pstart = pend - pc
    base = jnp.zeros_like(eid)
    for e in range(N_EXPERTS):
        base = jnp.where(eid == e, pstart[e], base)
    slots = (base + rank).astype(I32)
    slots = slots.reshape(TOP_K, n_tok // SC_WINDOW, SC_WINDOW).transpose(1, 0, 2)
    blk_start = jnp.arange(n_slots // tb, dtype=I32) * tb
    blk_e = jnp.minimum(jnp.sum(blk_start[:, None] >= pend[None, :], axis=1), N_EXPERTS - 1).astype(I32)
    nused = (pend[-1:] // tb).astype(I32)
    return slots, blk_e, nused, n_slots


def _trunk(x, mem, p):
    batch, seq, _ = x.shape
    n_tok = batch * seq
    x2d = x.reshape(n_tok, D_MODEL)
    kv = _memkv(mem.reshape(-1, D_MODEL), p["norm_mem_g"], p["w_mem_kv"])
    kv3 = kv.reshape(batch, mem.shape[1], 2 * CROSS_WIDTH)
    u, a0, a1, a2, qc, gate = _proj(x2d, p["norm_mix_g"], p["w_in"], _rope_tables(seq), seq)
    outs, lses = [], []
    for g, a_grp in enumerate((a0, a1, a2)):
        o, lse = _dilattn(a_grp, batch, seq, g)
        outs.append(o)
        lses.append(lse)
    h = _mix(x2d, u, qc, gate, outs, lses, kv3, p["w_pool"], p["pool_scale"], p["w_br_pool"], p["w_br_attn"],
             p["w_br_cross"], p["w_out"], batch, seq)
    hn, ri, rg, counts = _route(h, p["norm_ffn_g"], p["w_router_hi"], p["w_router_lo"], p["b_router"])
    slots, blk_e, nused, n_slots = _plan(ri, counts, n_tok)
    xs = _sc_dispatch(hn, slots, n_slots)
    ys = _experts(xs, blk_e, nused, p["w_up"], p["b_up"], p["w_down"], p["b_down"])
    yg = _sc_combine(ys, slots)
    out = _final(h, yg, rg, p["norm_final_g"])
    return out.reshape(batch, seq, D_MODEL)


def _prep_params(norm_mix_g, norm_mem_g, w_in, w_pool, pool_scale, w_mem_kv, w_br_pool, w_br_attn, w_br_cross,
                 w_out, norm_ffn_g, w_router, b_router, w_up, b_up, w_down, b_down, norm_final_g):
    wr = w_router[0]
    wr_hi = wr.astype(BF16)
    return dict(
        norm_mix_g=norm_mix_g[0].reshape(1, D_MODEL),
        norm_mem_g=norm_mem_g[0].reshape(1, D_MODEL),
        w_in=w_in[0].astype(BF16),
        w_pool=w_pool[0].astype(BF16),
        pool_scale=pool_scale[0].reshape(1, POOL_WIDTH),
        w_mem_kv=w_mem_kv[0].astype(BF16),
        w_br_pool=w_br_pool[0].astype(BF16),
        w_br_attn=w_br_attn[0].astype(BF16),
        w_br_cross=w_br_cross[0].astype(BF16),
        w_out=w_out[0].astype(BF16),
        norm_ffn_g=norm_ffn_g[0].reshape(1, D_MODEL),
        w_router_hi=jnp.pad(wr_hi, ((0, 0), (0, LANES - N_EXPERTS))),
        w_router_lo=jnp.pad((wr - wr_hi.astype(F32)).astype(BF16), ((0, 0), (0, LANES - N_EXPERTS))),
        b_router=jnp.pad(b_router[0].reshape(1, N_EXPERTS), ((0, 0), (0, LANES - N_EXPERTS))),
        w_up=w_up[0],
        b_up=b_up[0].reshape(N_EXPERTS, 1, 2 * D_FF),
        w_down=w_down[0],
        b_down=b_down[0].reshape(N_EXPERTS, 1, D_MODEL),
        norm_final_g=norm_final_g.reshape(1, D_MODEL),
    )


def kernel(x_prompt, x_sample, mem_prompt, mem_sample, norm_mix_g, norm_mem_g, w_in, w_pool, pool_scale, w_mem_kv,
           w_br_pool, w_br_attn, w_br_cross, w_out, norm_ffn_g, w_router, b_router, w_up, b_up, w_down, b_down,
           norm_final_g):
    p = _prep_params(norm_mix_g, norm_mem_g, w_in, w_pool, pool_scale, w_mem_kv, w_br_pool, w_br_attn, w_br_cross,
                     w_out, norm_ffn_g, w_router, b_router, w_up, b_up, w_down, b_down, norm_final_g)
    y_prompt = _trunk(x_prompt, mem_prompt, p)
    y_sample = _trunk(x_sample, mem_sample, p)
    return (y_prompt, y_sample)
```
